```python
import math
import jax, jax.numpy as jnp
from jax import lax
import numpy as np

D_MODEL = 1024
BATCH = 8
SEQ = 4096
DEPTH = 4
DEC_BATCH = 16
DEC_SEQ = 2048
PAST_LEN = 128

HEAD_DIM = 64
N_ATT_HEADS = 8
N_KV_HEADS = 2
ATT_WIDTH = N_ATT_HEADS * HEAD_DIM
KV_WIDTH = N_KV_HEADS * HEAD_DIM
WINDOW = 128
BLOCK = 128
ROT_DIM = HEAD_DIM // 4
ROPE_THETA = 500000.0
MLSTM_WIDTH = D_MODEL - ATT_WIDTH
N_MLSTM_HEADS = 4
MLSTM_HEAD_DIM = MLSTM_WIDTH // N_MLSTM_HEADS
CHUNK = 128
CONV_WIDTH = 3
MIX_WIDTH = ATT_WIDTH + MLSTM_WIDTH
OFF_AQ = 0
OFF_AK = OFF_AQ + ATT_WIDTH
OFF_AV = OFF_AK + KV_WIDTH
OFF_MQ = OFF_AV + KV_WIDTH
OFF_MK = OFF_MQ + MLSTM_WIDTH
OFF_MV = OFF_MK + MLSTM_WIDTH
OFF_MO = OFF_MV + MLSTM_WIDTH
OFF_G = OFF_MO + MLSTM_WIDTH
N_GATE_COLS = 4 * N_MLSTM_HEADS
IN_WIDTH = OFF_G + N_GATE_COLS
N_MEM = 256
N_X_HEADS = 4
X_HEAD_DIM = D_MODEL // N_X_HEADS
N_EXPERTS = 16
N_GROUPS = 4
EXPERTS_PER_GROUP = N_EXPERTS // N_GROUPS
TOP_K = 2
D_FF_EXPERT = 512
MOE_BLOCK = 128
ALPHA = (2.0 * DEPTH) ** 0.25
BETA = (8.0 * DEPTH) ** -0.25
LN_EPS = 1e-5
RMS_EPS = 1e-6
NEG = -1e30

kernel_name = 'hymba_swa_mlstm_shared_router_moe_encoder'


def layer_norm(x, g, b):
    xf = x.astype(jnp.float32)
    mu = jnp.mean(xf, axis=-1, keepdims=True)
    var = jnp.mean(jnp.square(xf - mu), axis=-1, keepdims=True)
    y = (xf - mu) * lax.rsqrt(var + LN_EPS) * g.astype(jnp.float32) + b.astype(jnp.float32)
    return y.astype(x.dtype)


def rms_norm(x, g):
    xf = x.astype(jnp.float32)
    return xf * lax.rsqrt(jnp.mean(jnp.square(xf), axis=-1, keepdims=True) + RMS_EPS) * g.astype(jnp.float32)


def partial_rope(x, pos):
    inv_freq = ROPE_THETA ** (-jnp.arange(0, ROT_DIM, 2, dtype=jnp.float32) / ROT_DIM)
    ang = pos.astype(jnp.float32)[:, None] * inv_freq[None, :]
    cos = jnp.cos(ang)[None, :, None, :]
    sin = jnp.sin(ang)[None, :, None, :]
    xr = x[..., :ROT_DIM].astype(jnp.float32)
    x1, x2 = xr[..., :ROT_DIM // 2], xr[..., ROT_DIM // 2:]
    rot = jnp.concatenate([x1 * cos - x2 * sin, x2 * cos + x1 * sin], axis=-1).astype(x.dtype)
    return jnp.concatenate([rot, x[..., ROT_DIM:]], axis=-1)


def windowed_gqa(q, k, v, sink):
    B, S = q.shape[0], q.shape[1]
    nb = S // BLOCK
    G = N_ATT_HEADS // N_KV_HEADS
    qb = q.reshape(B, nb, BLOCK, N_KV_HEADS, G, HEAD_DIM)

    def band(t):
        tp = jnp.pad(t.reshape(B, nb, BLOCK, N_KV_HEADS, HEAD_DIM), ((0, 0), (1, 1), (0, 0), (0, 0), (0, 0)))
        return jnp.concatenate([tp[:, :-2], tp[:, 1:-1], tp[:, 2:]], axis=2)

    kw, vw = band(k), band(v)
    s = jnp.einsum('bnqhgd,bnkhd->bnhgqk', qb, kw, preferred_element_type=jnp.float32) * (1.0 / math.sqrt(HEAD_DIM))
    q_off = jnp.arange(BLOCK)[:, None]
    k_off = jnp.arange(3 * BLOCK)[None, :] - BLOCK
    in_band = jnp.abs(k_off - q_off) <= WINDOW
    k_abs = jnp.arange(nb)[:, None] * BLOCK + k_off
    in_seq = (k_abs >= 0) & (k_abs < S)
    mask = in_band[None, :, :] & in_seq[:, None, :]
    s = jnp.where(mask[None, :, None, None], s, NEG)
    sink_l = sink.astype(jnp.float32).reshape(N_KV_HEADS, G)[None, None, :, :, None, None]
    m = jnp.maximum(jnp.max(s, axis=-1, keepdims=True), sink_l)
    p = jnp.exp(s - m)
    denom = jnp.sum(p, axis=-1, keepdims=True) + jnp.exp(sink_l - m)
    o = jnp.einsum('bnhgqk,bnkhd->bnqhgd', (p / denom).astype(v.dtype), vw)
    return o.reshape(B, S, ATT_WIDTH)


def mlstm_chunkwise(q, k, v, ig, lf):
    B, H, S, D = q.shape
    nc = S // CHUNK
    qc = q.reshape(B, H, nc, CHUNK, D)
    kc = k.reshape(B, H, nc, CHUNK, D)
    vc = v.reshape(B, H, nc, CHUNK, D)
    igc = ig.reshape(B, H, nc, CHUNK)
    b = jnp.cumsum(lf.reshape(B, H, nc, CHUNK), axis=-1)
    a = b[..., -1]
    w_log = a[..., None] - b + igc
    g = jnp.max(w_log, axis=-1)
    wk = jnp.exp(w_log - g[..., None])[..., None] * kc
    c_chunk = jnp.einsum('bhnld,bhnle->bhnde', wk, vc)
    n_chunk = jnp.sum(wk, axis=-2)

    def step(carry, inp):
        C, n, m = carry
        a_c, g_c, Cc, nc_ = inp
        m_new = jnp.maximum(a_c + m, g_c)
        f_s = jnp.exp(a_c + m - m_new)
        i_s = jnp.exp(g_c - m_new)
        C_new = f_s[..., None, None] * C + i_s[..., None, None] * Cc
        n_new = f_s[..., None] * n + i_s[..., None] * nc_
        return (C_new, n_new, m_new), (C, n, m)

    init = (jnp.zeros((B, H, D, D), jnp.float32), jnp.zeros((B, H, D), jnp.float32), jnp.zeros((B, H), jnp.float32))
    xs = (jnp.moveaxis(a, 2, 0), jnp.moveaxis(g, 2, 0), jnp.moveaxis(c_chunk, 2, 0), jnp.moveaxis(n_chunk, 2, 0))
    _, (C_in, n_in, m_in) = lax.scan(step, init, xs)
    C_in = jnp.moveaxis(C_in, 0, 2)
    n_in = jnp.moveaxis(n_in, 0, 2)
    m_in = jnp.moveaxis(m_in, 0, 2)
    causal = jnp.tril(jnp.ones((CHUNK, CHUNK), dtype=bool))
    log_d = jnp.where(causal, b[..., :, None] - b[..., None, :] + igc[..., None, :], NEG)
    inter_log = b + m_in[..., None]
    m_t = jnp.maximum(inter_log, jnp.max(log_d, axis=-1))
    dmat = jnp.exp(log_d - m_t[..., None])
    inter_w = jnp.exp(inter_log - m_t)
    sw = jnp.einsum('bhnld,bhnsd->bhnls', qc, kc) * dmat
    num = jnp.einsum('bhnls,bhnsd->bhnld', sw, vc) + inter_w[..., None] * jnp.einsum('bhnld,bhnde->bhnle', qc, C_in)
    den = jnp.sum(sw, axis=-1) + inter_w * jnp.einsum('bhnld,bhnd->bhnl', qc, n_in)
    h = num / jnp.maximum(jnp.abs(den), jnp.exp(-m_t))[..., None]
    return h.reshape(B, H, S, D)


def centred_dwconv(x, w, b):
    C = x.shape[-1]
    y = lax.conv_general_dilated(x, w.reshape(CONV_WIDTH, 1, C).astype(x.dtype), window_strides=(1,),
                                 padding=[(CONV_WIDTH // 2, CONV_WIDTH // 2)],
                                 dimension_numbers=('NWC', 'WIO', 'NWC'), feature_group_count=C)
    return y + b.astype(x.dtype)


def mlstm_group(u_qk, u_v, u_o, u_gates, conv_w, conv_b, gate_bias, norm_g):
    B, S = u_qk.shape[0], u_qk.shape[1]
    qk = jax.nn.silu(centred_dwconv(u_qk, conv_w, conv_b))

    def heads(t):
        return t.reshape(B, S, N_MLSTM_HEADS, MLSTM_HEAD_DIM).transpose(0, 2, 1, 3).astype(jnp.float32)

    q = heads(qk[..., :MLSTM_WIDTH])
    k = heads(qk[..., MLSTM_WIDTH:]) * (MLSTM_HEAD_DIM ** -0.5)
    v = heads(u_v)
    gts = (u_gates + gate_bias).astype(jnp.float32).transpose(0, 2, 1)
    i_f, f_f, i_b, f_b = jnp.split(gts, 4, axis=1)
    h_f = mlstm_chunkwise(q, k, v, i_f, jax.nn.log_sigmoid(f_f))
    fl = lambda t: jnp.flip(t, axis=2)
    h_b = fl(mlstm_chunkwise(fl(q), fl(k), fl(v), fl(i_b), fl(jax.nn.log_sigmoid(f_b))))
    h = (h_f + h_b).transpose(0, 2, 1, 3)
    h = rms_norm(h, norm_g.reshape(N_MLSTM_HEADS, MLSTM_HEAD_DIM)).reshape(B, S, MLSTM_WIDTH)
    return (h * jax.nn.sigmoid(u_o.astype(jnp.float32))).astype(u_o.dtype)


def parallel_mixer(x, pos, w_in, gate_bias, conv_w, conv_b, sink, att_g, mlstm_g, w_out):
    B, S, _ = x.shape
    u = x @ w_in
    aq = partial_rope(u[..., OFF_AQ:OFF_AK].reshape(B, S, N_ATT_HEADS, HEAD_DIM), pos)
    ak = partial_rope(u[..., OFF_AK:OFF_AV].reshape(B, S, N_KV_HEADS, HEAD_DIM), pos)
    av = u[..., OFF_AV:OFF_MQ].reshape(B, S, N_KV_HEADS, HEAD_DIM)
    y_att = rms_norm(windowed_gqa(aq, ak, av, sink), att_g).astype(x.dtype)
    y_ml = mlstm_group(u[..., OFF_MQ:OFF_MV], u[..., OFF_MV:OFF_MO], u[..., OFF_MO:OFF_G], u[..., OFF_G:],
                       conv_w, conv_b, gate_bias, mlstm_g)
    return jnp.concatenate([y_att, y_ml], axis=-1) @ w_out


def memory_cross_attention(x, mem, wq, wkv, wo):
    B, S, _ = x.shape
    M = mem.shape[1]
    q = (x @ wq).reshape(B, S, N_X_HEADS, X_HEAD_DIM)
    kv = mem @ wkv
    k = kv[..., :D_MODEL].reshape(B, M, N_X_HEADS, X_HEAD_DIM)
    v = kv[..., D_MODEL:].reshape(B, M, N_X_HEADS, X_HEAD_DIM)
    s = jnp.einsum('bqhd,bkhd->bhqk', q, k, preferred_element_type=jnp.float32) * (1.0 / math.sqrt(X_HEAD_DIM))
    p = jax.nn.softmax(s, axis=-1).astype(v.dtype)
    o = jnp.einsum('bhqk,bkhd->bqhd', p, v).reshape(B, S, D_MODEL)
    return o @ wo


def route(x2, router_w, router_bias):
    T = x2.shape[0]
    s = jax.nn.sigmoid((x2 @ router_w).astype(jnp.float32))
    sel = (s + router_bias.astype(jnp.float32)).reshape(T, N_GROUPS, EXPERTS_PER_GROUP)
    grp_score = jnp.sum(lax.top_k(sel, TOP_K)[0], axis=-1)
    g_idx = jnp.argmax(grp_score, axis=-1)
    in_group = jnp.take_along_axis(sel, g_idx[:, None, None], axis=1)[:, 0]
    _, local = lax.top_k(in_group, TOP_K)
    e_idx = (g_idx[:, None] * EXPERTS_PER_GROUP + local).astype(jnp.int32)
    w = jnp.take_along_axis(s, e_idx, axis=-1)
    return e_idx, w / jnp.sum(w, axis=-1, keepdims=True)


def grouped_moe(x, router_w, router_bias, w_gate, w_up, w_down):
    B, S, D = x.shape
    T = B * S
    x2 = x.reshape(T, D)
    e_idx, gw = route(x2, router_w, router_bias)
    A = T * TOP_K
    flat_e = e_idx.reshape(A)
    flat_tok = jnp.arange(A, dtype=jnp.int32) // TOP_K
    order = jnp.argsort(flat_e)
    sorted_e = flat_e[order]
    sorted_tok = flat_tok[order]
    counts = jnp.bincount(flat_e, length=N_EXPERTS).astype(jnp.int32)
    padded = (counts + MOE_BLOCK - 1) // MOE_BLOCK * MOE_BLOCK
    pad_end = jnp.cumsum(padded)
    pad_start = pad_end - padded
    start = jnp.cumsum(counts) - counts
    dest = pad_start[sorted_e] + jnp.arange(A, dtype=jnp.int32) - start[sorted_e]
    P = A + N_EXPERTS * MOE_BLOCK
    n_blk = P // MOE_BLOCK
    slot_tok = jnp.full((P,), T, dtype=jnp.int32).at[dest].set(sorted_tok)
    xp = jnp.concatenate([x2, jnp.zeros((1, D), x2.dtype)], axis=0)[slot_tok]
    blk_e = jnp.minimum(jnp.searchsorted(pad_end, jnp.arange(n_blk, dtype=jnp.int32) * MOE_BLOCK, side='right'),
                        N_EXPERTS - 1).astype(jnp.int32)

    def expert_block(args):
        xb, e = args
        h = jax.nn.silu(xb @ w_gate[e]) * (xb @ w_up[e])
        return h @ w_down[e]

    yp = lax.map(expert_block, (xp.reshape(n_blk, MOE_BLOCK, D), blk_e)).reshape(P, D)
    y_assign = yp[dest] * gw.reshape(A)[order][:, None].astype(yp.dtype)
    y = jax.ops.segment_sum(y_assign, sorted_tok, num_segments=T)
    return y.reshape(B, S, D)


def trunk(x, mem, weights):
    (w_in, gate_bias, conv_w, conv_b, attn_sink, attn_norm_g, mlstm_norm_g, w_out, ln1_g, ln1_b,
     wq_mem, wkv_mem, wo_mem, ln2_g, ln2_b, router_w, router_bias, w_gate, w_up, w_down, ln3_g, ln3_b) = weights
    pos = jnp.arange(x.shape[1], dtype=jnp.int32)
    for l in range(DEPTH):
        y = parallel_mixer(x, pos, w_in[l], gate_bias[l], conv_w[l], conv_b[l], attn_sink[l],
                           attn_norm_g[l], mlstm_norm_g[l], w_out[l])
        x = layer_norm(ALPHA * x + y, ln1_g[l], ln1_b[l])
        y = memory_cross_attention(x, mem, wq_mem[l], wkv_mem[l], wo_mem[l])
        x = layer_norm(ALPHA * x + y, ln2_g[l], ln2_b[l])
        y = grouped_moe(x, router_w, router_bias, w_gate[l], w_up[l], w_down[l])
        x = layer_norm(ALPHA * x + y, ln3_g[l], ln3_b[l])
    return x


def setup_inputs(seed: int = 0) -> dict:
    key = jax.random.key(seed)
    ks = jax.random.split(key, 32)
    f32 = jnp.float32
    nrm = lambda k, shape, scale: jax.random.normal(k, shape, f32) * scale
    ig_b = nrm(ks[4], (DEPTH, 2, 1, N_MLSTM_HEADS), 0.1)
    fg_b = jnp.linspace(3.0, 6.0, N_MLSTM_HEADS, dtype=f32)[None, None, None, :] + nrm(ks[5], (DEPTH, 2, 1, N_MLSTM_HEADS), 0.1)
    gate_bias = jnp.concatenate([ig_b, fg_b], axis=2).reshape(DEPTH, N_GATE_COLS)
    return {
        'x_prompt': nrm(ks[0], (BATCH, SEQ, D_MODEL), 1.0),
        'x_sample': nrm(ks[1], (DEC_BATCH, DEC_SEQ, D_MODEL), 1.0),
        'mem_prompt': nrm(ks[2], (BATCH, N_MEM, D_MODEL), 1.0),
        'mem_sample': nrm(ks[3], (DEC_BATCH, N_MEM, D_MODEL), 1.0),
        'w_in': nrm(ks[6], (DEPTH, D_MODEL, IN_WIDTH), D_MODEL ** -0.5),
        'gate_bias': gate_bias,
        'conv_w': nrm(ks[7], (DEPTH, CONV_WIDTH, 2 * MLSTM_WIDTH), CONV_WIDTH ** -0.5),
        'conv_b': nrm(ks[8], (DEPTH, 2 * MLSTM_WIDTH), 0.02),
        'attn_sink': nrm(ks[9], (DEPTH, N_ATT_HEADS), 0.5),
        'attn_norm_g': 1.0 + nrm(ks[10], (DEPTH, ATT_WIDTH), 0.02),
        'mlstm_norm_g': 1.0 + nrm(ks[11], (DEPTH, MLSTM_WIDTH), 0.02),
        'w_out': nrm(ks[12], (DEPTH, MIX_WIDTH, D_MODEL), MIX_WIDTH ** -0.5 * BETA),
        'ln1_g': 1.0 + nrm(ks[13], (DEPTH, D_MODEL), 0.02),
        'ln1_b': nrm(ks[14], (DEPTH, D_MODEL), 0.02),
        'wq_mem': nrm(ks[15], (DEPTH, D_MODEL, D_MODEL), D_MODEL ** -0.5),
        'wkv_mem': nrm(ks[16], (DEPTH, D_MODEL, 2 * D_MODEL), D_MODEL ** -0.5),
        'wo_mem': nrm(ks[17], (DEPTH, D_MODEL, D_MODEL), D_MODEL ** -0.5 * BETA),
        'ln2_g': 1.0 + nrm(ks[18], (DEPTH, D_MODEL), 0.02),
        'ln2_b': nrm(ks[19], (DEPTH, D_MODEL), 0.02),
        'router_w': nrm(ks[20], (D_MODEL, N_EXPERTS), D_MODEL ** -0.5),
        'router_bias': nrm(ks[21], (N_EXPERTS,), 0.01),
        'w_gate': nrm(ks[22], (DEPTH, N_EXPERTS, D_MODEL, D_FF_EXPERT), D_MODEL ** -0.5),
        'w_up': nrm(ks[23], (DEPTH, N_EXPERTS, D_MODEL, D_FF_EXPERT), D_MODEL ** -0.5),
        'w_down': nrm(ks[24], (DEPTH, N_EXPERTS, D_FF_EXPERT, D_MODEL), D_FF_EXPERT ** -0.5 * BETA),
        'ln3_g': 1.0 + nrm(ks[25], (DEPTH, D_MODEL), 0.02),
        'ln3_b': nrm(ks[26], (DEPTH, D_MODEL), 0.02),
    }


def reference(x_prompt, x_sample, mem_prompt, mem_sample, w_in, gate_bias, conv_w, conv_b, attn_sink,
              attn_norm_g, mlstm_norm_g, w_out, ln1_g, ln1_b, wq_mem, wkv_mem, wo_mem, ln2_g, ln2_b,
              router_w, router_bias, w_gate, w_up, w_down, ln3_g, ln3_b):
    weights = (w_in, gate_bias, conv_w, conv_b, attn_sink, attn_norm_g, mlstm_norm_g, w_out, ln1_g, ln1_b,
               wq_mem, wkv_mem, wo_mem, ln2_g, ln2_b, router_w, router_bias, w_gate, w_up, w_down, ln3_g, ln3_b)
    y_prompt = trunk(x_prompt, mem_prompt, weights)
    y_sample = trunk(x_sample, mem_sample, weights)
    return (y_prompt, y_sample)
```

```python
import functools
import math

import jax
import jax.numpy as jnp
from jax import lax
from jax.experimental import pallas as pl
from jax.experimental.pallas import tpu as pltpu

F32 = jnp.float32
BF16 = jnp.bfloat16

D_MODEL = 1024
DEPTH = 4
HEAD_DIM = 64
N_ATT_HEADS = 8
N_KV_HEADS = 2
ATT_WIDTH = N_ATT_HEADS * HEAD_DIM
KV_WIDTH = N_KV_HEADS * HEAD_DIM
BLOCK = 128
ROT_DIM = HEAD_DIM // 4
ROPE_THETA = 500000.0
MLSTM_WIDTH = D_MODEL - ATT_WIDTH
N_MLSTM_HEADS = 4
MLSTM_HEAD_DIM = MLSTM_WIDTH // N_MLSTM_HEADS
CHUNK = 128
OFF_AQ = 0
OFF_AK = OFF_AQ + ATT_WIDTH
OFF_AV = OFF_AK + KV_WIDTH
OFF_MQ = OFF_AV + KV_WIDTH
OFF_MV = OFF_MQ + 2 * MLSTM_WIDTH
OFF_MO = OFF_MV + MLSTM_WIDTH
OFF_G = OFF_MO + MLSTM_WIDTH
N_GATE_COLS = 4 * N_MLSTM_HEADS
N_MEM = 256
N_X_HEADS = 4
X_HEAD_DIM = D_MODEL // N_X_HEADS
N_EXPERTS = 16
N_GROUPS = 4
EXPERTS_PER_GROUP = N_EXPERTS // N_GROUPS
TOP_K = 2
D_FF_EXPERT = 512
ALPHA = (2.0 * DEPTH) ** 0.25
LN_EPS = 1e-5
RMS_EPS = 1e-6
NEG = -1e30

LANES = 128
TOKEN_TILE = 512
ROW_BLOCK = 512
COPY_ROWS = 512
VMEM_LIMIT = 56 * 1024 * 1024


def _params(*sem):
    return pltpu.CompilerParams(dimension_semantics=sem, vmem_limit_bytes=VMEM_LIMIT)


def _layer_norm(z, g, b):
    mu = jnp.mean(z, axis=-1, keepdims=True)
    zc = z - mu
    var = jnp.mean(zc * zc, axis=-1, keepdims=True)
    return zc * lax.rsqrt(var + LN_EPS) * g + b


def _dot(a, b):
    return jnp.dot(a, b, preferred_element_type=F32)


def _dot_nt(a, b):
    return lax.dot_general(a, b, (((1,), (1,)), ((), ())), preferred_element_type=F32)


def _dot_tn(a, b):
    return lax.dot_general(a, b, (((0,), (0,)), ((), ())), preferred_element_type=F32)


def _in_proj_kernel(x_ref, w_ref, wgt_ref, gb_ref, cos_ref, sa_ref, sb_ref, tri_u_ref, tri_l_ref,
                    qa_ref, ka_ref, va_ref, mqk_ref, mv_ref, mo_ref, gp_ref):
    xb = x_ref[...].astype(BF16)
    cos = cos_ref[...]
    sa = sa_ref[...]
    sb = sb_ref[...]

    def mm(lo, hi):
        return _dot(xb, w_ref[:, lo:hi])

    def rope(t):
        return t * cos + pltpu.roll(t, LANES - ROT_DIM // 2, 1) * sa + pltpu.roll(t, ROT_DIM // 2, 1) * sb

    q = mm(OFF_AQ, OFF_AK)
    scale = 1.0 / math.sqrt(HEAD_DIM)
    for j in range(ATT_WIDTH // LANES):
        qa_ref[:, j * LANES:(j + 1) * LANES] = (rope(q[:, j * LANES:(j + 1) * LANES]) * scale).astype(BF16)
    ka_ref[...] = rope(mm(OFF_AK, OFF_AV)).astype(BF16)
    va_ref[...] = mm(OFF_AV, OFF_MQ).astype(BF16)
    mqk_ref[...] = mm(OFF_MQ, OFF_MV).astype(BF16)
    mv_ref[...] = mm(OFF_MV, OFF_MO).astype(BF16)
    mo_ref[...] = mm(OFF_MO, OFF_G).astype(BF16)

    gt = _dot_nt(wgt_ref[...], xb) + gb_ref[...]
    ls = jnp.minimum(gt, 0.0) - jnp.log1p(jnp.exp(-jnp.abs(gt)))
    n_chunks = gt.shape[1] // CHUNK
    ls_rows = jnp.concatenate([ls[:, c * CHUNK:(c + 1) * CHUNK] for c in range(n_chunks)], axis=0)
    pre = jnp.dot(ls_rows, tri_u_ref[...], precision=lax.Precision.HIGHEST, preferred_element_type=F32)
    suf = jnp.dot(ls_rows, tri_l_ref[...], precision=lax.Precision.HIGHEST, preferred_element_type=F32)
    row = lax.broadcasted_iota(jnp.int32, (N_GATE_COLS, CHUNK), 0)
    for c in range(n_chunks):
        raw = gt[:, c * CHUNK:(c + 1) * CHUNK]
        p = pre[c * N_GATE_COLS:(c + 1) * N_GATE_COLS]
        s = suf[c * N_GATE_COLS:(c + 1) * N_GATE_COLS]
        is_f_fwd = (row >= N_MLSTM_HEADS) & (row < 2 * N_MLSTM_HEADS)
        is_f_bwd = row >= 3 * N_MLSTM_HEADS
        gp_ref[c] = jnp.where(is_f_fwd, p, jnp.where(is_f_bwd, s, raw))


def _in_proj(x, w_main, wgt, gb, rope_tabs, tri_u, tri_l, seq):
    T = x.shape[0]
    tm = TOKEN_TILE
    n_seq_tiles = seq // tm
    cos_t, sa_t, sb_t = rope_tabs
    row_spec = lambda w: pl.BlockSpec((tm, w), lambda i: (i, 0))
    full = lambda a: pl.BlockSpec(a.shape, lambda i: (0,) * a.ndim)
    tab_spec = pl.BlockSpec((tm, LANES), lambda i: (i % n_seq_tiles, 0))
    out_shapes = (
        jax.ShapeDtypeStruct((T, ATT_WIDTH), BF16),
        jax.ShapeDtypeStruct((T, KV_WIDTH), BF16),
        jax.ShapeDtypeStruct((T, KV_WIDTH), BF16),
        jax.ShapeDtypeStruct((T, 2 * MLSTM_WIDTH), BF16),
        jax.ShapeDtypeStruct((T, MLSTM_WIDTH), BF16),
        jax.ShapeDtypeStruct((T, MLSTM_WIDTH), BF16),
        jax.ShapeDtypeStruct((T // CHUNK, N_GATE_COLS, CHUNK), F32),
    )
    out_specs = (
        row_spec(ATT_WIDTH), row_spec(KV_WIDTH), row_spec(KV_WIDTH), row_spec(2 * MLSTM_WIDTH),
        row_spec(MLSTM_WIDTH), row_spec(MLSTM_WIDTH),
        pl.BlockSpec((tm // CHUNK, N_GATE_COLS, CHUNK), lambda i: (i, 0, 0)),
    )
    return pl.pallas_call(
        _in_proj_kernel,
        grid=(T // tm,),
        in_specs=[row_spec(D_MODEL), full(w_main), full(wgt), full(gb), tab_spec, tab_spec, tab_spec,
                  full(tri_u), full(tri_l)],
        out_specs=out_specs,
        out_shape=out_shapes,
        compiler_params=_params("parallel"),
        name="in_proj",
    )(x, w_main, wgt, gb, cos_t, sa_t, sb_t, tri_u, tri_l)


ATT_Q_TILE = 512
_Q_BLOCKS = ATT_Q_TILE // BLOCK


def _attn_kernel(sink_ref, q_ref, kp_ref, k_ref, kn_ref, vp_ref, v_ref, vn_ref, g_ref, o_ref, *, n_tiles):
    i = pl.program_id(1)
    lane = lax.broadcasted_iota(jnp.int32, (ATT_Q_TILE + 2 * BLOCK, LANES), 1)
    low = lane < HEAD_DIM

    def split(prev_ref, own_ref, next_ref):
        t = jnp.concatenate([prev_ref[...], own_ref[...], next_ref[...]], axis=0).astype(F32)
        r = pltpu.roll(t, HEAD_DIM, 1)
        zero = jnp.zeros_like(t)
        lo = (jnp.where(low, t, zero).astype(BF16), jnp.where(low, r, zero).astype(BF16))
        hi = (jnp.where(low, zero, r).astype(BF16), jnp.where(low, zero, t).astype(BF16))
        return lo, hi

    k_lo, k_hi = split(kp_ref, k_ref, kn_ref)
    v_lo, v_hi = split(vp_ref, v_ref, vn_ref)

    rowi = lax.broadcasted_iota(jnp.int32, (BLOCK, 3 * BLOCK), 0)
    coli = lax.broadcasted_iota(jnp.int32, (BLOCK, 3 * BLOCK), 1)
    out_of_band = (coli < rowi) | (coli - 2 * BLOCK > rowi)
    band_bias = jnp.where(out_of_band, NEG, 0.0).astype(F32)
    first_bias = jnp.where(i == 0, NEG, 0.0).astype(F32)
    last_bias = jnp.where(i == n_tiles - 1, NEG, 0.0).astype(F32)
    bias_first = band_bias + jnp.where(coli < BLOCK, first_bias, 0.0)
    bias_last = band_bias + jnp.where(coli >= 2 * BLOCK, last_bias, 0.0)
    lane_o = lax.broadcasted_iota(jnp.int32, (BLOCK, LANES), 1)
    g = g_ref[...]

    for r in range(_Q_BLOCKS):
        bias = bias_first if r == 0 else (bias_last if r == _Q_BLOCKS - 1 else band_bias)
        rows = slice(r * BLOCK, (r + 1) * BLOCK)
        win = slice(r * BLOCK, (r + 3) * BLOCK)
        tiles = []
        for c in range(N_KV_HEADS):
            q2 = jnp.concatenate([q_ref[rows, (2 * c) * LANES:(2 * c + 1) * LANES],
                                  q_ref[rows, (2 * c + 1) * LANES:(2 * c + 2) * LANES]], axis=0)
            kc = jnp.concatenate([k_lo[c][win], k_hi[c][win]], axis=0)
            vc = jnp.concatenate([v_lo[c][win], v_hi[c][win]], axis=0)
            s = _dot_nt(q2, kc)
            p_rows, inv_rows = [], []
            for t in range(2):
                ps, invs = [], []
                for hh in range(2):
                    head = 4 * c + 2 * t + hh
                    sink = sink_ref[head]
                    sh = s[t * BLOCK:(t + 1) * BLOCK, hh * 3 * BLOCK:(hh + 1) * 3 * BLOCK] + bias
                    m = jnp.maximum(jnp.max(sh, axis=-1, keepdims=True), sink)
                    p = jnp.exp(sh - m)
                    den = jnp.sum(p, axis=-1, keepdims=True) + jnp.exp(sink - m)
                    ps.append(p.astype(BF16))
                    invs.append(1.0 / den)
                p_rows.append(jnp.concatenate(ps, axis=1))
                inv_rows.append(jnp.where(lane_o < HEAD_DIM, invs[0], invs[1]))
            o2 = _dot(jnp.concatenate(p_rows, axis=0), vc)
            tiles.append(o2[:BLOCK] * inv_rows[0])
            tiles.append(o2[BLOCK:] * inv_rows[1])
        o = jnp.concatenate(tiles, axis=1)
        ms = jnp.mean(o * o, axis=-1, keepdims=True)
        o_ref[rows, :] = (o * lax.rsqrt(ms + RMS_EPS) * g).astype(BF16)


def _attention(qa, ka, va, sink, att_g, batch, seq):
    T = qa.shape[0]
    n_tiles = seq // ATT_Q_TILE
    blocks_per_seq = seq // BLOCK
    own = lambda w: pl.BlockSpec((ATT_Q_TILE, w), lambda b, i: (b * n_tiles + i, 0))
    prev = pl.BlockSpec((BLOCK, KV_WIDTH),
                        lambda b, i: (b * blocks_per_seq + jnp.maximum(i * _Q_BLOCKS - 1, 0), 0))
    nxt = pl.BlockSpec((BLOCK, KV_WIDTH),
                       lambda b, i: (b * blocks_per_seq + jnp.minimum((i + 1) * _Q_BLOCKS, blocks_per_seq - 1), 0))
    return pl.pallas_call(
        functools.partial(_attn_kernel, n_tiles=n_tiles),
        grid=(batch, n_tiles),
        in_specs=[pl.BlockSpec(memory_space=pltpu.SMEM), own(ATT_WIDTH), prev, own(KV_WIDTH), nxt,
                  prev, own(KV_WIDTH), nxt, pl.BlockSpec((1, ATT_WIDTH), lambda b, i: (0, 0))],
        out_specs=own(ATT_WIDTH),
        out_shape=jax.ShapeDtypeStruct((T, ATT_WIDTH), BF16),
        compiler_params=_params("parallel", "parallel"),
        name="band_attention",
    )(sink, qa, ka, ka, ka, va, va, va, att_g)


_HALO = 16


def _mlstm_kernel(qk_f, hp_f, hn_f, v_f, gp_f, qk_b, hp_b, hn_b, v_b, gp_b, cw_ref, cb_ref,
                  of_ref, ob_ref, c_state, m_state, *, n_chunks):
    i = pl.program_id(1)

    @pl.when(i == 0)
    def _():
        c_state[...] = jnp.zeros_like(c_state)
        m_state[...] = jnp.zeros_like(m_state)

    rowi = lax.broadcasted_iota(jnp.int32, (CHUNK, CHUNK), 0)
    coli = lax.broadcasted_iota(jnp.int32, (CHUNK, CHUNK), 1)
    row_w = lax.broadcasted_iota(jnp.int32, (CHUNK, 2 * MLSTM_WIDTH), 0)
    ones = jnp.ones((CHUNK, MLSTM_HEAD_DIM), BF16)
    w0 = cw_ref[0:1, :]
    w1 = cw_ref[1:2, :]
    w2 = cw_ref[2:3, :]
    cb = cb_ref[...]

    def direction(d, chunk, qk_ref, hp_ref, hn_ref, v_ref, gp_ref, o_ref):
        fwd = d == 0
        x = qk_ref[...].astype(F32)
        prev_row = jnp.where(chunk > 0, hp_ref[_HALO - 1:_HALO, :].astype(F32), 0.0)
        next_row = jnp.where(chunk < n_chunks - 1, hn_ref[0:1, :].astype(F32), 0.0)
        xp = jnp.where(row_w == 0, prev_row, pltpu.roll(x, 1, 0))
        xn = jnp.where(row_w == CHUNK - 1, next_row, pltpu.roll(x, CHUNK - 1, 0))
        y = w0 * xp + w1 * x + w2 * xn + cb
        y = y * (1.0 / (1.0 + jnp.exp(-y)))
        gates = gp_ref[0]
        gates_t = jnp.concatenate([gates, jnp.zeros((CHUNK - N_GATE_COLS, CHUNK), F32)], axis=0).T
        causal = (coli <= rowi) if fwd else (coli >= rowi)
        a_pos = CHUNK - 1 if fwd else 0
        base = 0 if fwd else 2 * N_MLSTM_HEADS
        for h in range(N_MLSTM_HEADS):
            hs = slice(h * MLSTM_HEAD_DIM, (h + 1) * MLSTM_HEAD_DIM)
            q = y[:, hs].astype(BF16)
            kf = y[:, MLSTM_WIDTH + h * MLSTM_HEAD_DIM:MLSTM_WIDTH + (h + 1) * MLSTM_HEAD_DIM] * (MLSTM_HEAD_DIM ** -0.5)
            v_aug = jnp.concatenate([v_ref[:, hs], ones], axis=1)
            ig_row = gates[base + h:base + h + 1, :]
            b_row = gates[base + N_MLSTM_HEADS + h:base + N_MLSTM_HEADS + h + 1, :]
            ig_col = gates_t[:, base + h:base + h + 1]
            b_col = gates_t[:, base + N_MLSTM_HEADS + h:base + N_MLSTM_HEADS + h + 1]
            a = b_row[:, a_pos:a_pos + 1]
            m_in = m_state[d * N_MLSTM_HEADS + h:d * N_MLSTM_HEADS + h + 1, 0:1]
            c_in = c_state[d, h]
            log_d = jnp.where(causal, b_col + (ig_row - b_row), NEG)
            inter_log = b_col + m_in
            m_t = jnp.maximum(inter_log, jnp.max(log_d, axis=-1, keepdims=True))
            dmat = jnp.exp(log_d - m_t)
            inter_w = jnp.exp(inter_log - m_t)
            sw = (_dot_nt(q, kf.astype(BF16)) * dmat).astype(BF16)
            tot = _dot(sw, v_aug) + inter_w * _dot(q, c_in.astype(BF16))
            den = jnp.maximum(jnp.abs(tot[:, MLSTM_HEAD_DIM:]), jnp.exp(-m_t))
            o_ref[:, hs] = (tot[:, :MLSTM_HEAD_DIM] / den).astype(o_ref.dtype)
            g_max = jnp.max(a - b_row + ig_row, axis=-1, keepdims=True)
            wk = jnp.exp(a - b_col + ig_col - g_max)
            c_chunk = _dot_tn((kf * wk).astype(BF16), v_aug)
            m_new = jnp.maximum(a + m_in, g_max)
            c_state[d, h] = jnp.exp(a + m_in - m_new) * c_in + jnp.exp(g_max - m_new) * c_chunk
            m_state[d * N_MLSTM_HEADS + h:d * N_MLSTM_HEADS + h + 1, :] = jnp.broadcast_to(m_new, (1, LANES))

    direction(0, i, qk_f, hp_f, hn_f, v_f, gp_f, of_ref)
    direction(1, n_chunks - 1 - i, qk_b, hp_b, hn_b, v_b, gp_b, ob_ref)


def _mlstm(mqk, mv, gp, conv_w, conv_b, batch, seq):
    T = mqk.shape[0]
    nc = seq // CHUNK
    halo_per_chunk = CHUNK // _HALO
    n_halo = T // _HALO

    def specs(chunk_of):
        blk = lambda b, i: b * nc + chunk_of(i)
        return [
            pl.BlockSpec((CHUNK, 2 * MLSTM_WIDTH), lambda b, i: (blk(b, i), 0)),
            pl.BlockSpec((_HALO, 2 * MLSTM_WIDTH), lambda b, i: (jnp.maximum(blk(b, i) * halo_per_chunk - 1, 0), 0)),
            pl.BlockSpec((_HALO, 2 * MLSTM_WIDTH),
                         lambda b, i: (jnp.minimum((blk(b, i) + 1) * halo_per_chunk, n_halo - 1), 0)),
            pl.BlockSpec((CHUNK, MLSTM_WIDTH), lambda b, i: (blk(b, i), 0)),
            pl.BlockSpec((1, N_GATE_COLS, CHUNK), lambda b, i: (blk(b, i), 0, 0)),
        ]

    fwd_chunk = lambda i: i
    bwd_chunk = lambda i: nc - 1 - i
    out = jax.ShapeDtypeStruct((T, MLSTM_WIDTH), BF16)
    return pl.pallas_call(
        functools.partial(_mlstm_kernel, n_chunks=nc),
        grid=(batch, nc),
        in_specs=specs(fwd_chunk) + specs(bwd_chunk) + [
            pl.BlockSpec(conv_w.shape, lambda b, i: (0, 0)), pl.BlockSpec(conv_b.shape, lambda b, i: (0, 0))],
        out_specs=(pl.BlockSpec((CHUNK, MLSTM_WIDTH), lambda b, i: (b * nc + i, 0)),
                   pl.BlockSpec((CHUNK, MLSTM_WIDTH), lambda b, i: (b * nc + nc - 1 - i, 0))),
        out_shape=(out, out),
        scratch_shapes=[pltpu.VMEM((2, N_MLSTM_HEADS, MLSTM_HEAD_DIM, 2 * MLSTM_HEAD_DIM), F32),
                        pltpu.VMEM((2 * N_MLSTM_HEADS, LANES), F32)],
        compiler_params=_params("parallel", "arbitrary"),
        name="mlstm",
    )(mqk, mqk, mqk, mv, gp, mqk, mqk, mqk, mv, gp, conv_w, conv_b)


def _out_proj_kernel(x_ref, att_ref, hf_ref, hb_ref, mo_ref, mg_ref, wa_ref, wm_ref, g_ref, b_ref, o_ref):
    h = hf_ref[...].astype(F32) + hb_ref[...].astype(F32)
    mg = mg_ref[...]
    parts = []
    for hd in range(N_MLSTM_HEADS):
        hs = slice(hd * MLSTM_HEAD_DIM, (hd + 1) * MLSTM_HEAD_DIM)
        hh = h[:, hs]
        ms = jnp.mean(hh * hh, axis=-1, keepdims=True)
        parts.append(hh * lax.rsqrt(ms + RMS_EPS) * mg[:, hs])
    hn = jnp.concatenate(parts, axis=1)
    gate = 1.0 / (1.0 + jnp.exp(-mo_ref[...].astype(F32)))
    y = _dot(att_ref[...], wa_ref[...]) + _dot((hn * gate).astype(BF16), wm_ref[...])
    o_ref[...] = _layer_norm(ALPHA * x_ref[...] + y, g_ref[...], b_ref[...])


def _out_proj(x, att, hf, hb, mo, mg, wa, wm, g, b):
    T = x.shape[0]
    tm = TOKEN_TILE
    row = lambda w: pl.BlockSpec((tm, w), lambda i: (i, 0))
    full = lambda a: pl.BlockSpec(a.shape, lambda i: (0,) * a.ndim)
    return pl.pallas_call(
        _out_proj_kernel,
        grid=(T // tm,),
        in_specs=[row(D_MODEL), row(ATT_WIDTH), row(MLSTM_WIDTH), row(MLSTM_WIDTH), row(MLSTM_WIDTH),
                  full(mg), full(wa), full(wm), full(g), full(b)],
        out_specs=row(D_MODEL),
        out_shape=jax.ShapeDtypeStruct((T, D_MODEL), F32),
        compiler_params=_params("parallel"),
        name="out_proj_ln1",
    )(x, att, hf, hb, mo, mg, wa, wm, g, b)


def _kv_proj_kernel(m_ref, w_ref, k_ref, v_ref):
    mb = m_ref[...].astype(BF16)
    k_ref[...] = _dot(mb, w_ref[:, :D_MODEL]).astype(BF16)
    v_ref[...] = _dot(mb, w_ref[:, D_MODEL:]).astype(BF16)


def _kv_proj(mem, wkv):
    M = mem.shape[0]
    tm = TOKEN_TILE
    row = pl.BlockSpec((tm, D_MODEL), lambda i: (i, 0))
    out = jax.ShapeDtypeStruct((M, D_MODEL), BF16)
    return pl.pallas_call(
        _kv_proj_kernel,
        grid=(M // tm,),
        in_specs=[row, pl.BlockSpec(wkv.shape, lambda i: (0, 0))],
        out_specs=(row, row),
        out_shape=(out, out),
        compiler_params=_params("parallel"),
        name="mem_kv_proj",
    )(mem, wkv)


def _xattn_kernel(x_ref, k_ref, v_ref, wq_ref, wo_ref, g_ref, b_ref, o_ref):
    x = x_ref[...]
    q = _dot(x.astype(BF16), wq_ref[...])
    scale = 1.0 / math.sqrt(X_HEAD_DIM)
    outs = []
    for h in range(N_X_HEADS):
        hs = slice(h * X_HEAD_DIM, (h + 1) * X_HEAD_DIM)
        s = _dot_nt((q[:, hs] * scale).astype(BF16), k_ref[:, hs])
        m = jnp.max(s, axis=-1, keepdims=True)
        p = jnp.exp(s - m)
        inv = 1.0 / jnp.sum(p, axis=-1, keepdims=True)
        outs.append((_dot(p.astype(BF16), v_ref[:, hs]) * inv).astype(BF16))
    y = _dot(jnp.concatenate(outs, axis=1), wo_ref[...])
    o_ref[...] = _layer_norm(ALPHA * x + y, g_ref[...], b_ref[...])


def _xattn(x, k_mem, v_mem, wq, wo, g, b, batch, seq):
    T = x.shape[0]
    tm = TOKEN_TILE
    nt = seq // tm
    row = pl.BlockSpec((tm, D_MODEL), lambda bb, i: (bb * nt + i, 0))
    mem = pl.BlockSpec((N_MEM, D_MODEL), lambda bb, i: (bb, 0))
    full = lambda a: pl.BlockSpec(a.shape, lambda bb, i: (0,) * a.ndim)
    return pl.pallas_call(
        _xattn_kernel,
        grid=(batch, nt),
        in_specs=[row, mem, mem, full(wq), full(wo), full(g), full(b)],
        out_specs=row,
        out_shape=jax.ShapeDtypeStruct((T, D_MODEL), F32),
        compiler_params=_params("parallel", "parallel"),
        name="mem_xattn_ln2",
    )(x, k_mem, v_mem, wq, wo, g, b)


_IDX_ROWS = 8


def _router_kernel(x_ref, wt_ref, b_ref, e_ref, w_ref):
    logits = lax.dot_general(wt_ref[...], x_ref[...], (((1,), (1,)), ((), ())),
                             precision=lax.Precision.HIGHEST, preferred_element_type=F32)
    s = 1.0 / (1.0 + jnp.exp(-logits))
    sel = s + b_ref[...]
    srow = lambda e: s[e:e + 1, :]
    brow = lambda e: sel[e:e + 1, :]
    best = None
    gi = None
    for gidx in range(N_GROUPS):
        vals = [brow(gidx * EXPERTS_PER_GROUP + j) for j in range(EXPERTS_PER_GROUP)]
        top2 = None
        for a in range(EXPERTS_PER_GROUP):
            for b in range(a + 1, EXPERTS_PER_GROUP):
                pair = vals[a] + vals[b]
                top2 = pair if top2 is None else jnp.maximum(top2, pair)
        if best is None:
            best, gi = top2, jnp.zeros(top2.shape, jnp.int32)
        else:
            better = top2 > best
            gi = jnp.where(better, gidx, gi)
            best = jnp.where(better, top2, best)

    def in_group(rowfn, j):
        out = rowfn(j)
        for gidx in range(1, N_GROUPS):
            out = jnp.where(gi == gidx, rowfn(gidx * EXPERTS_PER_GROUP + j), out)
        return out

    bv = [in_group(brow, j) for j in range(EXPERTS_PER_GROUP)]
    sv = [in_group(srow, j) for j in range(EXPERTS_PER_GROUP)]

    def argmax_first(vals):
        bi = jnp.zeros(vals[0].shape, jnp.int32)
        bm = vals[0]
        for j in range(1, len(vals)):
            better = vals[j] > bm
            bi = jnp.where(better, j, bi)
            bm = jnp.where(better, vals[j], bm)
        return bi

    i1 = argmax_first(bv)
    i2 = argmax_first([jnp.where(i1 == j, -jnp.inf, bv[j]) for j in range(EXPERTS_PER_GROUP)])

    def pick(vals, idx):
        out = vals[0]
        for j in range(1, len(vals)):
            out = jnp.where(idx == j, vals[j], out)
        return out

    w1 = pick(sv, i1)
    w2 = pick(sv, i2)
    tot = w1 + w2
    zi = jnp.zeros((_IDX_ROWS - TOP_K,) + i1.shape[1:], jnp.int32)
    zf = jnp.zeros((_IDX_ROWS - TOP_K,) + i1.shape[1:], F32)
    e_ref[...] = jnp.concatenate([gi * EXPERTS_PER_GROUP + i1, gi * EXPERTS_PER_GROUP + i2, zi], axis=0)
    w_ref[...] = jnp.concatenate([w1 / tot, w2 / tot, zf], axis=0)


def _router(x, router_wt, router_b):
    T = x.shape[0]
    tm = TOKEN_TILE
    out_spec = pl.BlockSpec((_IDX_ROWS, tm), lambda i: (0, i))
    return pl.pallas_call(
        _router_kernel,
        grid=(T // tm,),
        in_specs=[pl.BlockSpec((tm, D_MODEL), lambda i: (i, 0)),
                  pl.BlockSpec(router_wt.shape, lambda i: (0, 0)), pl.BlockSpec(router_b.shape, lambda i: (0, 0))],
        out_specs=(out_spec, out_spec),
        out_shape=(jax.ShapeDtypeStruct((_IDX_ROWS, T), jnp.int32), jax.ShapeDtypeStruct((_IDX_ROWS, T), F32)),
        compiler_params=_params("parallel"),
        name="router",
    )(x, router_wt, router_b)


_RANK_TILE = 512


def _rank_kernel(e_ref, tri_ref, rank_ref, cnt_ref, carry):
    @pl.when(pl.program_id(0) == 0)
    def _():
        carry[...] = jnp.zeros_like(carry)

    e = e_ref[...]
    onehot = (lax.broadcasted_iota(jnp.int32, (N_EXPERTS, _RANK_TILE), 0) == e).astype(F32)
    before = _dot(onehot.astype(BF16), tri_ref[...])
    c = carry[...]
    rank = jnp.sum(onehot * (before + c[:, 0:1]), axis=0, keepdims=True)
    rank_ref[...] = rank.astype(jnp.int32)
    c = c + jnp.sum(onehot, axis=1, keepdims=True)
    carry[...] = c
    cnt_ref[...] = c.astype(jnp.int32)


def _ranks(e_flat, tri_strict):
    A = e_flat.shape[1]
    return pl.pallas_call(
        _rank_kernel,
        grid=(A // _RANK_TILE,),
        in_specs=[pl.BlockSpec((1, _RANK_TILE), lambda i: (0, i)), pl.BlockSpec(tri_strict.shape, lambda i: (0, 0))],
        out_specs=(pl.BlockSpec((1, _RANK_TILE), lambda i: (0, i)), pl.BlockSpec((N_EXPERTS, LANES), lambda i: (0, 0))),
        out_shape=(jax.ShapeDtypeStruct((1, A), jnp.int32), jax.ShapeDtypeStruct((N_EXPERTS, LANES), jnp.int32)),
        scratch_shapes=[pltpu.VMEM((N_EXPERTS, LANES), F32)],
        compiler_params=_params("arbitrary"),
        name="slot_ranks",
    )(e_flat, tri_strict)


def _row_copy_kernel(idx_ref, src_ref, *rest, scatter, src_rows):
    dst_ref, sem = rest[-2:]
    base = pl.program_id(0) * COPY_ROWS

    def copy(j):
        lin = base + j
        if scatter:
            s, d = lin % src_rows, idx_ref[0, 0, j]
        else:
            s, d = idx_ref[0, 0, j], lin
        return pltpu.make_async_copy(src_ref.at[pl.ds(s, 1), :], dst_ref.at[pl.ds(d, 1), :], sem)

    def start(j, carry):
        copy(j).start()
        return carry

    def wait(j, carry):
        copy(j).wait()
        return carry

    lax.fori_loop(0, COPY_ROWS, start, 0)
    lax.fori_loop(0, COPY_ROWS, wait, 0)


def _row_copy(idx, src, dst_rows, scatter):
    A = idx.shape[0]
    idx3 = idx.reshape(A // COPY_ROWS, 1, COPY_ROWS)
    hbm = pl.BlockSpec(memory_space=pl.ANY)
    extra = (jnp.zeros((dst_rows, D_MODEL), src.dtype),) if scatter else ()
    return pl.pallas_call(
        functools.partial(_row_copy_kernel, scatter=scatter, src_rows=src.shape[0]),
        grid=(A // COPY_ROWS,),
        in_specs=[pl.BlockSpec((1, 1, COPY_ROWS), lambda i: (i, 0, 0), memory_space=pltpu.SMEM), hbm]
                 + [hbm] * len(extra),
        out_specs=hbm,
        out_shape=jax.ShapeDtypeStruct((dst_rows, D_MODEL), src.dtype),
        scratch_shapes=[pltpu.SemaphoreType.DMA],
        input_output_aliases={2: 0} if scatter else {},
        compiler_params=_params("arbitrary"),
        name="dispatch_rows" if scatter else "combine_rows",
    )(idx3, src, *extra)


def _ffn_kernel(be_ref, nu_ref, x_ref, wg_ref, wu_ref, wd_ref, o_ref):
    i = pl.program_id(0)

    @pl.when(i < nu_ref[0])
    def _():
        xb = x_ref[...].astype(BF16)
        gate = _dot(xb, wg_ref[...])
        up = _dot(xb, wu_ref[...])
        h = gate * (1.0 / (1.0 + jnp.exp(-gate))) * up
        o_ref[...] = _dot(h.astype(BF16), wd_ref[...])

    @pl.when(i >= nu_ref[0])
    def _():
        o_ref[...] = jnp.zeros_like(o_ref)


def _expert_ffn(xp, blk_e, n_used, w_gate, w_up, w_down, layer):
    P = xp.shape[0]
    wspec = lambda shp: pl.BlockSpec((None, None) + shp, lambda i, be, nu: (layer, be[i], 0, 0))
    grid_spec = pltpu.PrefetchScalarGridSpec(
        num_scalar_prefetch=2,
        grid=(P // ROW_BLOCK,),
        in_specs=[pl.BlockSpec((ROW_BLOCK, D_MODEL), lambda i, be, nu: (i, 0)),
                  wspec((D_MODEL, D_FF_EXPERT)), wspec((D_MODEL, D_FF_EXPERT)), wspec((D_FF_EXPERT, D_MODEL))],
        out_specs=pl.BlockSpec((ROW_BLOCK, D_MODEL), lambda i, be, nu: (i, 0)),
    )
    return pl.pallas_call(
        _ffn_kernel,
        grid_spec=grid_spec,
        out_shape=jax.ShapeDtypeStruct((P, D_MODEL), F32),
        compiler_params=_params("arbitrary"),
        name="expert_ffn",
    )(blk_e, n_used, xp, w_gate, w_up, w_down)


def _moe_ln_kernel(x_ref, y0_ref, y1_ref, gw_ref, eye_ref, g_ref, b_ref, o_ref):
    wcol = lax.dot_general(gw_ref[...], eye_ref[...], (((0,), (0,)), ((), ())),
                           precision=lax.Precision.HIGHEST, preferred_element_type=F32)
    y = wcol[:, 0:1] * y0_ref[...] + wcol[:, 1:2] * y1_ref[...]
    o_ref[...] = _layer_norm(ALPHA * x_ref[...] + y, g_ref[...], b_ref[...])


def _moe_ln(x, gathered, gw, eye, g, b):
    T = x.shape[0]
    tm = TOKEN_TILE
    nt = T // tm
    full = lambda a: pl.BlockSpec(a.shape, lambda i: (0,) * a.ndim)
    return pl.pallas_call(
        _moe_ln_kernel,
        grid=(nt,),
        in_specs=[pl.BlockSpec((tm, D_MODEL), lambda i: (i, 0)),
                  pl.BlockSpec((tm, D_MODEL), lambda i: (i, 0)),
                  pl.BlockSpec((tm, D_MODEL), lambda i: (i + nt, 0)),
                  pl.BlockSpec((_IDX_ROWS, tm), lambda i: (0, i)), full(eye), full(g), full(b)],
        out_specs=pl.BlockSpec((tm, D_MODEL), lambda i: (i, 0)),
        out_shape=jax.ShapeDtypeStruct((T, D_MODEL), F32),
        compiler_params=_params("parallel"),
        name="moe_combine_ln3",
    )(x, gathered, gathered, gw, eye, g, b)


def _moe(x, consts, router_wt, router_b, w_gate, w_up, w_down, layer, g, b):
    T = x.shape[0]
    A = T * TOP_K
    P = A + N_EXPERTS * ROW_BLOCK
    e_idx, gw = _router(x, router_wt, router_b)
    e_flat = e_idx[:TOP_K].reshape(1, A)
    rank, counts = _ranks(e_flat, consts["tri_strict"])
    counts = counts[:, 0]
    padded = (counts + ROW_BLOCK - 1) // ROW_BLOCK * ROW_BLOCK
    pad_end = jnp.cumsum(padded)
    pad_start = pad_end - padded
    dest = (pad_start[e_flat[0]] + rank[0]).astype(jnp.int32)
    n_blk = P // ROW_BLOCK
    blk_e = jnp.minimum(jnp.searchsorted(pad_end, jnp.arange(n_blk, dtype=jnp.int32) * ROW_BLOCK, side='right'),
                        N_EXPERTS - 1).astype(jnp.int32)
    n_used = (pad_end[-1:] // ROW_BLOCK).astype(jnp.int32)
    xp = _row_copy(dest, x, P, scatter=True)
    yp = _expert_ffn(xp, blk_e, n_used, w_gate, w_up, w_down, layer)
    gathered = _row_copy(dest, yp, A, scatter=False)
    return _moe_ln(x, gathered, gw, consts["eye"], g, b)


def _rope_tables(seq):
    inv_freq = ROPE_THETA ** (-jnp.arange(0, ROT_DIM, 2, dtype=F32) / ROT_DIM)
    ang = jnp.arange(seq, dtype=F32)[:, None] * inv_freq[None, :]
    cos, sin = jnp.cos(ang), jnp.sin(ang)
    half = ROT_DIM // 2
    one = jnp.ones((seq, HEAD_DIM - ROT_DIM), F32)
    zero = jnp.zeros((seq, HEAD_DIM - ROT_DIM), F32)
    zh = jnp.zeros((seq, half), F32)
    cos_h = jnp.concatenate([cos, cos, one], axis=1)
    sa_h = jnp.concatenate([-sin, zh, zero], axis=1)
    sb_h = jnp.concatenate([zh, sin, zero], axis=1)
    rep = lambda t: jnp.concatenate([t] * (LANES // HEAD_DIM), axis=1)
    return rep(cos_h), rep(sa_h), rep(sb_h)


def _constants():
    r = lax.broadcasted_iota(jnp.int32, (CHUNK, CHUNK), 0)
    c = lax.broadcasted_iota(jnp.int32, (CHUNK, CHUNK), 1)
    rr = lax.broadcasted_iota(jnp.int32, (_RANK_TILE, _RANK_TILE), 0)
    cc = lax.broadcasted_iota(jnp.int32, (_RANK_TILE, _RANK_TILE), 1)
    return {
        "tri_u": (r <= c).astype(F32),
        "tri_l": (r >= c).astype(F32),
        "tri_strict": (rr < cc).astype(BF16),
        "eye": (lax.broadcasted_iota(jnp.int32, (_IDX_ROWS, LANES), 0)
                == lax.broadcasted_iota(jnp.int32, (_IDX_ROWS, LANES), 1)).astype(F32),
    }


def _trunk(x, mem, wts, consts):
    batch, seq, _ = x.shape
    T = batch * seq
    x = x.reshape(T, D_MODEL)
    mem2 = mem.reshape(batch * N_MEM, D_MODEL)
    rope_tabs = _rope_tables(seq)
    for l in range(DEPTH):
        qa, ka, va, mqk, mv, mo, gp = _in_proj(x, wts["w_main"][l], wts["wgt"][l], wts["gb"][l], rope_tabs,
                                               consts["tri_u"], consts["tri_l"], seq)
        att = _attention(qa, ka, va, wts["sink"][l], wts["att_g"][l], batch, seq)
        hf, hb = _mlstm(mqk, mv, gp, wts["conv_w"][l], wts["conv_b"][l], batch, seq)
        x = _out_proj(x, att, hf, hb, mo, wts["mlstm_g"][l], wts["w_out_a"][l], wts["w_out_m"][l],
                      wts["ln1_g"][l], wts["ln1_b"][l])
        k_mem, v_mem = _kv_proj(mem2, wts["wkv"][l])
        x = _xattn(x, k_mem, v_mem, wts["wq"][l], wts["wo"][l], wts["ln2_g"][l], wts["ln2_b"][l], batch, seq)
        x = _moe(x, consts, wts["router_wt"], wts["router_b"], wts["w_gate"], wts["w_up"], wts["w_down"], l,
                 wts["ln3_g"][l], wts["ln3_b"][l])
    return x.reshape(batch, seq, D_MODEL)


def kernel(x_prompt, x_sample, mem_prompt, mem_sample, w_in, gate_bias, conv_w, conv_b, attn_sink, attn_norm_g, mlstm_norm_g, w_out, ln1_g, ln1_b, wq_mem, wkv_mem, wo_mem, ln2_g, ln2_b, router_w, router_bias, w_gate, w_up, w_down, ln3_g, ln3_b):
    row = lambda t: t.astype(F32).reshape(DEPTH, 1, t.shape[-1])
    wts = {
        "w_main": w_in[:, :, :OFF_G].astype(BF16),
        "wgt": jnp.swapaxes(w_in[:, :, OFF_G:], 1, 2).astype(BF16),
        "gb": gate_bias.astype(F32).reshape(DEPTH, N_GATE_COLS, 1),
        "conv_w": conv_w.astype(F32),
        "conv_b": row(conv_b),
        "sink": attn_sink.astype(F32),
        "att_g": row(attn_norm_g),
        "mlstm_g": row(mlstm_norm_g),
        "w_out_a": w_out[:, :ATT_WIDTH].astype(BF16),
        "w_out_m": w_out[:, ATT_WIDTH:].astype(BF16),
        "ln1_g": row(ln1_g), "ln1_b": row(ln1_b),
        "wq": wq_mem.astype(BF16), "wkv": wkv_mem.astype(BF16), "wo": wo_mem.astype(BF16),
        "ln2_g": row(ln2_g), "ln2_b": row(ln2_b),
        "router_wt": router_w.astype(F32).T,
        "router_b": router_bias.astype(F32).reshape(N_EXPERTS, 1),
        "w_gate": w_gate.astype(BF16), "w_up": w_up.astype(BF16), "w_down": w_down.astype(BF16),
        "ln3_g": row(ln3_g), "ln3_b": row(ln3_b),
    }
    consts = _constants()
    return (_trunk(x_prompt, mem_prompt, wts, consts), _trunk(x_sample, mem_sample, wts, consts))
```

```python
import functools
import math

import jax
import jax.numpy as jnp
from jax import lax
from jax.experimental import pallas as pl
from jax.experimental.pallas import tpu as pltpu

F32 = jnp.float32
BF16 = jnp.bfloat16

D_MODEL = 1024
DEPTH = 4
HEAD_DIM = 64
N_ATT_HEADS = 8
N_KV_HEADS = 2
ATT_WIDTH = N_ATT_HEADS * HEAD_DIM
KV_WIDTH = N_KV_HEADS * HEAD_DIM
BLOCK = 128
ROT_DIM = HEAD_DIM // 4
ROPE_THETA = 500000.0
MLSTM_WIDTH = D_MODEL - ATT_WIDTH
N_MLSTM_HEADS = 4
MLSTM_HEAD_DIM = MLSTM_WIDTH // N_MLSTM_HEADS
CHUNK = 128
OFF_AQ = 0
OFF_AK = OFF_AQ + ATT_WIDTH
OFF_AV = OFF_AK + KV_WIDTH
OFF_MQ = OFF_AV + KV_WIDTH
OFF_MV = OFF_MQ + 2 * MLSTM_WIDTH
OFF_MO = OFF_MV + MLSTM_WIDTH
OFF_G = OFF_MO + MLSTM_WIDTH
N_GATE_COLS = 4 * N_MLSTM_HEADS
N_MEM = 256
N_X_HEADS = 4
X_HEAD_DIM = D_MODEL // N_X_HEADS
N_EXPERTS = 16
N_GROUPS = 4
EXPERTS_PER_GROUP = N_EXPERTS // N_GROUPS
TOP_K = 2
D_FF_EXPERT = 512
ALPHA = (2.0 * DEPTH) ** 0.25
LN_EPS = 1e-5
RMS_EPS = 1e-6
NEG = -1e30

LANES = 128
TOKEN_TILE = 512
ROW_BLOCK = 512
VMEM_LIMIT = 56 * 1024 * 1024


def _params(*sem):
    return pltpu.CompilerParams(dimension_semantics=sem, vmem_limit_bytes=VMEM_LIMIT)


def _layer_norm(z, g, b):
    mu = jnp.mean(z, axis=-1, keepdims=True)
    zc = z - mu
    var = jnp.mean(zc * zc, axis=-1, keepdims=True)
    return zc * lax.rsqrt(var + LN_EPS) * g + b


def _dot(a, b):
    return jnp.dot(a, b, preferred_element_type=F32)


def _dot_nt(a, b):
    return lax.dot_general(a, b, (((1,), (1,)), ((), ())), preferred_element_type=F32)


def _dot_tn(a, b):
    return lax.dot_general(a, b, (((0,), (0,)), ((), ())), preferred_element_type=F32)


_X_HALO = 8


def _in_proj_kernel(x_ref, xp_ref, xn_ref, w_ref, wgt_ref, gb_ref, cw_ref, cb_ref, cos_ref, sa_ref, sb_ref,
                    tri_u_ref, tri_l_ref, qa_ref, ka_ref, va_ref, mq_ref, mk_ref, mv_ref, mo_ref, gp_ref,
                    *, n_seq_tiles):
    xb = x_ref[...].astype(BF16)
    cos = cos_ref[...]
    sa = sa_ref[...]
    sb = sb_ref[...]

    def mm(lo, hi):
        return _dot(xb, w_ref[:, lo:hi])

    def rope(t):
        return t * cos + pltpu.roll(t, LANES - ROT_DIM // 2, 1) * sa + pltpu.roll(t, ROT_DIM // 2, 1) * sb

    q = mm(OFF_AQ, OFF_AK)
    scale = 1.0 / math.sqrt(HEAD_DIM)
    for j in range(ATT_WIDTH // LANES):
        qa_ref[:, j * LANES:(j + 1) * LANES] = (rope(q[:, j * LANES:(j + 1) * LANES]) * scale).astype(BF16)
    ka_ref[...] = rope(mm(OFF_AK, OFF_AV)).astype(BF16)
    va_ref[...] = mm(OFF_AV, OFF_MQ).astype(BF16)
    u = mm(OFF_MQ, OFF_MV)
    tm = u.shape[0]
    pos = pl.program_id(0) % n_seq_tiles
    halo = jnp.concatenate([xp_ref[...], xn_ref[...]], axis=0).astype(BF16)
    uh = _dot(halo, w_ref[:, OFF_MQ:OFF_MV])
    prev_row = jnp.where(pos > 0, uh[_X_HALO - 1:_X_HALO, :], 0.0)
    next_row = jnp.where(pos < n_seq_tiles - 1, uh[_X_HALO:_X_HALO + 1, :], 0.0)
    rowi = lax.broadcasted_iota(jnp.int32, u.shape, 0)
    u_prev = jnp.where(rowi == 0, prev_row, pltpu.roll(u, 1, 0))
    u_next = jnp.where(rowi == tm - 1, next_row, pltpu.roll(u, tm - 1, 0))
    y = cw_ref[0:1, :] * u_prev + cw_ref[1:2, :] * u + cw_ref[2:3, :] * u_next + cb_ref[...]
    y = y * (1.0 / (1.0 + jnp.exp(-y)))
    mq_ref[...] = y[:, :MLSTM_WIDTH].astype(BF16)
    mk_ref[...] = (y[:, MLSTM_WIDTH:] * (MLSTM_HEAD_DIM ** -0.5)).astype(BF16)
    mv_ref[...] = mm(OFF_MV, OFF_MO).astype(BF16)
    mo_ref[...] = mm(OFF_MO, OFF_G).astype(BF16)

    gt = _dot_nt(wgt_ref[...], xb) + gb_ref[...]
    ls = jnp.minimum(gt, 0.0) - jnp.log1p(jnp.exp(-jnp.abs(gt)))
    n_chunks = gt.shape[1] // CHUNK
    ls_rows = jnp.concatenate([ls[:, c * CHUNK:(c + 1) * CHUNK] for c in range(n_chunks)], axis=0)
    pre = jnp.dot(ls_rows, tri_u_ref[...], precision=lax.Precision.HIGHEST, preferred_element_type=F32)
    suf = jnp.dot(ls_rows, tri_l_ref[...], precision=lax.Precision.HIGHEST, preferred_element_type=F32)
    row = lax.broadcasted_iota(jnp.int32, (N_GATE_COLS, CHUNK), 0)
    for c in range(n_chunks):
        raw = gt[:, c * CHUNK:(c + 1) * CHUNK]
        p = pre[c * N_GATE_COLS:(c + 1) * N_GATE_COLS]
        s = suf[c * N_GATE_COLS:(c + 1) * N_GATE_COLS]
        is_f_fwd = (row >= N_MLSTM_HEADS) & (row < 2 * N_MLSTM_HEADS)
        is_f_bwd = row >= 3 * N_MLSTM_HEADS
        gp_ref[c] = jnp.where(is_f_fwd, p, jnp.where(is_f_bwd, s, raw))


def _in_proj(x, w_main, wgt, gb, conv_w, conv_b, rope_tabs, tri_u, tri_l, seq):
    T = x.shape[0]
    tm = TOKEN_TILE
    n_seq_tiles = seq // tm
    halo_per_tile = tm // _X_HALO
    n_halo = T // _X_HALO
    cos_t, sa_t, sb_t = rope_tabs
    row_spec = lambda w: pl.BlockSpec((tm, w), lambda i: (i, 0))
    full = lambda a: pl.BlockSpec(a.shape, lambda i: (0,) * a.ndim)
    tab_spec = pl.BlockSpec((tm, LANES), lambda i: (i % n_seq_tiles, 0))
    prev_spec = pl.BlockSpec((_X_HALO, D_MODEL), lambda i: (jnp.maximum(i * halo_per_tile - 1, 0), 0))
    next_spec = pl.BlockSpec((_X_HALO, D_MODEL), lambda i: (jnp.minimum((i + 1) * halo_per_tile, n_halo - 1), 0))
    widths = (ATT_WIDTH, KV_WIDTH, KV_WIDTH, MLSTM_WIDTH, MLSTM_WIDTH, MLSTM_WIDTH, MLSTM_WIDTH)
    out_shapes = tuple(jax.ShapeDtypeStruct((T, w), BF16) for w in widths) + (
        jax.ShapeDtypeStruct((T // CHUNK, N_GATE_COLS, CHUNK), F32),)
    out_specs = tuple(row_spec(w) for w in widths) + (
        pl.BlockSpec((tm // CHUNK, N_GATE_COLS, CHUNK), lambda i: (i, 0, 0)),)
    return pl.pallas_call(
        functools.partial(_in_proj_kernel, n_seq_tiles=n_seq_tiles),
        grid=(T // tm,),
        in_specs=[row_spec(D_MODEL), prev_spec, next_spec, full(w_main), full(wgt), full(gb), full(conv_w),
                  full(conv_b), tab_spec, tab_spec, tab_spec, full(tri_u), full(tri_l)],
        out_specs=out_specs,
        out_shape=out_shapes,
        compiler_params=_params("parallel"),
        name="in_proj",
    )(x, x, x, w_main, wgt, gb, conv_w, conv_b, cos_t, sa_t, sb_t, tri_u, tri_l)


ATT_Q_TILE = 512
_Q_BLOCKS = ATT_Q_TILE // BLOCK


def _attn_kernel(sink_ref, q_ref, kp_ref, k_ref, kn_ref, vp_ref, v_ref, vn_ref, g_ref, o_ref, *, n_tiles):
    i = pl.program_id(1)
    lane = lax.broadcasted_iota(jnp.int32, (ATT_Q_TILE + 2 * BLOCK, LANES), 1)
    low = lane < HEAD_DIM

    def split(prev_ref, own_ref, next_ref):
        t = jnp.concatenate([prev_ref[...], own_ref[...], next_ref[...]], axis=0).astype(F32)
        r = pltpu.roll(t, HEAD_DIM, 1)
        zero = jnp.zeros_like(t)
        lo = (jnp.where(low, t, zero).astype(BF16), jnp.where(low, r, zero).astype(BF16))
        hi = (jnp.where(low, zero, r).astype(BF16), jnp.where(low, zero, t).astype(BF16))
        return lo, hi

    k_lo, k_hi = split(kp_ref, k_ref, kn_ref)
    v_lo, v_hi = split(vp_ref, v_ref, vn_ref)

    rowi = lax.broadcasted_iota(jnp.int32, (BLOCK, 3 * BLOCK), 0)
    coli = lax.broadcasted_iota(jnp.int32, (BLOCK, 3 * BLOCK), 1)
    out_of_band = (coli < rowi) | (coli - 2 * BLOCK > rowi)
    band_bias = jnp.where(out_of_band, NEG, 0.0).astype(F32)
    first_bias = jnp.where(i == 0, NEG, 0.0).astype(F32)
    last_bias = jnp.where(i == n_tiles - 1, NEG, 0.0).astype(F32)
    bias_first = band_bias + jnp.where(coli < BLOCK, first_bias, 0.0)
    bias_last = band_bias + jnp.where(coli >= 2 * BLOCK, last_bias, 0.0)
    lane_o = lax.broadcasted_iota(jnp.int32, (BLOCK, LANES), 1)
    g = g_ref[...]

    for r in range(_Q_BLOCKS):
        bias = bias_first if r == 0 else (bias_last if r == _Q_BLOCKS - 1 else band_bias)
        rows = slice(r * BLOCK, (r + 1) * BLOCK)
        win = slice(r * BLOCK, (r + 3) * BLOCK)
        tiles = []
        for c in range(N_KV_HEADS):
            q2 = jnp.concatenate([q_ref[rows, (2 * c) * LANES:(2 * c + 1) * LANES],
                                  q_ref[rows, (2 * c + 1) * LANES:(2 * c + 2) * LANES]], axis=0)
            kc = jnp.concatenate([k_lo[c][win], k_hi[c][win]], axis=0)
            vc = jnp.concatenate([v_lo[c][win], v_hi[c][win]], axis=0)
            s = _dot_nt(q2, kc)
            p_rows, inv_rows = [], []
            for t in range(2):
                ps, invs = [], []
                for hh in range(2):
                    head = 4 * c + 2 * t + hh
                    sink = sink_ref[head]
                    sh = s[t * BLOCK:(t + 1) * BLOCK, hh * 3 * BLOCK:(hh + 1) * 3 * BLOCK] + bias
                    m = jnp.maximum(jnp.max(sh, axis=-1, keepdims=True), sink)
                    p = jnp.exp(sh - m)
                    den = jnp.sum(p, axis=-1, keepdims=True) + jnp.exp(sink - m)
                    ps.append(p.astype(BF16))
                    invs.append(1.0 / den)
                p_rows.append(jnp.concatenate(ps, axis=1))
                inv_rows.append(jnp.where(lane_o < HEAD_DIM, invs[0], invs[1]))
            o2 = _dot(jnp.concatenate(p_rows, axis=0), vc)
            tiles.append(o2[:BLOCK] * inv_rows[0])
            tiles.append(o2[BLOCK:] * inv_rows[1])
        o = jnp.concatenate(tiles, axis=1)
        ms = jnp.mean(o * o, axis=-1, keepdims=True)
        o_ref[rows, :] = (o * lax.rsqrt(ms + RMS_EPS) * g).astype(BF16)


def _attention(qa, ka, va, sink, att_g, batch, seq):
    T = qa.shape[0]
    n_tiles = seq // ATT_Q_TILE
    blocks_per_seq = seq // BLOCK
    own = lambda w: pl.BlockSpec((ATT_Q_TILE, w), lambda b, i: (b * n_tiles + i, 0))
    prev = pl.BlockSpec((BLOCK, KV_WIDTH),
                        lambda b, i: (b * blocks_per_seq + jnp.maximum(i * _Q_BLOCKS - 1, 0), 0))
    nxt = pl.BlockSpec((BLOCK, KV_WIDTH),
                       lambda b, i: (b * blocks_per_seq + jnp.minimum((i + 1) * _Q_BLOCKS, blocks_per_seq - 1), 0))
    return pl.pallas_call(
        functools.partial(_attn_kernel, n_tiles=n_tiles),
        grid=(batch, n_tiles),
        in_specs=[pl.BlockSpec(memory_space=pltpu.SMEM), own(ATT_WIDTH), prev, own(KV_WIDTH), nxt,
                  prev, own(KV_WIDTH), nxt, pl.BlockSpec((1, ATT_WIDTH), lambda b, i: (0, 0))],
        out_specs=own(ATT_WIDTH),
        out_shape=jax.ShapeDtypeStruct((T, ATT_WIDTH), BF16),
        compiler_params=_params("parallel", "parallel"),
        name="band_attention",
    )(sink, qa, ka, ka, ka, va, va, va, att_g)


def _mlstm_kernel(q_f, k_f, v_f, gp_f, q_b, k_b, v_b, gp_b, of_ref, ob_ref, c_state, m_state):
    @pl.when(pl.program_id(1) == 0)
    def _():
        c_state[...] = jnp.zeros_like(c_state)
        m_state[...] = jnp.zeros_like(m_state)

    rowi = lax.broadcasted_iota(jnp.int32, (CHUNK, CHUNK), 0)
    coli = lax.broadcasted_iota(jnp.int32, (CHUNK, CHUNK), 1)
    ones = jnp.ones((CHUNK, MLSTM_HEAD_DIM), BF16)

    units = []
    for d, (q_ref, k_ref, v_ref, gp_ref, o_ref) in enumerate(((q_f, k_f, v_f, gp_f, of_ref),
                                                              (q_b, k_b, v_b, gp_b, ob_ref))):
        fwd = d == 0
        gates = gp_ref[0]
        gates_t = jnp.concatenate([gates, jnp.zeros((CHUNK - N_GATE_COLS, CHUNK), F32)], axis=0).T
        causal = (coli <= rowi) if fwd else (coli >= rowi)
        a_pos = CHUNK - 1 if fwd else 0
        base = 0 if fwd else 2 * N_MLSTM_HEADS
        for h in range(N_MLSTM_HEADS):
            hs = slice(h * MLSTM_HEAD_DIM, (h + 1) * MLSTM_HEAD_DIM)
            b_row = gates[base + N_MLSTM_HEADS + h:base + N_MLSTM_HEADS + h + 1, :]
            units.append(dict(
                d=d, h=h, hs=hs, o_ref=o_ref, causal=causal, q_ref=q_ref, k_ref=k_ref, v_ref=v_ref,
                ig_row=gates[base + h:base + h + 1, :], b_row=b_row,
                ig_col=gates_t[:, base + h:base + h + 1],
                b_col=gates_t[:, base + N_MLSTM_HEADS + h:base + N_MLSTM_HEADS + h + 1],
                a=b_row[:, a_pos:a_pos + 1],
                m_in=m_state[d * N_MLSTM_HEADS + h:d * N_MLSTM_HEADS + h + 1, 0:1]))

    for u in units:
        log_d = jnp.where(u["causal"], u["b_col"] + (u["ig_row"] - u["b_row"]), NEG)
        inter_log = u["b_col"] + u["m_in"]
        m_t = jnp.maximum(inter_log, jnp.max(log_d, axis=-1, keepdims=True))
        u["dmat"] = jnp.exp(log_d - m_t)
        u["inter_w"] = jnp.exp(inter_log - m_t)
        u["floor"] = jnp.exp(-m_t)
    for u in units:
        q = u["q_ref"][:, u["hs"]]
        u["s"] = _dot_nt(q, u["k_ref"][:, u["hs"]])
        u["c_in"] = c_state[u["d"], u["h"]]
        u["inter"] = _dot(q, u["c_in"].astype(BF16))
        u["v_aug"] = jnp.concatenate([u["v_ref"][:, u["hs"]], ones], axis=1)
    for u in units:
        sw = (u["s"] * u["dmat"]).astype(BF16)
        tot = _dot(sw, u["v_aug"]) + u["inter_w"] * u["inter"]
        den = jnp.maximum(jnp.abs(tot[:, MLSTM_HEAD_DIM:]), u["floor"])
        u["o_ref"][:, u["hs"]] = (tot[:, :MLSTM_HEAD_DIM] / den).astype(u["o_ref"].dtype)
    for u in units:
        a, m_in = u["a"], u["m_in"]
        g_max = jnp.max(a - u["b_row"] + u["ig_row"], axis=-1, keepdims=True)
        wk = jnp.exp(a - u["b_col"] + u["ig_col"] - g_max)
        kw = (u["k_ref"][:, u["hs"]].astype(F32) * wk).astype(BF16)
        c_chunk = _dot_tn(kw, u["v_aug"])
        m_new = jnp.maximum(a + m_in, g_max)
        d, h = u["d"], u["h"]
        c_state[d, h] = jnp.exp(a + m_in - m_new) * u["c_in"] + jnp.exp(g_max - m_new) * c_chunk
        m_state[d * N_MLSTM_HEADS + h:d * N_MLSTM_HEADS + h + 1, :] = jnp.broadcast_to(m_new, (1, LANES))


def _mlstm(mq, mk, mv, gp, batch, seq):
    T = mq.shape[0]
    nc = seq // CHUNK

    def specs(chunk_of):
        blk = lambda b, i: b * nc + chunk_of(i)
        row = pl.BlockSpec((CHUNK, MLSTM_WIDTH), lambda b, i: (blk(b, i), 0))
        return [row, row, row, pl.BlockSpec((1, N_GATE_COLS, CHUNK), lambda b, i: (blk(b, i), 0, 0))]

    fwd_chunk = lambda i: i
    bwd_chunk = lambda i: nc - 1 - i
    out = jax.ShapeDtypeStruct((T, MLSTM_WIDTH), BF16)
    return pl.pallas_call(
        _mlstm_kernel,
        grid=(batch, nc),
        in_specs=specs(fwd_chunk) + specs(bwd_chunk),
        out_specs=(pl.BlockSpec((CHUNK, MLSTM_WIDTH), lambda b, i: (b * nc + i, 0)),
                   pl.BlockSpec((CHUNK, MLSTM_WIDTH), lambda b, i: (b * nc + nc - 1 - i, 0))),
        out_shape=(out, out),
        scratch_shapes=[pltpu.VMEM((2, N_MLSTM_HEADS, MLSTM_HEAD_DIM, 2 * MLSTM_HEAD_DIM), F32),
                        pltpu.VMEM((2 * N_MLSTM_HEADS, LANES), F32)],
        compiler_params=_params("parallel", "arbitrary"),
        name="mlstm",
    )(mq, mk, mv, gp, mq, mk, mv, gp)


def _out_proj_kernel(x_ref, att_ref, hf_ref, hb_ref, mo_ref, mg_ref, wa_ref, wm_ref, g_ref, b_ref, o_ref):
    h = hf_ref[...].astype(F32) + hb_ref[...].astype(F32)
    mg = mg_ref[...]
    parts = []
    for hd in range(N_MLSTM_HEADS):
        hs = slice(hd * MLSTM_HEAD_DIM, (hd + 1) * MLSTM_HEAD_DIM)
        hh = h[:, hs]
        ms = jnp.mean(hh * hh, axis=-1, keepdims=True)
        parts.append(hh * lax.rsqrt(ms + RMS_EPS) * mg[:, hs])
    hn = jnp.concatenate(parts, axis=1)
    gate = 1.0 / (1.0 + jnp.exp(-mo_ref[...].astype(F32)))
    y = _dot(att_ref[...], wa_ref[...]) + _dot((hn * gate).astype(BF16), wm_ref[...])
    o_ref[...] = _layer_norm(ALPHA * x_ref[...] + y, g_ref[...], b_ref[...])


def _out_proj(x, att, hf, hb, mo, mg, wa, wm, g, b):
    T = x.shape[0]
    tm = TOKEN_TILE
    row = lambda w: pl.BlockSpec((tm, w), lambda i: (i, 0))
    full = lambda a: pl.BlockSpec(a.shape, lambda i: (0,) * a.ndim)
    return pl.pallas_call(
        _out_proj_kernel,
        grid=(T // tm,),
        in_specs=[row(D_MODEL), row(ATT_WIDTH), row(MLSTM_WIDTH), row(MLSTM_WIDTH), row(MLSTM_WIDTH),
                  full(mg), full(wa), full(wm), full(g), full(b)],
        out_specs=row(D_MODEL),
        out_shape=jax.ShapeDtypeStruct((T, D_MODEL), F32),
        compiler_params=_params("parallel"),
        name="out_proj_ln1",
    )(x, att, hf, hb, mo, mg, wa, wm, g, b)


def _kv_proj_kernel(m_ref, w_ref, k_ref, v_ref):
    mb = m_ref[...].astype(BF16)
    k_ref[...] = _dot(mb, w_ref[:, :D_MODEL]).astype(BF16)
    v_ref[...] = _dot(mb, w_ref[:, D_MODEL:]).astype(BF16)


def _kv_proj(mem, wkv):
    M = mem.shape[0]
    tm = TOKEN_TILE
    row = pl.BlockSpec((tm, D_MODEL), lambda i: (i, 0))
    out = jax.ShapeDtypeStruct((M, D_MODEL), BF16)
    return pl.pallas_call(
        _kv_proj_kernel,
        grid=(M // tm,),
        in_specs=[row, pl.BlockSpec(wkv.shape, lambda i: (0, 0))],
        out_specs=(row, row),
        out_shape=(out, out),
        compiler_params=_params("parallel"),
        name="mem_kv_proj",
    )(mem, wkv)


def _xattn_kernel(x_ref, k_ref, v_ref, wq_ref, wo_ref, g_ref, b_ref, o_ref):
    x = x_ref[...]
    q = _dot(x.astype(BF16), wq_ref[...])
    scale = 1.0 / math.sqrt(X_HEAD_DIM)
    outs = []
    for h in range(N_X_HEADS):
        hs = slice(h * X_HEAD_DIM, (h + 1) * X_HEAD_DIM)
        s = _dot_nt((q[:, hs] * scale).astype(BF16), k_ref[:, hs])
        m = jnp.max(s, axis=-1, keepdims=True)
        p = jnp.exp(s - m)
        inv = 1.0 / jnp.sum(p, axis=-1, keepdims=True)
        outs.append((_dot(p.astype(BF16), v_ref[:, hs]) * inv).astype(BF16))
    y = _dot(jnp.concatenate(outs, axis=1), wo_ref[...])
    o_ref[...] = _layer_norm(ALPHA * x + y, g_ref[...], b_ref[...])


def _xattn(x, k_mem, v_mem, wq, wo, g, b, batch, seq):
    T = x.shape[0]
    tm = TOKEN_TILE
    nt = seq // tm
    row = pl.BlockSpec((tm, D_MODEL), lambda bb, i: (bb * nt + i, 0))
    mem = pl.BlockSpec((N_MEM, D_MODEL), lambda bb, i: (bb, 0))
    full = lambda a: pl.BlockSpec(a.shape, lambda bb, i: (0,) * a.ndim)
    return pl.pallas_call(
        _xattn_kernel,
        grid=(batch, nt),
        in_specs=[row, mem, mem, full(wq), full(wo), full(g), full(b)],
        out_specs=row,
        out_shape=jax.ShapeDtypeStruct((T, D_MODEL), F32),
        compiler_params=_params("parallel", "parallel"),
        name="mem_xattn_ln2",
    )(x, k_mem, v_mem, wq, wo, g, b)


_IDX_ROWS = 8
SLAB = 16
TILE_SLOTS = TOP_K * TOKEN_TILE + N_EXPERTS * SLAB


def _router_kernel(x_ref, wt_ref, b_ref, tri_ref, low_ref, info_ref, cnt_ref):
    logits = lax.dot_general(wt_ref[...], x_ref[...], (((1,), (1,)), ((), ())),
                             precision=lax.Precision.HIGHEST, preferred_element_type=F32)
    s = 1.0 / (1.0 + jnp.exp(-logits))
    sel = s + b_ref[...]
    srow = lambda e: s[e:e + 1, :]
    brow = lambda e: sel[e:e + 1, :]
    best = None
    gi = None
    for gidx in range(N_GROUPS):
        vals = [brow(gidx * EXPERTS_PER_GROUP + j) for j in range(EXPERTS_PER_GROUP)]
        top2 = None
        for a in range(EXPERTS_PER_GROUP):
            for b in range(a + 1, EXPERTS_PER_GROUP):
                pair = vals[a] + vals[b]
                top2 = pair if top2 is None else jnp.maximum(top2, pair)
        if best is None:
            best, gi = top2, jnp.zeros(top2.shape, jnp.int32)
        else:
            better = top2 > best
            gi = jnp.where(better, gidx, gi)
            best = jnp.where(better, top2, best)

    def in_group(rowfn, j):
        out = rowfn(j)
        for gidx in range(1, N_GROUPS):
            out = jnp.where(gi == gidx, rowfn(gidx * EXPERTS_PER_GROUP + j), out)
        return out

    bv = [in_group(brow, j) for j in range(EXPERTS_PER_GROUP)]
    sv = [in_group(srow, j) for j in range(EXPERTS_PER_GROUP)]

    def argmax_first(vals):
        bi = jnp.zeros(vals[0].shape, jnp.int32)
        bm = vals[0]
        for j in range(1, len(vals)):
            better = vals[j] > bm
            bi = jnp.where(better, j, bi)
            bm = jnp.where(better, vals[j], bm)
        return bi

    i1 = argmax_first(bv)
    i2 = argmax_first([jnp.where(i1 == j, -jnp.inf, bv[j]) for j in range(EXPERTS_PER_GROUP)])

    def pick(vals, idx):
        out = vals[0]
        for j in range(1, len(vals)):
            out = jnp.where(idx == j, vals[j], out)
        return out

    w1 = pick(sv, i1)
    w2 = pick(sv, i2)
    tot = w1 + w2
    tm = logits.shape[1]
    eid = lax.broadcasted_iota(jnp.int32, (N_EXPERTS, tm), 0)
    oh1 = (eid == gi * EXPERTS_PER_GROUP + i1).astype(F32)
    oh2 = (eid == gi * EXPERTS_PER_GROUP + i2).astype(F32)
    before1 = _dot(oh1.astype(BF16), tri_ref[...])
    before2 = _dot(oh2.astype(BF16), tri_ref[...])
    c1 = jnp.sum(oh1, axis=1, keepdims=True)
    cnt = c1 + jnp.sum(oh2, axis=1, keepdims=True)
    rows = jnp.floor((cnt + (SLAB - 1)) * (1.0 / SLAB)) * SLAB
    rows_b = jnp.broadcast_to(rows, (N_EXPERTS, LANES))
    start = jnp.dot(low_ref[...], rows_b, precision=lax.Precision.HIGHEST, preferred_element_type=F32)[:, 0:1]
    slot1 = jnp.sum(oh1 * (start + before1), axis=0, keepdims=True)
    slot2 = jnp.sum(oh2 * (start + c1 + before2), axis=0, keepdims=True)
    zf = jnp.zeros((_IDX_ROWS - 2 * TOP_K, tm), F32)
    info_ref[...] = jnp.concatenate([w1 / tot, w2 / tot, slot1, slot2, zf], axis=0)
    cnt_ref[0] = rows_b.astype(jnp.int32)


def _router(x, router_wt, router_b, tri_strict, low_strict):
    T = x.shape[0]
    tm = TOKEN_TILE
    full = lambda a: pl.BlockSpec(a.shape, lambda i: (0,) * a.ndim)
    return pl.pallas_call(
        _router_kernel,
        grid=(T // tm,),
        in_specs=[pl.BlockSpec((tm, D_MODEL), lambda i: (i, 0)), full(router_wt), full(router_b),
                  full(tri_strict), full(low_strict)],
        out_specs=(pl.BlockSpec((_IDX_ROWS, tm), lambda i: (0, i)),
                   pl.BlockSpec((1, N_EXPERTS, LANES), lambda i: (i, 0, 0))),
        out_shape=(jax.ShapeDtypeStruct((_IDX_ROWS, T), F32),
                   jax.ShapeDtypeStruct((T // tm, N_EXPERTS, LANES), jnp.int32)),
        compiler_params=_params("parallel"),
        name="router",
    )(x, router_wt, router_b, tri_strict, low_strict)


def _slab_copies(i, gs_ref, ls_ref, rc_ref, make_copy, op):
    for e in range(N_EXPERTS):
        idx = i * N_EXPERTS + e
        ls = ls_ref[idx]
        gs = gs_ref[idx]

        def body(j, carry, ls=ls, gs=gs):
            op(make_copy(pl.multiple_of(ls + j * SLAB, SLAB), pl.multiple_of(gs + j * SLAB, SLAB)))
            return carry

        lax.fori_loop(0, rc_ref[idx] // SLAB, body, 0)


_DISPATCH_CHUNK = 256


def _dispatch_kernel(gs_ref, ls_ref, rc_ref, x_ref, info_ref, zero_ref, xp_ref, xs_ref, sem):
    del zero_ref
    i = pl.program_id(0)
    slot0 = info_ref[2:3, :].astype(jnp.int32)
    slot1 = info_ref[3:4, :].astype(jnp.int32)
    xb = x_ref[...].astype(BF16)
    tm = xb.shape[0]
    for c in range(TILE_SLOTS // _DISPATCH_CHUNK):
        sid = lax.broadcasted_iota(jnp.int32, (_DISPATCH_CHUNK, tm), 0) + c * _DISPATCH_CHUNK
        sel = jnp.where((sid == slot0) | (sid == slot1), 1.0, 0.0).astype(BF16)
        xs_ref[c * _DISPATCH_CHUNK:(c + 1) * _DISPATCH_CHUNK, :] = _dot(sel, xb).astype(BF16)

    def make_copy(tile_row, global_row):
        return pltpu.make_async_copy(xs_ref.at[pl.ds(tile_row, SLAB), :], xp_ref.at[pl.ds(global_row, SLAB), :], sem)

    _slab_copies(i, gs_ref, ls_ref, rc_ref, make_copy, lambda cp: cp.start())
    _slab_copies(i, gs_ref, ls_ref, rc_ref, make_copy, lambda cp: cp.wait())


def _dispatch(x, info, gstart, lstart, rc, n_rows):
    T = x.shape[0]
    tm = TOKEN_TILE
    hbm = pl.BlockSpec(memory_space=pl.ANY)
    grid_spec = pltpu.PrefetchScalarGridSpec(
        num_scalar_prefetch=3,
        grid=(T // tm,),
        in_specs=[pl.BlockSpec((tm, D_MODEL), lambda i, *_: (i, 0)),
                  pl.BlockSpec((_IDX_ROWS, tm), lambda i, *_: (0, i)), hbm],
        out_specs=hbm,
        scratch_shapes=[pltpu.VMEM((TILE_SLOTS, D_MODEL), BF16), pltpu.SemaphoreType.DMA],
    )
    return pl.pallas_call(
        _dispatch_kernel,
        grid_spec=grid_spec,
        out_shape=jax.ShapeDtypeStruct((n_rows, D_MODEL), BF16),
        input_output_aliases={5: 0},
        compiler_params=_params("arbitrary"),
        name="moe_dispatch",
    )(gstart, lstart, rc, x, info, jnp.zeros((n_rows, D_MODEL), BF16))


def _ffn_kernel(be_ref, nu_ref, x_ref, wg_ref, wu_ref, wd_ref, o_ref):
    i = pl.program_id(0)

    @pl.when(i < nu_ref[0])
    def _():
        xb = x_ref[...]
        gate = _dot(xb, wg_ref[...])
        up = _dot(xb, wu_ref[...])
        h = gate * (1.0 / (1.0 + jnp.exp(-gate))) * up
        o_ref[...] = _dot(h.astype(BF16), wd_ref[...]).astype(BF16)

    @pl.when(i >= nu_ref[0])
    def _():
        o_ref[...] = jnp.zeros_like(o_ref)


def _expert_ffn(xp, blk_e, n_used, w_gate, w_up, w_down, layer):
    P = xp.shape[0]
    wspec = lambda shp: pl.BlockSpec((None, None) + shp, lambda i, be, nu: (layer, be[i], 0, 0))
    grid_spec = pltpu.PrefetchScalarGridSpec(
        num_scalar_prefetch=2,
        grid=(P // ROW_BLOCK,),
        in_specs=[pl.BlockSpec((ROW_BLOCK, D_MODEL), lambda i, be, nu: (i, 0)),
                  wspec((D_MODEL, D_FF_EXPERT)), wspec((D_MODEL, D_FF_EXPERT)), wspec((D_FF_EXPERT, D_MODEL))],
        out_specs=pl.BlockSpec((ROW_BLOCK, D_MODEL), lambda i, be, nu: (i, 0)),
    )
    return pl.pallas_call(
        _ffn_kernel,
        grid_spec=grid_spec,
        out_shape=jax.ShapeDtypeStruct((P, D_MODEL), BF16),
        compiler_params=_params("arbitrary"),
        name="expert_ffn",
    )(blk_e, n_used, xp, w_gate, w_up, w_down)


def _combine_kernel(gs_ref, ls_ref, rc_ref, x_ref, info_ref, eye_ref, g_ref, b_ref, yp_ref, o_ref, ys_ref, sem):
    i = pl.program_id(0)

    @pl.when(i == 0)
    def _():
        ys_ref[...] = jnp.zeros_like(ys_ref)

    def make_copy(tile_row, global_row):
        return pltpu.make_async_copy(yp_ref.at[pl.ds(global_row, SLAB), :], ys_ref.at[pl.ds(tile_row, SLAB), :], sem)

    _slab_copies(i, gs_ref, ls_ref, rc_ref, make_copy, lambda cp: cp.start())
    cols = lax.dot_general(info_ref[...], eye_ref[...], (((0,), (0,)), ((), ())),
                           precision=lax.Precision.HIGHEST, preferred_element_type=F32)
    tm = cols.shape[0]
    sid = lax.broadcasted_iota(jnp.int32, (tm, TILE_SLOTS), 1)
    sel0 = jnp.where(sid == cols[:, 2:3].astype(jnp.int32), 1.0, 0.0).astype(BF16)
    sel1 = jnp.where(sid == cols[:, 3:4].astype(jnp.int32), 1.0, 0.0).astype(BF16)
    _slab_copies(i, gs_ref, ls_ref, rc_ref, make_copy, lambda cp: cp.wait())
    ys = ys_ref[...]
    y = cols[:, 0:1] * _dot(sel0, ys) + cols[:, 1:2] * _dot(sel1, ys)
    o_ref[...] = _layer_norm(ALPHA * x_ref[...] + y, g_ref[...], b_ref[...])


def _combine_ln(x, yp, info, gstart, lstart, rc, eye, g, b):
    T = x.shape[0]
    tm = TOKEN_TILE
    full = lambda a: pl.BlockSpec(a.shape, lambda i, *_: (0,) * a.ndim)
    grid_spec = pltpu.PrefetchScalarGridSpec(
        num_scalar_prefetch=3,
        grid=(T // tm,),
        in_specs=[pl.BlockSpec((tm, D_MODEL), lambda i, *_: (i, 0)),
                  pl.BlockSpec((_IDX_ROWS, tm), lambda i, *_: (0, i)), full(eye), full(g), full(b),
                  pl.BlockSpec(memory_space=pl.ANY)],
        out_specs=pl.BlockSpec((tm, D_MODEL), lambda i, *_: (i, 0)),
        scratch_shapes=[pltpu.VMEM((TILE_SLOTS, D_MODEL), BF16), pltpu.SemaphoreType.DMA],
    )
    return pl.pallas_call(
        _combine_kernel,
        grid_spec=grid_spec,
        out_shape=jax.ShapeDtypeStruct((T, D_MODEL), F32),
        compiler_params=_params("arbitrary"),
        name="moe_combine_ln3",
    )(gstart, lstart, rc, x, info, eye, g, b, yp)


def _moe(x, consts, router_wt, router_b, w_gate, w_up, w_down, layer, g, b):
    T = x.shape[0]
    n_tiles = T // TOKEN_TILE
    P = T * TOP_K + n_tiles * N_EXPERTS * SLAB + N_EXPERTS * ROW_BLOCK
    info, rows = _router(x, router_wt, router_b, consts["tri_strict"], consts["low_strict"])
    rc = rows[:, :, 0]
    lstart = jnp.cumsum(rc, axis=1) - rc
    region = (jnp.sum(rc, axis=0) + ROW_BLOCK - 1) // ROW_BLOCK * ROW_BLOCK
    region_end = jnp.cumsum(region)
    gstart = (region_end - region)[None, :] + jnp.cumsum(rc, axis=0) - rc
    n_blk = P // ROW_BLOCK
    blk_row0 = jnp.arange(n_blk, dtype=jnp.int32) * ROW_BLOCK
    blk_e = jnp.minimum(jnp.sum(region_end[None, :] <= blk_row0[:, None], axis=1), N_EXPERTS - 1).astype(jnp.int32)
    n_used = (region_end[-1:] // ROW_BLOCK).astype(jnp.int32)
    flat = lambda t: t.reshape(-1).astype(jnp.int32)
    gstart, lstart, rc = flat(gstart), flat(lstart), flat(rc)
    xp = _dispatch(x, info, gstart, lstart, rc, P)
    yp = _expert_ffn(xp, blk_e, n_used, w_gate, w_up, w_down, layer)
    return _combine_ln(x, yp, info, gstart, lstart, rc, consts["eye"], g, b)


def _rope_tables(seq):
    inv_freq = ROPE_THETA ** (-jnp.arange(0, ROT_DIM, 2, dtype=F32) / ROT_DIM)
    ang = jnp.arange(seq, dtype=F32)[:, None] * inv_freq[None, :]
    cos, sin = jnp.cos(ang), jnp.sin(ang)
    half = ROT_DIM // 2
    one = jnp.ones((seq, HEAD_DIM - ROT_DIM), F32)
    zero = jnp.zeros((seq, HEAD_DIM - ROT_DIM), F32)
    zh = jnp.zeros((seq, half), F32)
    cos_h = jnp.concatenate([cos, cos, one], axis=1)
    sa_h = jnp.concatenate([-sin, zh, zero], axis=1)
    sb_h = jnp.concatenate([zh, sin, zero], axis=1)
    rep = lambda t: jnp.concatenate([t] * (LANES // HEAD_DIM), axis=1)
    return rep(cos_h), rep(sa_h), rep(sb_h)


def _constants():
    r = lax.broadcasted_iota(jnp.int32, (CHUNK, CHUNK), 0)
    c = lax.broadcasted_iota(jnp.int32, (CHUNK, CHUNK), 1)
    rr = lax.broadcasted_iota(jnp.int32, (TOKEN_TILE, TOKEN_TILE), 0)
    cc = lax.broadcasted_iota(jnp.int32, (TOKEN_TILE, TOKEN_TILE), 1)
    er = lax.broadcasted_iota(jnp.int32, (N_EXPERTS, N_EXPERTS), 0)
    ec = lax.broadcasted_iota(jnp.int32, (N_EXPERTS, N_EXPERTS), 1)
    return {
        "tri_u": (r <= c).astype(F32),
        "tri_l": (r >= c).astype(F32),
        "tri_strict": (rr < cc).astype(BF16),
        "low_strict": (ec < er).astype(F32),
        "eye": (lax.broadcasted_iota(jnp.int32, (_IDX_ROWS, LANES), 0)
                == lax.broadcasted_iota(jnp.int32, (_IDX_ROWS, LANES), 1)).astype(F32),
    }


def _trunk(x, mem, wts, consts):
    batch, seq, _ = x.shape
    T = batch * seq
    x = x.reshape(T, D_MODEL)
    mem2 = mem.reshape(batch * N_MEM, D_MODEL)
    rope_tabs = _rope_tables(seq)
    for l in range(DEPTH):
        qa, ka, va, mq, mk, mv, mo, gp = _in_proj(x, wts["w_main"][l], wts["wgt"][l], wts["gb"][l],
                                                  wts["conv_w"][l], wts["conv_b"][l], rope_tabs,
                                                  consts["tri_u"], consts["tri_l"], seq)
        att = _attention(qa, ka, va, wts["sink"][l], wts["att_g"][l], batch, seq)
        hf, hb = _mlstm(mq, mk, mv, gp, batch, seq)
        x = _out_proj(x, att, hf, hb, mo, wts["mlstm_g"][l], wts["w_out_a"][l], wts["w_out_m"][l],
                      wts["ln1_g"][l], wts["ln1_b"][l])
        k_mem, v_mem = _kv_proj(mem2, wts["wkv"][l])
        x = _xattn(x, k_mem, v_mem, wts["wq"][l], wts["wo"][l], wts["ln2_g"][l], wts["ln2_b"][l], batch, seq)
        x = _moe(x, consts, wts["router_wt"], wts["router_b"], wts["w_gate"], wts["w_up"], wts["w_down"], l,
                 wts["ln3_g"][l], wts["ln3_b"][l])
    return x.reshape(batch, seq, D_MODEL)


def kernel(x_prompt, x_sample, mem_prompt, mem_sample, w_in, gate_bias, conv_w, conv_b, attn_sink, attn_norm_g, mlstm_norm_g, w_out, ln1_g, ln1_b, wq_mem, wkv_mem, wo_mem, ln2_g, ln2_b, router_w, router_bias, w_gate, w_up, w_down, ln3_g, ln3_b):
    row = lambda t: t.astype(F32).reshape(DEPTH, 1, t.shape[-1])
    wts = {
        "w_main": w_in[:, :, :OFF_G].astype(BF16),
        "wgt": jnp.swapaxes(w_in[:, :, OFF_G:], 1, 2).astype(BF16),
        "gb": gate_bias.astype(F32).reshape(DEPTH, N_GATE_COLS, 1),
        "conv_w": conv_w.astype(F32),
        "conv_b": row(conv_b),
        "sink": attn_sink.astype(F32),
        "att_g": row(attn_norm_g),
        "mlstm_g": row(mlstm_norm_g),
        "w_out_a": w_out[:, :ATT_WIDTH].astype(BF16),
        "w_out_m": w_out[:, ATT_WIDTH:].astype(BF16),
        "ln1_g": row(ln1_g), "ln1_b": row(ln1_b),
        "wq": wq_mem.astype(BF16), "wkv": wkv_mem.astype(BF16), "wo": wo_mem.astype(BF16),
        "ln2_g": row(ln2_g), "ln2_b": row(ln2_b),
        "router_wt": router_w.astype(F32).T,
        "router_b": router_bias.astype(F32).reshape(N_EXPERTS, 1),
        "w_gate": w_gate.astype(BF16), "w_up": w_up.astype(BF16), "w_down": w_down.astype(BF16),
        "ln3_g": row(ln3_g), "ln3_b": row(ln3_b),
    }
    consts = _constants()
    return (_trunk(x_prompt, mem_prompt, wts, consts), _trunk(x_sample, mem_sample, wts, consts))
```

```python
import functools
import math

import jax
import jax.numpy as jnp
from jax import lax
from jax.experimental import pallas as pl
from jax.experimental.pallas import tpu as pltpu

F32 = jnp.float32
BF16 = jnp.bfloat16

D_MODEL = 1024
DEPTH = 4
HEAD_DIM = 64
N_ATT_HEADS = 8
N_KV_HEADS = 2
ATT_WIDTH = N_ATT_HEADS * HEAD_DIM
KV_WIDTH = N_KV_HEADS * HEAD_DIM
BLOCK = 128
ROT_DIM = HEAD_DIM // 4
ROPE_THETA = 500000.0
MLSTM_WIDTH = D_MODEL - ATT_WIDTH
N_MLSTM_HEADS = 4
MLSTM_HEAD_DIM = MLSTM_WIDTH // N_MLSTM_HEADS
CHUNK = 128
OFF_AQ = 0
OFF_AK = OFF_AQ + ATT_WIDTH
OFF_AV = OFF_AK + KV_WIDTH
OFF_MQ = OFF_AV + KV_WIDTH
OFF_MV = OFF_MQ + 2 * MLSTM_WIDTH
OFF_MO = OFF_MV + MLSTM_WIDTH
OFF_G = OFF_MO + MLSTM_WIDTH
N_GATE_COLS = 4 * N_MLSTM_HEADS
N_MEM = 256
N_X_HEADS = 4
X_HEAD_DIM = D_MODEL // N_X_HEADS
N_EXPERTS = 16
N_GROUPS = 4
EXPERTS_PER_GROUP = N_EXPERTS // N_GROUPS
TOP_K = 2
D_FF_EXPERT = 512
ALPHA = (2.0 * DEPTH) ** 0.25
LN_EPS = 1e-5
RMS_EPS = 1e-6
NEG = -1e30

LANES = 128
TOKEN_TILE = 512
ROW_BLOCK = 512
VMEM_LIMIT = 56 * 1024 * 1024


def _params(*sem):
    return pltpu.CompilerParams(dimension_semantics=sem, vmem_limit_bytes=VMEM_LIMIT)


def _layer_norm(z, g, b):
    mu = jnp.mean(z, axis=-1, keepdims=True)
    zc = z - mu
    var = jnp.mean(zc * zc, axis=-1, keepdims=True)
    return zc * lax.rsqrt(var + LN_EPS) * g + b


def _dot(a, b):
    return jnp.dot(a, b, preferred_element_type=F32)


def _dot_nt(a, b):
    return lax.dot_general(a, b, (((1,), (1,)), ((), ())), preferred_element_type=F32)


def _dot_tn(a, b):
    return lax.dot_general(a, b, (((0,), (0,)), ((), ())), preferred_element_type=F32)


_X_HALO = 8


_W_MK = OFF_MQ
_W_END = _W_MK + MLSTM_WIDTH
_GP_ROWS = 3 * 2 * N_MLSTM_HEADS


def _in_proj_kernel(x_ref, xp_ref, xn_ref, w_ref, wt_ref, wgt_ref, gb_ref, cwk_ref, cbk_ref, cq_ref,
                    cos_ref, sa_ref, sb_ref, tri_u_ref, tri_l_ref,
                    qa_ref, ka_ref, va_ref, mk_ref, mqt_ref, mvt_ref, mot_ref, gp_ref, rn_ref, *, n_seq_tiles):
    xb = x_ref[...].astype(BF16)
    cos = cos_ref[...]
    sa = sa_ref[...]
    sb = sb_ref[...]

    def mm(lo, hi):
        return _dot(xb, w_ref[:, lo:hi])

    def rope(t):
        return t * cos + pltpu.roll(t, LANES - ROT_DIM // 2, 1) * sa + pltpu.roll(t, ROT_DIM // 2, 1) * sb

    q = mm(OFF_AQ, OFF_AK)
    scale = 1.0 / math.sqrt(HEAD_DIM)
    for j in range(ATT_WIDTH // LANES):
        qa_ref[:, j * LANES:(j + 1) * LANES] = (rope(q[:, j * LANES:(j + 1) * LANES]) * scale).astype(BF16)
    ka_ref[...] = rope(mm(OFF_AK, OFF_AV)).astype(BF16)
    va_ref[...] = mm(OFF_AV, OFF_MQ).astype(BF16)
    tm = xb.shape[0]
    pos = pl.program_id(0) % n_seq_tiles
    has_prev = pos > 0
    has_next = pos < n_seq_tiles - 1
    halo = jnp.concatenate([xp_ref[...], xn_ref[...]], axis=0).astype(BF16)

    def silu(y):
        return y * (1.0 / (1.0 + jnp.exp(-y)))

    u = mm(_W_MK, _W_END)
    uh = _dot(halo, w_ref[:, _W_MK:_W_END])
    rowi = lax.broadcasted_iota(jnp.int32, u.shape, 0)
    u_prev = jnp.where(rowi == 0, jnp.where(has_prev, uh[_X_HALO - 1:_X_HALO, :], 0.0), pltpu.roll(u, 1, 0))
    u_next = jnp.where(rowi == tm - 1, jnp.where(has_next, uh[_X_HALO:_X_HALO + 1, :], 0.0),
                       pltpu.roll(u, tm - 1, 0))
    yk = silu(cwk_ref[0:1, :] * u_prev + cwk_ref[1:2, :] * u + cwk_ref[2:3, :] * u_next + cbk_ref[...])
    mk_ref[...] = (yk * (MLSTM_HEAD_DIM ** -0.5)).astype(BF16)

    def mm_t(lo, hi, rhs):
        return _dot_nt(wt_ref[lo:hi, :], rhs)

    ut = mm_t(0, MLSTM_WIDTH, xb)
    uht = mm_t(0, MLSTM_WIDTH, halo)
    lanei = lax.broadcasted_iota(jnp.int32, ut.shape, 1)
    ut_prev = jnp.where(lanei == 0, jnp.where(has_prev, uht[:, _X_HALO - 1:_X_HALO], 0.0), pltpu.roll(ut, 1, 1))
    ut_next = jnp.where(lanei == tm - 1, jnp.where(has_next, uht[:, _X_HALO:_X_HALO + 1], 0.0),
                        pltpu.roll(ut, tm - 1, 1))
    cq = cq_ref[...]
    mqt_ref[...] = silu(cq[:, 0:1] * ut_prev + cq[:, 1:2] * ut + cq[:, 2:3] * ut_next + cq[:, 3:4]).astype(BF16)
    mvt_ref[...] = mm_t(MLSTM_WIDTH, 2 * MLSTM_WIDTH, xb).astype(BF16)
    mot_ref[...] = mm_t(2 * MLSTM_WIDTH, 3 * MLSTM_WIDTH, xb).astype(BF16)

    gt = _dot_nt(wgt_ref[...], xb) + gb_ref[...]
    half = 2 * N_MLSTM_HEADS
    gi, gf = gt[:half], gt[half:]
    ls = jnp.minimum(gf, 0.0) - jnp.log1p(jnp.exp(-jnp.abs(gf)))
    n_chunks = tm // CHUNK
    stack = lambda t: jnp.concatenate([t[:, c * CHUNK:(c + 1) * CHUNK] for c in range(n_chunks)], axis=0)
    ls_rows, gi_rows = stack(ls), stack(gi)
    pre = jnp.dot(ls_rows, tri_u_ref[...], precision=lax.Precision.HIGHEST, preferred_element_type=F32)
    suf = jnp.dot(ls_rows, tri_l_ref[...], precision=lax.Precision.HIGHEST, preferred_element_type=F32)
    is_fwd = (lax.broadcasted_iota(jnp.int32, ls_rows.shape, 0) % half) < N_MLSTM_HEADS
    lane = lax.broadcasted_iota(jnp.int32, ls_rows.shape, 1)
    b = jnp.where(is_fwd, pre, suf)
    r = gi_rows - b
    cm_f, cm_b = r, r
    k = 1
    while k < CHUNK:
        cm_f = jnp.maximum(cm_f, jnp.where(lane >= k, pltpu.roll(cm_f, k, 1), -jnp.inf))
        cm_b = jnp.maximum(cm_b, jnp.where(lane < CHUNK - k, pltpu.roll(cm_b, CHUNK - k, 1), -jnp.inf))
        k *= 2
    cm = jnp.where(is_fwd, cm_f, cm_b)
    pad = jnp.zeros((CHUNK - half, CHUNK), F32)
    for c in range(n_chunks):
        rows = slice(c * half, (c + 1) * half)
        gp_ref[c] = jnp.concatenate([b[rows], cm[rows], r[rows]], axis=0)
        rn_ref[c * CHUNK:(c + 1) * CHUNK, :] = jnp.concatenate([r[rows], pad], axis=0).T


def _in_proj(x, w_rows, w_t, wgt, gb, cwk, cbk, cq, rope_tabs, tri_u, tri_l, seq):
    T = x.shape[0]
    tm = TOKEN_TILE
    n_seq_tiles = seq // tm
    halo_per_tile = tm // _X_HALO
    n_halo = T // _X_HALO
    cos_t, sa_t, sb_t = rope_tabs
    row_spec = lambda w: pl.BlockSpec((tm, w), lambda i: (i, 0))
    full = lambda a: pl.BlockSpec(a.shape, lambda i: (0,) * a.ndim)
    tab_spec = pl.BlockSpec((tm, LANES), lambda i: (i % n_seq_tiles, 0))
    prev_spec = pl.BlockSpec((_X_HALO, D_MODEL), lambda i: (jnp.maximum(i * halo_per_tile - 1, 0), 0))
    next_spec = pl.BlockSpec((_X_HALO, D_MODEL), lambda i: (jnp.minimum((i + 1) * halo_per_tile, n_halo - 1), 0))
    widths = (ATT_WIDTH, KV_WIDTH, KV_WIDTH, MLSTM_WIDTH)
    col_spec = _feature_major_spec(MLSTM_WIDTH, tm, n_seq_tiles)
    feat_major = jax.ShapeDtypeStruct((MLSTM_SEQS, MLSTM_WIDTH, T // MLSTM_SEQS), BF16)
    out_shapes = tuple(jax.ShapeDtypeStruct((T, w), BF16) for w in widths) + (
        feat_major, feat_major, feat_major,
        jax.ShapeDtypeStruct((T // CHUNK, _GP_ROWS, CHUNK), F32), jax.ShapeDtypeStruct((T, LANES), F32))
    out_specs = tuple(row_spec(w) for w in widths) + (
        col_spec, col_spec, col_spec,
        pl.BlockSpec((tm // CHUNK, _GP_ROWS, CHUNK), lambda i: (i, 0, 0)), row_spec(LANES))
    return pl.pallas_call(
        functools.partial(_in_proj_kernel, n_seq_tiles=n_seq_tiles),
        grid=(T // tm,),
        in_specs=[row_spec(D_MODEL), prev_spec, next_spec, full(w_rows), full(w_t), full(wgt), full(gb),
                  full(cwk), full(cbk), full(cq), tab_spec, tab_spec, tab_spec, full(tri_u), full(tri_l)],
        out_specs=out_specs,
        out_shape=out_shapes,
        compiler_params=_params("parallel"),
        name="in_proj",
    )(x, x, x, w_rows, w_t, wgt, gb, cwk, cbk, cq, cos_t, sa_t, sb_t, tri_u, tri_l)


ATT_Q_TILE = 512
_Q_BLOCKS = ATT_Q_TILE // BLOCK


def _attn_kernel(sink_ref, q_ref, kp_ref, k_ref, kn_ref, vp_ref, v_ref, vn_ref, g_ref, o_ref, *, n_tiles):
    i = pl.program_id(1)
    lane = lax.broadcasted_iota(jnp.int32, (ATT_Q_TILE + 2 * BLOCK, LANES), 1)
    low = lane < HEAD_DIM

    def split(prev_ref, own_ref, next_ref):
        t = jnp.concatenate([prev_ref[...], own_ref[...], next_ref[...]], axis=0).astype(F32)
        r = pltpu.roll(t, HEAD_DIM, 1)
        zero = jnp.zeros_like(t)
        lo = (jnp.where(low, t, zero).astype(BF16), jnp.where(low, r, zero).astype(BF16))
        hi = (jnp.where(low, zero, r).astype(BF16), jnp.where(low, zero, t).astype(BF16))
        return lo, hi

    k_lo, k_hi = split(kp_ref, k_ref, kn_ref)
    v_lo, v_hi = split(vp_ref, v_ref, vn_ref)

    rowi = lax.broadcasted_iota(jnp.int32, (BLOCK, 3 * BLOCK), 0)
    coli = lax.broadcasted_iota(jnp.int32, (BLOCK, 3 * BLOCK), 1)
    out_of_band = (coli < rowi) | (coli - 2 * BLOCK > rowi)
    band_bias = jnp.where(out_of_band, NEG, 0.0).astype(F32)
    first_bias = jnp.where(i == 0, NEG, 0.0).astype(F32)
    last_bias = jnp.where(i == n_tiles - 1, NEG, 0.0).astype(F32)
    bias_first = band_bias + jnp.where(coli < BLOCK, first_bias, 0.0)
    bias_last = band_bias + jnp.where(coli >= 2 * BLOCK, last_bias, 0.0)
    lane_o = lax.broadcasted_iota(jnp.int32, (BLOCK, LANES), 1)
    g = g_ref[...]

    for r in range(_Q_BLOCKS):
        bias = bias_first if r == 0 else (bias_last if r == _Q_BLOCKS - 1 else band_bias)
        rows = slice(r * BLOCK, (r + 1) * BLOCK)
        win = slice(r * BLOCK, (r + 3) * BLOCK)
        tiles = []
        for c in range(N_KV_HEADS):
            q2 = jnp.concatenate([q_ref[rows, (2 * c) * LANES:(2 * c + 1) * LANES],
                                  q_ref[rows, (2 * c + 1) * LANES:(2 * c + 2) * LANES]], axis=0)
            kc = jnp.concatenate([k_lo[c][win], k_hi[c][win]], axis=0)
            vc = jnp.concatenate([v_lo[c][win], v_hi[c][win]], axis=0)
            s = _dot_nt(q2, kc)
            p_rows, inv_rows = [], []
            for t in range(2):
                ps, invs = [], []
                for hh in range(2):
                    head = 4 * c + 2 * t + hh
                    sink = sink_ref[head]
                    sh = s[t * BLOCK:(t + 1) * BLOCK, hh * 3 * BLOCK:(hh + 1) * 3 * BLOCK] + bias
                    m = jnp.maximum(jnp.max(sh, axis=-1, keepdims=True), sink)
                    p = jnp.exp(sh - m)
                    den = jnp.sum(p, axis=-1, keepdims=True) + jnp.exp(sink - m)
                    ps.append(p.astype(BF16))
                    invs.append(1.0 / den)
                p_rows.append(jnp.concatenate(ps, axis=1))
                inv_rows.append(jnp.where(lane_o < HEAD_DIM, invs[0], invs[1]))
            o2 = _dot(jnp.concatenate(p_rows, axis=0), vc)
            tiles.append(o2[:BLOCK] * inv_rows[0])
            tiles.append(o2[BLOCK:] * inv_rows[1])
        o = jnp.concatenate(tiles, axis=1)
        ms = jnp.mean(o * o, axis=-1, keepdims=True)
        o_ref[rows, :] = (o * lax.rsqrt(ms + RMS_EPS) * g).astype(BF16)


def _attention(qa, ka, va, sink, att_g, batch, seq):
    T = qa.shape[0]
    n_tiles = seq // ATT_Q_TILE
    blocks_per_seq = seq // BLOCK
    own = lambda w: pl.BlockSpec((ATT_Q_TILE, w), lambda b, i: (b * n_tiles + i, 0))
    prev = pl.BlockSpec((BLOCK, KV_WIDTH),
                        lambda b, i: (b * blocks_per_seq + jnp.maximum(i * _Q_BLOCKS - 1, 0), 0))
    nxt = pl.BlockSpec((BLOCK, KV_WIDTH),
                       lambda b, i: (b * blocks_per_seq + jnp.minimum((i + 1) * _Q_BLOCKS, blocks_per_seq - 1), 0))
    return pl.pallas_call(
        functools.partial(_attn_kernel, n_tiles=n_tiles),
        grid=(batch, n_tiles),
        in_specs=[pl.BlockSpec(memory_space=pltpu.SMEM), own(ATT_WIDTH), prev, own(KV_WIDTH), nxt,
                  prev, own(KV_WIDTH), nxt, pl.BlockSpec((1, ATT_WIDTH), lambda b, i: (0, 0))],
        out_specs=own(ATT_WIDTH),
        out_shape=jax.ShapeDtypeStruct((T, ATT_WIDTH), BF16),
        compiler_params=_params("parallel", "parallel"),
        name="band_attention",
    )(sink, qa, ka, ka, ka, va, va, va, att_g)


MLSTM_SEQS = 2
_STATE_ROWS = MLSTM_HEAD_DIM + 8


def _mlstm_kernel(qt_f, vt_f, k_f, rn_f, gp_f, qt_b, vt_b, k_b, rn_b, gp_b, of_ref, ob_ref, c_state, m_state):
    @pl.when(pl.program_id(1) == 0)
    def _():
        c_state[...] = jnp.zeros_like(c_state)
        m_state[...] = jnp.zeros_like(m_state)

    key = lax.broadcasted_iota(jnp.int32, (CHUNK, CHUNK), 0)
    qry = lax.broadcasted_iota(jnp.int32, (CHUNK, CHUNK), 1)
    nh = N_MLSTM_HEADS

    units = []
    for sq in range(MLSTM_SEQS):
        for fwd, qt_ref, vt_ref, k_ref, rn_ref, gp_ref, o_ref in ((True, qt_f, vt_f, k_f, rn_f, gp_f, of_ref),
                                                                 (False, qt_b, vt_b, k_b, rn_b, gp_b, ob_ref)):
            gates = gp_ref[sq, 0]
            off = 0 if fwd else nh
            a_pos = CHUNK - 1 if fwd else 0
            for h in range(nh):
                hs = slice(h * MLSTM_HEAD_DIM, (h + 1) * MLSTM_HEAD_DIM)
                b = gates[off + h:off + h + 1, :]
                st = (2 * sq + (0 if fwd else 1)) * nh + h
                units.append(dict(
                    sq=sq, hs=hs, st=st, o_ref=o_ref, qt_ref=qt_ref, vt_ref=vt_ref, k_ref=k_ref,
                    visible=(key <= qry) if fwd else (key >= qry),
                    b=b, cm=gates[2 * nh + off + h:2 * nh + off + h + 1, :],
                    r_row=gates[4 * nh + off + h:4 * nh + off + h + 1, :],
                    r_keys=jnp.broadcast_to(rn_ref[sq, :, off + h:off + h + 1], (CHUNK, CHUNK)),
                    a=b[:, a_pos:a_pos + 1],
                    m_in=m_state[st:st + 1, 0:1]))

    for u in units:
        inter_log = u["b"] + u["m_in"]
        m_t = jnp.maximum(inter_log, u["b"] + u["cm"])
        u["decay"] = jnp.exp(jnp.where(u["visible"], u["r_keys"] + (u["b"] - m_t), NEG))
        u["inter_w"] = jnp.exp(inter_log - m_t)
        u["floor"] = jnp.exp(-m_t)
    for u in units:
        qt = u["qt_ref"][u["sq"], u["hs"], :]
        u["k"] = u["k_ref"][u["sq"], :, u["hs"]]
        u["vt"] = u["vt_ref"][u["sq"], u["hs"], :]
        u["scores"] = _dot(u["k"], qt)
        u["c_in"] = c_state[u["st"]]
        u["inter"] = _dot(u["c_in"].astype(BF16), qt)
    for u in units:
        sw = u["scores"] * u["decay"]
        num = _dot(u["vt"], sw.astype(BF16)) + u["inter_w"] * u["inter"][:MLSTM_HEAD_DIM]
        den = (jnp.sum(sw, axis=0, keepdims=True)
               + u["inter_w"] * u["inter"][MLSTM_HEAD_DIM:MLSTM_HEAD_DIM + 1])
        u["o_ref"][u["sq"], u["hs"], :] = (num / jnp.maximum(jnp.abs(den), u["floor"])).astype(u["o_ref"].dtype)
    for u in units:
        a, m_in, st = u["a"], u["m_in"], u["st"]
        g_max = a + jnp.max(u["r_row"], axis=-1, keepdims=True)
        kw = u["k"].astype(F32) * jnp.exp(u["r_keys"] + (a - g_max))
        m_new = jnp.maximum(a + m_in, g_max)
        keep = jnp.exp(a + m_in - m_new)
        add = jnp.exp(g_max - m_new)
        c_state[st, 0:MLSTM_HEAD_DIM, :] = keep * u["c_in"][:MLSTM_HEAD_DIM] + add * _dot(u["vt"], kw.astype(BF16))
        c_state[st, MLSTM_HEAD_DIM:MLSTM_HEAD_DIM + 1, :] = (
            keep * u["c_in"][MLSTM_HEAD_DIM:MLSTM_HEAD_DIM + 1] + add * jnp.sum(kw, axis=0, keepdims=True))
        m_state[st:st + 1, :] = jnp.broadcast_to(m_new, (1, LANES))


def _feature_major_spec(width, tm, tiles_per_seq):
    def index(i):
        s = i // tiles_per_seq
        return (s % MLSTM_SEQS, 0, (s // MLSTM_SEQS) * tiles_per_seq + i % tiles_per_seq)
    return pl.BlockSpec((None, width, tm), index)


def _mlstm(mqt, mvt, mk, rn, gp, batch, seq):
    T = mk.shape[0]
    nc = seq // CHUNK
    groups = batch // MLSTM_SEQS
    mk = mk.reshape(groups, MLSTM_SEQS, seq, MLSTM_WIDTH)
    rn = rn.reshape(groups, MLSTM_SEQS, seq, LANES)
    gp = gp.reshape(groups, MLSTM_SEQS, nc, _GP_ROWS, CHUNK)

    def specs(chunk_of):
        feat = pl.BlockSpec((MLSTM_SEQS, MLSTM_WIDTH, CHUNK), lambda b, i: (0, 0, b * nc + chunk_of(i)))
        return [feat, feat,
                pl.BlockSpec((None, MLSTM_SEQS, CHUNK, MLSTM_WIDTH), lambda b, i: (b, 0, chunk_of(i), 0)),
                pl.BlockSpec((None, MLSTM_SEQS, CHUNK, LANES), lambda b, i: (b, 0, chunk_of(i), 0)),
                pl.BlockSpec((None, MLSTM_SEQS, 1, _GP_ROWS, CHUNK), lambda b, i: (b, 0, chunk_of(i), 0, 0))]

    fwd_chunk = lambda i: i
    bwd_chunk = lambda i: nc - 1 - i
    out = jax.ShapeDtypeStruct((MLSTM_SEQS, MLSTM_WIDTH, T // MLSTM_SEQS), BF16)
    n_state = 2 * MLSTM_SEQS * N_MLSTM_HEADS
    return pl.pallas_call(
        _mlstm_kernel,
        grid=(groups, nc),
        in_specs=specs(fwd_chunk) + specs(bwd_chunk),
        out_specs=(specs(fwd_chunk)[0], specs(bwd_chunk)[0]),
        out_shape=(out, out),
        scratch_shapes=[pltpu.VMEM((n_state, _STATE_ROWS, MLSTM_HEAD_DIM), F32),
                        pltpu.VMEM((n_state, LANES), F32)],
        compiler_params=_params("parallel", "arbitrary"),
        name="mlstm",
    )(mqt, mvt, mk, rn, gp, mqt, mvt, mk, rn, gp)


def _out_proj_kernel(x_ref, att_ref, hf_ref, hb_ref, mo_ref, mg_ref, wa_ref, wm_ref, g_ref, b_ref, o_ref):
    h = hf_ref[...].astype(F32) + hb_ref[...].astype(F32)
    parts = []
    for hd in range(N_MLSTM_HEADS):
        hh = h[hd * MLSTM_HEAD_DIM:(hd + 1) * MLSTM_HEAD_DIM]
        ms = jnp.mean(hh * hh, axis=0, keepdims=True)
        parts.append(hh * lax.rsqrt(ms + RMS_EPS))
    hn = jnp.concatenate(parts, axis=0) * mg_ref[...]
    gate = 1.0 / (1.0 + jnp.exp(-mo_ref[...].astype(F32)))
    y = _dot(att_ref[...], wa_ref[...]) + _dot_tn((hn * gate).astype(BF16), wm_ref[...])
    o_ref[...] = _layer_norm(ALPHA * x_ref[...] + y, g_ref[...], b_ref[...])


def _out_proj(x, att, hft, hbt, mot, mg, wa, wm, g, b, seq):
    T = x.shape[0]
    tm = TOKEN_TILE
    row = lambda w: pl.BlockSpec((tm, w), lambda i: (i, 0))
    feat = _feature_major_spec(MLSTM_WIDTH, tm, seq // tm)
    full = lambda a: pl.BlockSpec(a.shape, lambda i: (0,) * a.ndim)
    return pl.pallas_call(
        _out_proj_kernel,
        grid=(T // tm,),
        in_specs=[row(D_MODEL), row(ATT_WIDTH), feat, feat, feat,
                  full(mg), full(wa), full(wm), full(g), full(b)],
        out_specs=row(D_MODEL),
        out_shape=jax.ShapeDtypeStruct((T, D_MODEL), F32),
        compiler_params=_params("parallel"),
        name="out_proj_ln1",
    )(x, att, hft, hbt, mot, mg, wa, wm, g, b)


def _kv_proj_kernel(m_ref, w_ref, k_ref, v_ref):
    mb = m_ref[...].astype(BF16)
    k_ref[...] = _dot(mb, w_ref[:, :D_MODEL]).astype(BF16)
    v_ref[...] = _dot(mb, w_ref[:, D_MODEL:]).astype(BF16)


def _kv_proj(mem, wkv):
    M = mem.shape[0]
    tm = TOKEN_TILE
    row = pl.BlockSpec((tm, D_MODEL), lambda i: (i, 0))
    out = jax.ShapeDtypeStruct((M, D_MODEL), BF16)
    return pl.pallas_call(
        _kv_proj_kernel,
        grid=(M // tm,),
        in_specs=[row, pl.BlockSpec(wkv.shape, lambda i: (0, 0))],
        out_specs=(row, row),
        out_shape=(out, out),
        compiler_params=_params("parallel"),
        name="mem_kv_proj",
    )(mem, wkv)


def _xattn_kernel(x_ref, k_ref, v_ref, wq_ref, wo_ref, g_ref, b_ref, o_ref):
    x = x_ref[...]
    q = _dot(x.astype(BF16), wq_ref[...])
    scale = 1.0 / math.sqrt(X_HEAD_DIM)
    outs = []
    for h in range(N_X_HEADS):
        hs = slice(h * X_HEAD_DIM, (h + 1) * X_HEAD_DIM)
        s = _dot_nt((q[:, hs] * scale).astype(BF16), k_ref[:, hs])
        m = jnp.max(s, axis=-1, keepdims=True)
        p = jnp.exp(s - m)
        inv = 1.0 / jnp.sum(p, axis=-1, keepdims=True)
        outs.append((_dot(p.astype(BF16), v_ref[:, hs]) * inv).astype(BF16))
    y = _dot(jnp.concatenate(outs, axis=1), wo_ref[...])
    o_ref[...] = _layer_norm(ALPHA * x + y, g_ref[...], b_ref[...])


def _xattn(x, k_mem, v_mem, wq, wo, g, b, batch, seq):
    T = x.shape[0]
    tm = TOKEN_TILE
    nt = seq // tm
    row = pl.BlockSpec((tm, D_MODEL), lambda bb, i: (bb * nt + i, 0))
    mem = pl.BlockSpec((N_MEM, D_MODEL), lambda bb, i: (bb, 0))
    full = lambda a: pl.BlockSpec(a.shape, lambda bb, i: (0,) * a.ndim)
    return pl.pallas_call(
        _xattn_kernel,
        grid=(batch, nt),
        in_specs=[row, mem, mem, full(wq), full(wo), full(g), full(b)],
        out_specs=row,
        out_shape=jax.ShapeDtypeStruct((T, D_MODEL), F32),
        compiler_params=_params("parallel", "parallel"),
        name="mem_xattn_ln2",
    )(x, k_mem, v_mem, wq, wo, g, b)


_IDX_ROWS = 8
SLAB = 16
TILE_SLOTS = TOP_K * TOKEN_TILE + N_EXPERTS * SLAB


def _router_kernel(x_ref, wt_ref, b_ref, tri_ref, low_ref, info_ref, cnt_ref):
    logits = lax.dot_general(wt_ref[...], x_ref[...], (((1,), (1,)), ((), ())),
                             precision=lax.Precision.HIGHEST, preferred_element_type=F32)
    s = 1.0 / (1.0 + jnp.exp(-logits))
    sel = s + b_ref[...]
    srow = lambda e: s[e:e + 1, :]
    brow = lambda e: sel[e:e + 1, :]
    best = None
    gi = None
    for gidx in range(N_GROUPS):
        vals = [brow(gidx * EXPERTS_PER_GROUP + j) for j in range(EXPERTS_PER_GROUP)]
        top2 = None
        for a in range(EXPERTS_PER_GROUP):
            for b in range(a + 1, EXPERTS_PER_GROUP):
                pair = vals[a] + vals[b]
                top2 = pair if top2 is None else jnp.maximum(top2, pair)
        if best is None:
            best, gi = top2, jnp.zeros(top2.shape, jnp.int32)
        else:
            better = top2 > best
            gi = jnp.where(better, gidx, gi)
            best = jnp.where(better, top2, best)

    def in_group(rowfn, j):
        out = rowfn(j)
        for gidx in range(1, N_GROUPS):
            out = jnp.where(gi == gidx, rowfn(gidx * EXPERTS_PER_GROUP + j), out)
        return out

    bv = [in_group(brow, j) for j in range(EXPERTS_PER_GROUP)]
    sv = [in_group(srow, j) for j in range(EXPERTS_PER_GROUP)]

    def argmax_first(vals):
        bi = jnp.zeros(vals[0].shape, jnp.int32)
        bm = vals[0]
        for j in range(1, len(vals)):
            better = vals[j] > bm
            bi = jnp.where(better, j, bi)
            bm = jnp.where(better, vals[j], bm)
        return bi

    i1 = argmax_first(bv)
    i2 = argmax_first([jnp.where(i1 == j, -jnp.inf, bv[j]) for j in range(EXPERTS_PER_GROUP)])

    def pick(vals, idx):
        out = vals[0]
        for j in range(1, len(vals)):
            out = jnp.where(idx == j, vals[j], out)
        return out

    w1 = pick(sv, i1)
    w2 = pick(sv, i2)
    tot = w1 + w2
    tm = logits.shape[1]
    eid = lax.broadcasted_iota(jnp.int32, (N_EXPERTS, tm), 0)
    oh1 = (eid == gi * EXPERTS_PER_GROUP + i1).astype(F32)
    oh2 = (eid == gi * EXPERTS_PER_GROUP + i2).astype(F32)
    before1 = _dot(oh1.astype(BF16), tri_ref[...])
    before2 = _dot(oh2.astype(BF16), tri_ref[...])
    c1 = jnp.sum(oh1, axis=1, keepdims=True)
    cnt = c1 + jnp.sum(oh2, axis=1, keepdims=True)
    rows = jnp.floor((cnt + (SLAB - 1)) * (1.0 / SLAB)) * SLAB
    rows_b = jnp.broadcast_to(rows, (N_EXPERTS, LANES))
    start = jnp.dot(low_ref[...], rows_b, precision=lax.Precision.HIGHEST, preferred_element_type=F32)[:, 0:1]
    slot1 = jnp.sum(oh1 * (start + before1), axis=0, keepdims=True)
    slot2 = jnp.sum(oh2 * (start + c1 + before2), axis=0, keepdims=True)
    zf = jnp.zeros((_IDX_ROWS - 2 * TOP_K, tm), F32)
    info_ref[...] = jnp.concatenate([w1 / tot, w2 / tot, slot1, slot2, zf], axis=0)
    cnt_ref[0] = rows_b.astype(jnp.int32)


def _router(x, router_wt, router_b, tri_strict, low_strict):
    T = x.shape[0]
    tm = TOKEN_TILE
    full = lambda a: pl.BlockSpec(a.shape, lambda i: (0,) * a.ndim)
    return pl.pallas_call(
        _router_kernel,
        grid=(T // tm,),
        in_specs=[pl.BlockSpec((tm, D_MODEL), lambda i: (i, 0)), full(router_wt), full(router_b),
                  full(tri_strict), full(low_strict)],
        out_specs=(pl.BlockSpec((_IDX_ROWS, tm), lambda i: (0, i)),
                   pl.BlockSpec((1, N_EXPERTS, LANES), lambda i: (i, 0, 0))),
        out_shape=(jax.ShapeDtypeStruct((_IDX_ROWS, T), F32),
                   jax.ShapeDtypeStruct((T // tm, N_EXPERTS, LANES), jnp.int32)),
        compiler_params=_params("parallel"),
        name="router",
    )(x, router_wt, router_b, tri_strict, low_strict)


def _slab_copies(i, gs_ref, ls_ref, rc_ref, make_copy, op):
    for e in range(N_EXPERTS):
        idx = i * N_EXPERTS + e
        ls = ls_ref[idx]
        gs = gs_ref[idx]

        def body(j, carry, ls=ls, gs=gs):
            op(make_copy(pl.multiple_of(ls + j * SLAB, SLAB), pl.multiple_of(gs + j * SLAB, SLAB)))
            return carry

        lax.fori_loop(0, rc_ref[idx] // SLAB, body, 0)


_DISPATCH_CHUNK = 256


def _dispatch_kernel(gs_ref, ls_ref, rc_ref, x_ref, info_ref, zero_ref, xp_ref, xs_ref, sem):
    del zero_ref
    i = pl.program_id(0)
    slot0 = info_ref[2:3, :].astype(jnp.int32)
    slot1 = info_ref[3:4, :].astype(jnp.int32)
    xb = x_ref[...].astype(BF16)
    tm = xb.shape[0]
    for c in range(TILE_SLOTS // _DISPATCH_CHUNK):
        sid = lax.broadcasted_iota(jnp.int32, (_DISPATCH_CHUNK, tm), 0) + c * _DISPATCH_CHUNK
        sel = jnp.where((sid == slot0) | (sid == slot1), 1.0, 0.0).astype(BF16)
        xs_ref[c * _DISPATCH_CHUNK:(c + 1) * _DISPATCH_CHUNK, :] = _dot(sel, xb).astype(BF16)

    def make_copy(tile_row, global_row):
        return pltpu.make_async_copy(xs_ref.at[pl.ds(tile_row, SLAB), :], xp_ref.at[pl.ds(global_row, SLAB), :], sem)

    _slab_copies(i, gs_ref, ls_ref, rc_ref, make_copy, lambda cp: cp.start())
    _slab_copies(i, gs_ref, ls_ref, rc_ref, make_copy, lambda cp: cp.wait())


def _dispatch(x, info, gstart, lstart, rc, n_rows):
    T = x.shape[0]
    tm = TOKEN_TILE
    hbm = pl.BlockSpec(memory_space=pl.ANY)
    grid_spec = pltpu.PrefetchScalarGridSpec(
        num_scalar_prefetch=3,
        grid=(T // tm,),
        in_specs=[pl.BlockSpec((tm, D_MODEL), lambda i, *_: (i, 0)),
                  pl.BlockSpec((_IDX_ROWS, tm), lambda i, *_: (0, i)), hbm],
        out_specs=hbm,
        scratch_shapes=[pltpu.VMEM((TILE_SLOTS, D_MODEL), BF16), pltpu.SemaphoreType.DMA],
    )
    return pl.pallas_call(
        _dispatch_kernel,
        grid_spec=grid_spec,
        out_shape=jax.ShapeDtypeStruct((n_rows, D_MODEL), BF16),
        input_output_aliases={5: 0},
        compiler_params=_params("arbitrary"),
        name="moe_dispatch",
    )(gstart, lstart, rc, x, info, jnp.zeros((n_rows, D_MODEL), BF16))


def _ffn_kernel(be_ref, nu_ref, x_ref, wg_ref, wu_ref, wd_ref, o_ref):
    i = pl.program_id(0)

    @pl.when(i < nu_ref[0])
    def _():
        xb = x_ref[...]
        gate = _dot(xb, wg_ref[...])
        up = _dot(xb, wu_ref[...])
        h = gate * (1.0 / (1.0 + jnp.exp(-gate))) * up
        o_ref[...] = _dot(h.astype(BF16), wd_ref[...]).astype(BF16)

    @pl.when(i >= nu_ref[0])
    def _():
        o_ref[...] = jnp.zeros_like(o_ref)


def _expert_ffn(xp, blk_e, n_used, w_gate, w_up, w_down, layer):
    P = xp.shape[0]
    wspec = lambda shp: pl.BlockSpec((None, None) + shp, lambda i, be, nu: (layer, be[i], 0, 0))
    grid_spec = pltpu.PrefetchScalarGridSpec(
        num_scalar_prefetch=2,
        grid=(P // ROW_BLOCK,),
        in_specs=[pl.BlockSpec((ROW_BLOCK, D_MODEL), lambda i, be, nu: (i, 0)),
                  wspec((D_MODEL, D_FF_EXPERT)), wspec((D_MODEL, D_FF_EXPERT)), wspec((D_FF_EXPERT, D_MODEL))],
        out_specs=pl.BlockSpec((ROW_BLOCK, D_MODEL), lambda i, be, nu: (i, 0)),
    )
    return pl.pallas_call(
        _ffn_kernel,
        grid_spec=grid_spec,
        out_shape=jax.ShapeDtypeStruct((P, D_MODEL), BF16),
        compiler_params=_params("arbitrary"),
        name="expert_ffn",
    )(blk_e, n_used, xp, w_gate, w_up, w_down)


def _combine_kernel(gs_ref, ls_ref, rc_ref, x_ref, info_ref, eye_ref, g_ref, b_ref, yp_ref, o_ref, ys_ref, sem):
    i = pl.program_id(0)

    @pl.when(i == 0)
    def _():
        ys_ref[...] = jnp.zeros_like(ys_ref)

    def make_copy(tile_row, global_row):
        return pltpu.make_async_copy(yp_ref.at[pl.ds(global_row, SLAB), :], ys_ref.at[pl.ds(tile_row, SLAB), :], sem)

    _slab_copies(i, gs_ref, ls_ref, rc_ref, make_copy, lambda cp: cp.start())
    cols = lax.dot_general(info_ref[...], eye_ref[...], (((0,), (0,)), ((), ())),
                           precision=lax.Precision.HIGHEST, preferred_element_type=F32)
    tm = cols.shape[0]
    sid = lax.broadcasted_iota(jnp.int32, (tm, TILE_SLOTS), 1)
    sel0 = jnp.where(sid == cols[:, 2:3].astype(jnp.int32), 1.0, 0.0).astype(BF16)
    sel1 = jnp.where(sid == cols[:, 3:4].astype(jnp.int32), 1.0, 0.0).astype(BF16)
    _slab_copies(i, gs_ref, ls_ref, rc_ref, make_copy, lambda cp: cp.wait())
    ys = ys_ref[...]
    y = cols[:, 0:1] * _dot(sel0, ys) + cols[:, 1:2] * _dot(sel1, ys)
    o_ref[...] = _layer_norm(ALPHA * x_ref[...] + y, g_ref[...], b_ref[...])


def _combine_ln(x, yp, info, gstart, lstart, rc, eye, g, b):
    T = x.shape[0]
    tm = TOKEN_TILE
    full = lambda a: pl.BlockSpec(a.shape, lambda i, *_: (0,) * a.ndim)
    grid_spec = pltpu.PrefetchScalarGridSpec(
        num_scalar_prefetch=3,
        grid=(T // tm,),
        in_specs=[pl.BlockSpec((tm, D_MODEL), lambda i, *_: (i, 0)),
                  pl.BlockSpec((_IDX_ROWS, tm), lambda i, *_: (0, i)), full(eye), full(g), full(b),
                  pl.BlockSpec(memory_space=pl.ANY)],
        out_specs=pl.BlockSpec((tm, D_MODEL), lambda i, *_: (i, 0)),
        scratch_shapes=[pltpu.VMEM((TILE_SLOTS, D_MODEL), BF16), pltpu.SemaphoreType.DMA],
    )
    return pl.pallas_call(
        _combine_kernel,
        grid_spec=grid_spec,
        out_shape=jax.ShapeDtypeStruct((T, D_MODEL), F32),
        compiler_params=_params("arbitrary"),
        name="moe_combine_ln3",
    )(gstart, lstart, rc, x, info, eye, g, b, yp)


def _moe(x, consts, router_wt, router_b, w_gate, w_up, w_down, layer, g, b):
    T = x.shape[0]
    n_tiles = T // TOKEN_TILE
    P = T * TOP_K + n_tiles * N_EXPERTS * SLAB + N_EXPERTS * ROW_BLOCK
    info, rows = _router(x, router_wt, router_b, consts["tri_strict"], consts["low_strict"])
    rc = rows[:, :, 0]
    lstart = jnp.cumsum(rc, axis=1) - rc
    region = (jnp.sum(rc, axis=0) + ROW_BLOCK - 1) // ROW_BLOCK * ROW_BLOCK
    region_end = jnp.cumsum(region)
    gstart = (region_end - region)[None, :] + jnp.cumsum(rc, axis=0) - rc
    n_blk = P // ROW_BLOCK
    blk_row0 = jnp.arange(n_blk, dtype=jnp.int32) * ROW_BLOCK
    blk_e = jnp.minimum(jnp.sum(region_end[None, :] <= blk_row0[:, None], axis=1), N_EXPERTS - 1).astype(jnp.int32)
    n_used = (region_end[-1:] // ROW_BLOCK).astype(jnp.int32)
    flat = lambda t: t.reshape(-1).astype(jnp.int32)
    gstart, lstart, rc = flat(gstart), flat(lstart), flat(rc)
    xp = _dispatch(x, info, gstart, lstart, rc, P)
    yp = _expert_ffn(xp, blk_e, n_used, w_gate, w_up, w_down, layer)
    return _combine_ln(x, yp, info, gstart, lstart, rc, consts["eye"], g, b)


def _rope_tables(seq):
    inv_freq = ROPE_THETA ** (-jnp.arange(0, ROT_DIM, 2, dtype=F32) / ROT_DIM)
    ang = jnp.arange(seq, dtype=F32)[:, None] * inv_freq[None, :]
    cos, sin = jnp.cos(ang), jnp.sin(ang)
    half = ROT_DIM // 2
    one = jnp.ones((seq, HEAD_DIM - ROT_DIM), F32)
    zero = jnp.zeros((seq, HEAD_DIM - ROT_DIM), F32)
    zh = jnp.zeros((seq, half), F32)
    cos_h = jnp.concatenate([cos, cos, one], axis=1)
    sa_h = jnp.concatenate([-sin, zh, zero], axis=1)
    sb_h = jnp.concatenate([zh, sin, zero], axis=1)
    rep = lambda t: jnp.concatenate([t] * (LANES // HEAD_DIM), axis=1)
    return rep(cos_h), rep(sa_h), rep(sb_h)


def _constants():
    r = lax.broadcasted_iota(jnp.int32, (CHUNK, CHUNK), 0)
    c = lax.broadcasted_iota(jnp.int32, (CHUNK, CHUNK), 1)
    rr = lax.broadcasted_iota(jnp.int32, (TOKEN_TILE, TOKEN_TILE), 0)
    cc = lax.broadcasted_iota(jnp.int32, (TOKEN_TILE, TOKEN_TILE), 1)
    er = lax.broadcasted_iota(jnp.int32, (N_EXPERTS, N_EXPERTS), 0)
    ec = lax.broadcasted_iota(jnp.int32, (N_EXPERTS, N_EXPERTS), 1)
    return {
        "tri_u": (r <= c).astype(F32),
        "tri_l": (r >= c).astype(F32),
        "tri_strict": (rr < cc).astype(BF16),
        "low_strict": (ec < er).astype(F32),
        "eye": (lax.broadcasted_iota(jnp.int32, (_IDX_ROWS, LANES), 0)
                == lax.broadcasted_iota(jnp.int32, (_IDX_ROWS, LANES), 1)).astype(F32),
    }


def _trunk(x, mem, wts, consts):
    batch, seq, _ = x.shape
    T = batch * seq
    x = x.reshape(T, D_MODEL)
    mem2 = mem.reshape(batch * N_MEM, D_MODEL)
    rope_tabs = _rope_tables(seq)
    for l in range(DEPTH):
        qa, ka, va, mk, mqt, mvt, mot, gp, rn = _in_proj(
            x, wts["w_rows"][l], wts["w_t"][l], wts["wgt"][l], wts["gb"][l], wts["conv_wk"][l], wts["conv_bk"][l],
            wts["conv_q"][l], rope_tabs, consts["tri_u"], consts["tri_l"], seq)
        att = _attention(qa, ka, va, wts["sink"][l], wts["att_g"][l], batch, seq)
        hft, hbt = _mlstm(mqt, mvt, mk, rn, gp, batch, seq)
        x = _out_proj(x, att, hft, hbt, mot, wts["mlstm_g"][l], wts["w_out_a"][l], wts["w_out_m"][l],
                      wts["ln1_g"][l], wts["ln1_b"][l], seq)
        k_mem, v_mem = _kv_proj(mem2, wts["wkv"][l])
        x = _xattn(x, k_mem, v_mem, wts["wq"][l], wts["wo"][l], wts["ln2_g"][l], wts["ln2_b"][l], batch, seq)
        x = _moe(x, consts, wts["router_wt"], wts["router_b"], wts["w_gate"], wts["w_up"], wts["w_down"], l,
                 wts["ln3_g"][l], wts["ln3_b"][l])
    return x.reshape(batch, seq, D_MODEL)


def _prepare_weights(w_in, gate_bias, conv_w, conv_b, attn_sink, attn_norm_g, mlstm_norm_g, w_out, ln1_g, ln1_b,
                     wq_mem, wkv_mem, wo_mem, ln2_g, ln2_b, router_w, router_bias, w_gate, w_up, w_down, ln3_g, ln3_b):
    row = lambda t: t.astype(F32).reshape(DEPTH, 1, t.shape[-1])
    nh = N_MLSTM_HEADS
    gate_order = jnp.array(list(range(0, nh)) + list(range(2 * nh, 3 * nh))
                           + list(range(nh, 2 * nh)) + list(range(3 * nh, 4 * nh)), jnp.int32)
    cq = jnp.concatenate([jnp.swapaxes(conv_w[:, :, :MLSTM_WIDTH], 1, 2), conv_b[:, :MLSTM_WIDTH, None],
                          jnp.zeros((DEPTH, MLSTM_WIDTH, 4), conv_w.dtype)], axis=2).astype(F32)
    feature_major = jnp.concatenate([w_in[:, :, OFF_MQ:OFF_MQ + MLSTM_WIDTH], w_in[:, :, OFF_MV:OFF_G]], axis=2)
    return {
        "w_rows": jnp.concatenate([w_in[:, :, :OFF_MQ], w_in[:, :, OFF_MQ + MLSTM_WIDTH:OFF_MV]],
                                  axis=2).astype(BF16),
        "w_t": jnp.swapaxes(feature_major, 1, 2).astype(BF16),
        "wgt": jnp.swapaxes(w_in[:, :, OFF_G:], 1, 2)[:, gate_order].astype(BF16),
        "gb": gate_bias.astype(F32)[:, gate_order].reshape(DEPTH, N_GATE_COLS, 1),
        "conv_wk": conv_w[:, :, MLSTM_WIDTH:].astype(F32),
        "conv_bk": row(conv_b[:, MLSTM_WIDTH:]),
        "conv_q": cq,
        "sink": attn_sink.astype(F32),
        "att_g": row(attn_norm_g),
        "mlstm_g": mlstm_norm_g.astype(F32).reshape(DEPTH, MLSTM_WIDTH, 1),
        "w_out_a": w_out[:, :ATT_WIDTH].astype(BF16),
        "w_out_m": w_out[:, ATT_WIDTH:].astype(BF16),
        "ln1_g": row(ln1_g), "ln1_b": row(ln1_b),
        "wq": wq_mem.astype(BF16), "wkv": wkv_mem.astype(BF16), "wo": wo_mem.astype(BF16),
        "ln2_g": row(ln2_g), "ln2_b": row(ln2_b),
        "router_wt": router_w.astype(F32).T,
        "router_b": router_bias.astype(F32).reshape(N_EXPERTS, 1),
        "w_gate": w_gate.astype(BF16), "w_up": w_up.astype(BF16), "w_down": w_down.astype(BF16),
        "ln3_g": row(ln3_g), "ln3_b": row(ln3_b),
    }


def kernel(x_prompt, x_sample, mem_prompt, mem_sample, w_in, gate_bias, conv_w, conv_b, attn_sink, attn_norm_g, mlstm_norm_g, w_out, ln1_g, ln1_b, wq_mem, wkv_mem, wo_mem, ln2_g, ln2_b, router_w, router_bias, w_gate, w_up, w_down, ln3_g, ln3_b):
    wts = _prepare_weights(w_in, gate_bias, conv_w, conv_b, attn_sink, attn_norm_g, mlstm_norm_g, w_out, ln1_g, ln1_b,
                           wq_mem, wkv_mem, wo_mem, ln2_g, ln2_b, router_w, router_bias, w_gate, w_up, w_down,
                           ln3_g, ln3_b)
    consts = _constants()
    return (_trunk(x_prompt, mem_prompt, wts, consts), _trunk(x_sample, mem_sample, wts, consts))
```

```python
import functools
import math

import jax
import jax.numpy as jnp
from jax import lax
from jax.experimental import pallas as pl
from jax.experimental.pallas import tpu as pltpu

F32 = jnp.float32
BF16 = jnp.bfloat16

D_MODEL = 1024
DEPTH = 4
HEAD_DIM = 64
N_ATT_HEADS = 8
N_KV_HEADS = 2
ATT_WIDTH = N_ATT_HEADS * HEAD_DIM
KV_WIDTH = N_KV_HEADS * HEAD_DIM
BLOCK = 128
ROT_DIM = HEAD_DIM // 4
ROPE_THETA = 500000.0
MLSTM_WIDTH = D_MODEL - ATT_WIDTH
N_MLSTM_HEADS = 4
MLSTM_HEAD_DIM = MLSTM_WIDTH // N_MLSTM_HEADS
CHUNK = 128
OFF_AQ = 0
OFF_AK = OFF_AQ + ATT_WIDTH
OFF_AV = OFF_AK + KV_WIDTH
OFF_MQ = OFF_AV + KV_WIDTH
OFF_MV = OFF_MQ + 2 * MLSTM_WIDTH
OFF_MO = OFF_MV + MLSTM_WIDTH
OFF_G = OFF_MO + MLSTM_WIDTH
N_GATE_COLS = 4 * N_MLSTM_HEADS
N_MEM = 256
N_X_HEADS = 4
X_HEAD_DIM = D_MODEL // N_X_HEADS
N_EXPERTS = 16
N_GROUPS = 4
EXPERTS_PER_GROUP = N_EXPERTS // N_GROUPS
TOP_K = 2
D_FF_EXPERT = 512
ALPHA = (2.0 * DEPTH) ** 0.25
LN_EPS = 1e-5
RMS_EPS = 1e-6
NEG = -1e30

LANES = 128
TOKEN_TILE = 512
ROW_BLOCK = 512
VMEM_LIMIT = 56 * 1024 * 1024


def _params(*sem):
    return pltpu.CompilerParams(dimension_semantics=sem, vmem_limit_bytes=VMEM_LIMIT)


def _layer_norm(z, g, b):
    mu = jnp.mean(z, axis=-1, keepdims=True)
    zc = z - mu
    var = jnp.mean(zc * zc, axis=-1, keepdims=True)
    return zc * lax.rsqrt(var + LN_EPS) * g + b


def _dot(a, b):
    return jnp.dot(a, b, preferred_element_type=F32)


def _dot_nt(a, b):
    return lax.dot_general(a, b, (((1,), (1,)), ((), ())), preferred_element_type=F32)


def _dot_tn(a, b):
    return lax.dot_general(a, b, (((0,), (0,)), ((), ())), preferred_element_type=F32)


_X_HALO = 8


_W_MK = OFF_MQ
_W_END = _W_MK + MLSTM_WIDTH
_GP_ROWS = 3 * 2 * N_MLSTM_HEADS


def _in_proj_kernel(x_ref, xp_ref, xn_ref, w_ref, wt_ref, wgt_ref, gb_ref, cwk_ref, cbk_ref, cq_ref,
                    cos_ref, sa_ref, sb_ref, tri_u_ref, tri_l_ref,
                    qa_ref, ka_ref, va_ref, mk_ref, mqt_ref, mvt_ref, mot_ref, gp_ref, rn_ref, *, n_seq_tiles):
    xb = x_ref[...].astype(BF16)
    cos = cos_ref[...]
    sa = sa_ref[...]
    sb = sb_ref[...]

    def mm(lo, hi):
        return _dot(xb, w_ref[:, lo:hi])

    def rope(t):
        return t * cos + pltpu.roll(t, LANES - ROT_DIM // 2, 1) * sa + pltpu.roll(t, ROT_DIM // 2, 1) * sb

    q = mm(OFF_AQ, OFF_AK)
    scale = 1.0 / math.sqrt(HEAD_DIM)
    for j in range(ATT_WIDTH // LANES):
        qa_ref[:, j * LANES:(j + 1) * LANES] = (rope(q[:, j * LANES:(j + 1) * LANES]) * scale).astype(BF16)
    ka_ref[...] = rope(mm(OFF_AK, OFF_AV)).astype(BF16)
    va_ref[...] = mm(OFF_AV, OFF_MQ).astype(BF16)
    tm = xb.shape[0]
    pos = pl.program_id(0) % n_seq_tiles
    has_prev = pos > 0
    has_next = pos < n_seq_tiles - 1
    halo = jnp.concatenate([xp_ref[...], xn_ref[...]], axis=0).astype(BF16)

    def silu(y):
        return y * (1.0 / (1.0 + jnp.exp(-y)))

    u = mm(_W_MK, _W_END)
    uh = _dot(halo, w_ref[:, _W_MK:_W_END])
    rowi = lax.broadcasted_iota(jnp.int32, u.shape, 0)
    u_prev = jnp.where(rowi == 0, jnp.where(has_prev, uh[_X_HALO - 1:_X_HALO, :], 0.0), pltpu.roll(u, 1, 0))
    u_next = jnp.where(rowi == tm - 1, jnp.where(has_next, uh[_X_HALO:_X_HALO + 1, :], 0.0),
                       pltpu.roll(u, tm - 1, 0))
    yk = silu(cwk_ref[0:1, :] * u_prev + cwk_ref[1:2, :] * u + cwk_ref[2:3, :] * u_next + cbk_ref[...])
    mk_ref[...] = (yk * (MLSTM_HEAD_DIM ** -0.5)).astype(BF16)

    def mm_t(lo, hi, rhs):
        return _dot_nt(wt_ref[lo:hi, :], rhs)

    ut = mm_t(0, MLSTM_WIDTH, xb)
    uht = mm_t(0, MLSTM_WIDTH, halo)
    lanei = lax.broadcasted_iota(jnp.int32, ut.shape, 1)
    ut_prev = jnp.where(lanei == 0, jnp.where(has_prev, uht[:, _X_HALO - 1:_X_HALO], 0.0), pltpu.roll(ut, 1, 1))
    ut_next = jnp.where(lanei == tm - 1, jnp.where(has_next, uht[:, _X_HALO:_X_HALO + 1], 0.0),
                        pltpu.roll(ut, tm - 1, 1))
    cq = cq_ref[...]
    mqt_ref[...] = silu(cq[:, 0:1] * ut_prev + cq[:, 1:2] * ut + cq[:, 2:3] * ut_next + cq[:, 3:4]).astype(BF16)
    mvt_ref[...] = mm_t(MLSTM_WIDTH, 2 * MLSTM_WIDTH, xb).astype(BF16)
    mot_ref[...] = mm_t(2 * MLSTM_WIDTH, 3 * MLSTM_WIDTH, xb).astype(BF16)

    gt = _dot_nt(wgt_ref[...], xb) + gb_ref[...]
    half = 2 * N_MLSTM_HEADS
    gi, gf = gt[:half], gt[half:]
    ls = jnp.minimum(gf, 0.0) - jnp.log1p(jnp.exp(-jnp.abs(gf)))
    n_chunks = tm // CHUNK
    stack = lambda t: jnp.concatenate([t[:, c * CHUNK:(c + 1) * CHUNK] for c in range(n_chunks)], axis=0)
    ls_rows, gi_rows = stack(ls), stack(gi)
    pre = jnp.dot(ls_rows, tri_u_ref[...], precision=lax.Precision.HIGHEST, preferred_element_type=F32)
    suf = jnp.dot(ls_rows, tri_l_ref[...], precision=lax.Precision.HIGHEST, preferred_element_type=F32)
    is_fwd = (lax.broadcasted_iota(jnp.int32, ls_rows.shape, 0) % half) < N_MLSTM_HEADS
    lane = lax.broadcasted_iota(jnp.int32, ls_rows.shape, 1)
    b = jnp.where(is_fwd, pre, suf)
    r = gi_rows - b
    cm_f, cm_b = r, r
    k = 1
    while k < CHUNK:
        cm_f = jnp.maximum(cm_f, jnp.where(lane >= k, pltpu.roll(cm_f, k, 1), -jnp.inf))
        cm_b = jnp.maximum(cm_b, jnp.where(lane < CHUNK - k, pltpu.roll(cm_b, CHUNK - k, 1), -jnp.inf))
        k *= 2
    cm = jnp.where(is_fwd, cm_f, cm_b)
    pad = jnp.zeros((CHUNK - half, CHUNK), F32)
    for c in range(n_chunks):
        rows = slice(c * half, (c + 1) * half)
        gp_ref[c] = jnp.concatenate([b[rows], cm[rows], r[rows]], axis=0)
        rn_ref[c * CHUNK:(c + 1) * CHUNK, :] = jnp.concatenate([r[rows], pad], axis=0).T


def _in_proj(x, w_rows, w_t, wgt, gb, cwk, cbk, cq, rope_tabs, tri_u, tri_l, seq):
    T = x.shape[0]
    tm = TOKEN_TILE
    n_seq_tiles = seq // tm
    halo_per_tile = tm // _X_HALO
    n_halo = T // _X_HALO
    cos_t, sa_t, sb_t = rope_tabs
    row_spec = lambda w: pl.BlockSpec((tm, w), lambda i: (i, 0))
    full = lambda a: pl.BlockSpec(a.shape, lambda i: (0,) * a.ndim)
    tab_spec = pl.BlockSpec((tm, LANES), lambda i: (i % n_seq_tiles, 0))
    prev_spec = pl.BlockSpec((_X_HALO, D_MODEL), lambda i: (jnp.maximum(i * halo_per_tile - 1, 0), 0))
    next_spec = pl.BlockSpec((_X_HALO, D_MODEL), lambda i: (jnp.minimum((i + 1) * halo_per_tile, n_halo - 1), 0))
    widths = (ATT_WIDTH, KV_WIDTH, KV_WIDTH, MLSTM_WIDTH)
    col_spec = _feature_major_spec(MLSTM_WIDTH, tm, n_seq_tiles)
    feat_major = jax.ShapeDtypeStruct((MLSTM_SEQS, MLSTM_WIDTH, T // MLSTM_SEQS), BF16)
    out_shapes = tuple(jax.ShapeDtypeStruct((T, w), BF16) for w in widths) + (
        feat_major, feat_major, feat_major,
        jax.ShapeDtypeStruct((T // CHUNK, _GP_ROWS, CHUNK), F32), jax.ShapeDtypeStruct((T, LANES), F32))
    out_specs = tuple(row_spec(w) for w in widths) + (
        col_spec, col_spec, col_spec,
        pl.BlockSpec((tm // CHUNK, _GP_ROWS, CHUNK), lambda i: (i, 0, 0)), row_spec(LANES))
    return pl.pallas_call(
        functools.partial(_in_proj_kernel, n_seq_tiles=n_seq_tiles),
        grid=(T // tm,),
        in_specs=[row_spec(D_MODEL), prev_spec, next_spec, full(w_rows), full(w_t), full(wgt), full(gb),
                  full(cwk), full(cbk), full(cq), tab_spec, tab_spec, tab_spec, full(tri_u), full(tri_l)],
        out_specs=out_specs,
        out_shape=out_shapes,
        compiler_params=_params("parallel"),
        name="in_proj",
    )(x, x, x, w_rows, w_t, wgt, gb, cwk, cbk, cq, cos_t, sa_t, sb_t, tri_u, tri_l)


ATT_Q_TILE = 512
_Q_BLOCKS = ATT_Q_TILE // BLOCK


def _attn_kernel(sink_ref, q_ref, kp_ref, k_ref, kn_ref, vp_ref, v_ref, vn_ref, g_ref, o_ref, *, n_tiles):
    i = pl.program_id(1)
    lane = lax.broadcasted_iota(jnp.int32, (ATT_Q_TILE + 2 * BLOCK, LANES), 1)
    low = lane < HEAD_DIM

    def split(prev_ref, own_ref, next_ref):
        t = jnp.concatenate([prev_ref[...], own_ref[...], next_ref[...]], axis=0).astype(F32)
        r = pltpu.roll(t, HEAD_DIM, 1)
        zero = jnp.zeros_like(t)
        lo = (jnp.where(low, t, zero).astype(BF16), jnp.where(low, r, zero).astype(BF16))
        hi = (jnp.where(low, zero, r).astype(BF16), jnp.where(low, zero, t).astype(BF16))
        return lo, hi

    k_lo, k_hi = split(kp_ref, k_ref, kn_ref)
    v_lo, v_hi = split(vp_ref, v_ref, vn_ref)

    rowi = lax.broadcasted_iota(jnp.int32, (BLOCK, 3 * BLOCK), 0)
    coli = lax.broadcasted_iota(jnp.int32, (BLOCK, 3 * BLOCK), 1)
    out_of_band = (coli < rowi) | (coli - 2 * BLOCK > rowi)
    band_bias = jnp.where(out_of_band, NEG, 0.0).astype(F32)
    first_bias = jnp.where(i == 0, NEG, 0.0).astype(F32)
    last_bias = jnp.where(i == n_tiles - 1, NEG, 0.0).astype(F32)
    bias_first = band_bias + jnp.where(coli < BLOCK, first_bias, 0.0)
    bias_last = band_bias + jnp.where(coli >= 2 * BLOCK, last_bias, 0.0)
    lane_o = lax.broadcasted_iota(jnp.int32, (BLOCK, LANES), 1)
    g = g_ref[...]

    for r in range(_Q_BLOCKS):
        bias = bias_first if r == 0 else (bias_last if r == _Q_BLOCKS - 1 else band_bias)
        rows = slice(r * BLOCK, (r + 1) * BLOCK)
        win = slice(r * BLOCK, (r + 3) * BLOCK)
        tiles = []
        for c in range(N_KV_HEADS):
            q2 = jnp.concatenate([q_ref[rows, (2 * c) * LANES:(2 * c + 1) * LANES],
                                  q_ref[rows, (2 * c + 1) * LANES:(2 * c + 2) * LANES]], axis=0)
            kc = jnp.concatenate([k_lo[c][win], k_hi[c][win]], axis=0)
            vc = jnp.concatenate([v_lo[c][win], v_hi[c][win]], axis=0)
            s = _dot_nt(q2, kc)
            p_rows, inv_rows = [], []
            for t in range(2):
                ps, invs = [], []
                for hh in range(2):
                    head = 4 * c + 2 * t + hh
                    sink = sink_ref[head]
                    sh = s[t * BLOCK:(t + 1) * BLOCK, hh * 3 * BLOCK:(hh + 1) * 3 * BLOCK] + bias
                    m = jnp.maximum(jnp.max(sh, axis=-1, keepdims=True), sink)
                    p = jnp.exp(sh - m)
                    den = jnp.sum(p, axis=-1, keepdims=True) + jnp.exp(sink - m)
                    ps.append(p.astype(BF16))
                    invs.append(1.0 / den)
                p_rows.append(jnp.concatenate(ps, axis=1))
                inv_rows.append(jnp.where(lane_o < HEAD_DIM, invs[0], invs[1]))
            o2 = _dot(jnp.concatenate(p_rows, axis=0), vc)
            tiles.append(o2[:BLOCK] * inv_rows[0])
            tiles.append(o2[BLOCK:] * inv_rows[1])
        o = jnp.concatenate(tiles, axis=1)
        ms = jnp.mean(o * o, axis=-1, keepdims=True)
        o_ref[rows, :] = (o * lax.rsqrt(ms + RMS_EPS) * g).astype(BF16)


def _attention(qa, ka, va, sink, att_g, batch, seq):
    T = qa.shape[0]
    n_tiles = seq // ATT_Q_TILE
    blocks_per_seq = seq // BLOCK
    own = lambda w: pl.BlockSpec((ATT_Q_TILE, w), lambda b, i: (b * n_tiles + i, 0))
    prev = pl.BlockSpec((BLOCK, KV_WIDTH),
                        lambda b, i: (b * blocks_per_seq + jnp.maximum(i * _Q_BLOCKS - 1, 0), 0))
    nxt = pl.BlockSpec((BLOCK, KV_WIDTH),
                       lambda b, i: (b * blocks_per_seq + jnp.minimum((i + 1) * _Q_BLOCKS, blocks_per_seq - 1), 0))
    return pl.pallas_call(
        functools.partial(_attn_kernel, n_tiles=n_tiles),
        grid=(batch, n_tiles),
        in_specs=[pl.BlockSpec(memory_space=pltpu.SMEM), own(ATT_WIDTH), prev, own(KV_WIDTH), nxt,
                  prev, own(KV_WIDTH), nxt, pl.BlockSpec((1, ATT_WIDTH), lambda b, i: (0, 0))],
        out_specs=own(ATT_WIDTH),
        out_shape=jax.ShapeDtypeStruct((T, ATT_WIDTH), BF16),
        compiler_params=_params("parallel", "parallel"),
        name="band_attention",
    )(sink, qa, ka, ka, ka, va, va, va, att_g)


MLSTM_SEQS = 2
_STATE_ROWS = MLSTM_HEAD_DIM + 8


def _mlstm_kernel(qt_f, vt_f, k_f, rn_f, gp_f, qt_b, vt_b, k_b, rn_b, gp_b, of_ref, ob_ref, c_state, m_state):
    @pl.when(pl.program_id(1) == 0)
    def _():
        c_state[...] = jnp.zeros_like(c_state)
        m_state[...] = jnp.zeros_like(m_state)

    key = lax.broadcasted_iota(jnp.int32, (CHUNK, CHUNK), 0)
    qry = lax.broadcasted_iota(jnp.int32, (CHUNK, CHUNK), 1)
    nh = N_MLSTM_HEADS

    units = []
    for sq in range(MLSTM_SEQS):
        for fwd, qt_ref, vt_ref, k_ref, rn_ref, gp_ref, o_ref in ((True, qt_f, vt_f, k_f, rn_f, gp_f, of_ref),
                                                                 (False, qt_b, vt_b, k_b, rn_b, gp_b, ob_ref)):
            gates = gp_ref[sq, 0]
            off = 0 if fwd else nh
            a_pos = CHUNK - 1 if fwd else 0
            for h in range(nh):
                hs = slice(h * MLSTM_HEAD_DIM, (h + 1) * MLSTM_HEAD_DIM)
                b = gates[off + h:off + h + 1, :]
                st = (2 * sq + (0 if fwd else 1)) * nh + h
                units.append(dict(
                    sq=sq, hs=hs, st=st, o_ref=o_ref, qt_ref=qt_ref, vt_ref=vt_ref, k_ref=k_ref,
                    visible=(key <= qry) if fwd else (key >= qry),
                    b=b, cm=gates[2 * nh + off + h:2 * nh + off + h + 1, :],
                    r_row=gates[4 * nh + off + h:4 * nh + off + h + 1, :],
                    r_keys=jnp.broadcast_to(rn_ref[sq, :, off + h:off + h + 1], (CHUNK, CHUNK)),
                    a=b[:, a_pos:a_pos + 1],
                    m_in=m_state[st:st + 1, 0:1]))

    for u in units:
        inter_log = u["b"] + u["m_in"]
        m_t = jnp.maximum(inter_log, u["b"] + u["cm"])
        u["decay"] = jnp.exp(jnp.where(u["visible"], u["r_keys"] + (u["b"] - m_t), NEG))
        u["inter_w"] = jnp.exp(inter_log - m_t)
        u["floor"] = jnp.exp(-m_t)
    for u in units:
        qt = u["qt_ref"][u["sq"], u["hs"], :]
        u["k"] = u["k_ref"][u["sq"], :, u["hs"]]
        u["vt"] = u["vt_ref"][u["sq"], u["hs"], :]
        u["scores"] = _dot(u["k"], qt)
        u["c_in"] = c_state[u["st"]]
        u["inter"] = _dot(u["c_in"].astype(BF16), qt)
    for u in units:
        sw = u["scores"] * u["decay"]
        num = _dot(u["vt"], sw.astype(BF16)) + u["inter_w"] * u["inter"][:MLSTM_HEAD_DIM]
        den = (jnp.sum(sw, axis=0, keepdims=True)
               + u["inter_w"] * u["inter"][MLSTM_HEAD_DIM:MLSTM_HEAD_DIM + 1])
        u["o_ref"][u["sq"], u["hs"], :] = (num / jnp.maximum(jnp.abs(den), u["floor"])).astype(u["o_ref"].dtype)
    for u in units:
        a, m_in, st = u["a"], u["m_in"], u["st"]
        g_max = a + jnp.max(u["r_row"], axis=-1, keepdims=True)
        kw = u["k"].astype(F32) * jnp.exp(u["r_keys"] + (a - g_max))
        m_new = jnp.maximum(a + m_in, g_max)
        keep = jnp.exp(a + m_in - m_new)
        add = jnp.exp(g_max - m_new)
        c_state[st, 0:MLSTM_HEAD_DIM, :] = keep * u["c_in"][:MLSTM_HEAD_DIM] + add * _dot(u["vt"], kw.astype(BF16))
        c_state[st, MLSTM_HEAD_DIM:MLSTM_HEAD_DIM + 1, :] = (
            keep * u["c_in"][MLSTM_HEAD_DIM:MLSTM_HEAD_DIM + 1] + add * jnp.sum(kw, axis=0, keepdims=True))
        m_state[st:st + 1, :] = jnp.broadcast_to(m_new, (1, LANES))


def _feature_major_spec(width, tm, tiles_per_seq):
    def index(i):
        s = i // tiles_per_seq
        return (s % MLSTM_SEQS, 0, (s // MLSTM_SEQS) * tiles_per_seq + i % tiles_per_seq)
    return pl.BlockSpec((None, width, tm), index)


def _mlstm(mqt, mvt, mk, rn, gp, batch, seq):
    T = mk.shape[0]
    nc = seq // CHUNK
    groups = batch // MLSTM_SEQS
    mk = mk.reshape(groups, MLSTM_SEQS, seq, MLSTM_WIDTH)
    rn = rn.reshape(groups, MLSTM_SEQS, seq, LANES)
    gp = gp.reshape(groups, MLSTM_SEQS, nc, _GP_ROWS, CHUNK)

    def specs(chunk_of):
        feat = pl.BlockSpec((MLSTM_SEQS, MLSTM_WIDTH, CHUNK), lambda b, i: (0, 0, b * nc + chunk_of(i)))
        return [feat, feat,
                pl.BlockSpec((None, MLSTM_SEQS, CHUNK, MLSTM_WIDTH), lambda b, i: (b, 0, chunk_of(i), 0)),
                pl.BlockSpec((None, MLSTM_SEQS, CHUNK, LANES), lambda b, i: (b, 0, chunk_of(i), 0)),
                pl.BlockSpec((None, MLSTM_SEQS, 1, _GP_ROWS, CHUNK), lambda b, i: (b, 0, chunk_of(i), 0, 0))]

    fwd_chunk = lambda i: i
    bwd_chunk = lambda i: nc - 1 - i
    out = jax.ShapeDtypeStruct((MLSTM_SEQS, MLSTM_WIDTH, T // MLSTM_SEQS), BF16)
    n_state = 2 * MLSTM_SEQS * N_MLSTM_HEADS
    return pl.pallas_call(
        _mlstm_kernel,
        grid=(groups, nc),
        in_specs=specs(fwd_chunk) + specs(bwd_chunk),
        out_specs=(specs(fwd_chunk)[0], specs(bwd_chunk)[0]),
        out_shape=(out, out),
        scratch_shapes=[pltpu.VMEM((n_state, _STATE_ROWS, MLSTM_HEAD_DIM), F32),
                        pltpu.VMEM((n_state, LANES), F32)],
        compiler_params=_params("parallel", "arbitrary"),
        name="mlstm",
    )(mqt, mvt, mk, rn, gp, mqt, mvt, mk, rn, gp)


def _out_proj_kernel(x_ref, att_ref, hf_ref, hb_ref, mo_ref, mg_ref, wa_ref, wm_ref, g_ref, b_ref, o_ref):
    h = hf_ref[...].astype(F32) + hb_ref[...].astype(F32)
    parts = []
    for hd in range(N_MLSTM_HEADS):
        hh = h[hd * MLSTM_HEAD_DIM:(hd + 1) * MLSTM_HEAD_DIM]
        ms = jnp.mean(hh * hh, axis=0, keepdims=True)
        parts.append(hh * lax.rsqrt(ms + RMS_EPS))
    hn = jnp.concatenate(parts, axis=0) * mg_ref[...]
    gate = 1.0 / (1.0 + jnp.exp(-mo_ref[...].astype(F32)))
    y = _dot(att_ref[...], wa_ref[...]) + _dot_tn((hn * gate).astype(BF16), wm_ref[...])
    o_ref[...] = _layer_norm(ALPHA * x_ref[...] + y, g_ref[...], b_ref[...])


def _out_proj(x, att, hft, hbt, mot, mg, wa, wm, g, b, seq):
    T = x.shape[0]
    tm = TOKEN_TILE
    row = lambda w: pl.BlockSpec((tm, w), lambda i: (i, 0))
    feat = _feature_major_spec(MLSTM_WIDTH, tm, seq // tm)
    full = lambda a: pl.BlockSpec(a.shape, lambda i: (0,) * a.ndim)
    return pl.pallas_call(
        _out_proj_kernel,
        grid=(T // tm,),
        in_specs=[row(D_MODEL), row(ATT_WIDTH), feat, feat, feat,
                  full(mg), full(wa), full(wm), full(g), full(b)],
        out_specs=row(D_MODEL),
        out_shape=jax.ShapeDtypeStruct((T, D_MODEL), F32),
        compiler_params=_params("parallel"),
        name="out_proj_ln1",
    )(x, att, hft, hbt, mot, mg, wa, wm, g, b)


def _kv_proj_kernel(m_ref, w_ref, k_ref, v_ref):
    mb = m_ref[...].astype(BF16)
    k_ref[...] = _dot(mb, w_ref[:, :D_MODEL]).astype(BF16)
    v_ref[...] = _dot(mb, w_ref[:, D_MODEL:]).astype(BF16)


def _kv_proj(mem, wkv):
    M = mem.shape[0]
    tm = TOKEN_TILE
    row = pl.BlockSpec((tm, D_MODEL), lambda i: (i, 0))
    out = jax.ShapeDtypeStruct((M, D_MODEL), BF16)
    return pl.pallas_call(
        _kv_proj_kernel,
        grid=(M // tm,),
        in_specs=[row, pl.BlockSpec(wkv.shape, lambda i: (0, 0))],
        out_specs=(row, row),
        out_shape=(out, out),
        compiler_params=_params("parallel"),
        name="mem_kv_proj",
    )(mem, wkv)


def _xattn_kernel(x_ref, k_ref, v_ref, wq_ref, wo_ref, g_ref, b_ref, rwt_ref, rb_ref, tri_ref, low_ref,
                  o_ref, info_ref, cnt_ref):
    x = x_ref[...]
    q = _dot(x.astype(BF16), wq_ref[...])
    scale = 1.0 / math.sqrt(X_HEAD_DIM)
    outs = []
    for h in range(N_X_HEADS):
        hs = slice(h * X_HEAD_DIM, (h + 1) * X_HEAD_DIM)
        s = _dot_nt((q[:, hs] * scale).astype(BF16), k_ref[:, hs])
        m = jnp.max(s, axis=-1, keepdims=True)
        p = jnp.exp(s - m)
        inv = 1.0 / jnp.sum(p, axis=-1, keepdims=True)
        outs.append((_dot(p.astype(BF16), v_ref[:, hs]) * inv).astype(BF16))
    y = _dot(jnp.concatenate(outs, axis=1), wo_ref[...])
    x2 = _layer_norm(ALPHA * x + y, g_ref[...], b_ref[...])
    o_ref[...] = x2
    _route(x2, rwt_ref, rb_ref, tri_ref, low_ref, info_ref, cnt_ref)


def _xattn(x, k_mem, v_mem, wq, wo, g, b, router_wt, router_b, tri_strict, low_strict, batch, seq):
    T = x.shape[0]
    tm = TOKEN_TILE
    nt = seq // tm
    row = pl.BlockSpec((tm, D_MODEL), lambda bb, i: (bb * nt + i, 0))
    mem = pl.BlockSpec((N_MEM, D_MODEL), lambda bb, i: (bb, 0))
    full = lambda a: pl.BlockSpec(a.shape, lambda bb, i: (0,) * a.ndim)
    return pl.pallas_call(
        _xattn_kernel,
        grid=(batch, nt),
        in_specs=[row, mem, mem, full(wq), full(wo), full(g), full(b), full(router_wt), full(router_b),
                  full(tri_strict), full(low_strict)],
        out_specs=(row, pl.BlockSpec((_IDX_ROWS, tm), lambda bb, i: (0, bb * nt + i)),
                   pl.BlockSpec((1, N_EXPERTS, LANES), lambda bb, i: (bb * nt + i, 0, 0))),
        out_shape=(jax.ShapeDtypeStruct((T, D_MODEL), F32), jax.ShapeDtypeStruct((_IDX_ROWS, T), F32),
                   jax.ShapeDtypeStruct((T // tm, N_EXPERTS, LANES), jnp.int32)),
        compiler_params=_params("parallel", "parallel"),
        name="mem_xattn_ln2_route",
    )(x, k_mem, v_mem, wq, wo, g, b, router_wt, router_b, tri_strict, low_strict)


_IDX_ROWS = 8
SLAB = 16
TILE_SLOTS = TOP_K * TOKEN_TILE + N_EXPERTS * SLAB


def _route(x, wt_ref, b_ref, tri_ref, low_ref, info_ref, cnt_ref):
    x_hi = x.astype(BF16)
    x_lo = (x - x_hi.astype(F32)).astype(BF16)
    by_hi = _dot_nt(wt_ref[...], x_hi)
    logits = by_hi[:N_EXPERTS] + by_hi[N_EXPERTS:] + _dot_nt(wt_ref[0:N_EXPERTS, :], x_lo)
    s = 1.0 / (1.0 + jnp.exp(-logits))
    sel = s + b_ref[...]
    srow = lambda e: s[e:e + 1, :]
    brow = lambda e: sel[e:e + 1, :]
    best = None
    gi = None
    for gidx in range(N_GROUPS):
        vals = [brow(gidx * EXPERTS_PER_GROUP + j) for j in range(EXPERTS_PER_GROUP)]
        top2 = None
        for a in range(EXPERTS_PER_GROUP):
            for b in range(a + 1, EXPERTS_PER_GROUP):
                pair = vals[a] + vals[b]
                top2 = pair if top2 is None else jnp.maximum(top2, pair)
        if best is None:
            best, gi = top2, jnp.zeros(top2.shape, jnp.int32)
        else:
            better = top2 > best
            gi = jnp.where(better, gidx, gi)
            best = jnp.where(better, top2, best)

    def in_group(rowfn, j):
        out = rowfn(j)
        for gidx in range(1, N_GROUPS):
            out = jnp.where(gi == gidx, rowfn(gidx * EXPERTS_PER_GROUP + j), out)
        return out

    bv = [in_group(brow, j) for j in range(EXPERTS_PER_GROUP)]
    sv = [in_group(srow, j) for j in range(EXPERTS_PER_GROUP)]

    def argmax_first(vals):
        bi = jnp.zeros(vals[0].shape, jnp.int32)
        bm = vals[0]
        for j in range(1, len(vals)):
            better = vals[j] > bm
            bi = jnp.where(better, j, bi)
            bm = jnp.where(better, vals[j], bm)
        return bi

    i1 = argmax_first(bv)
    i2 = argmax_first([jnp.where(i1 == j, -jnp.inf, bv[j]) for j in range(EXPERTS_PER_GROUP)])

    def pick(vals, idx):
        out = vals[0]
        for j in range(1, len(vals)):
            out = jnp.where(idx == j, vals[j], out)
        return out

    w1 = pick(sv, i1)
    w2 = pick(sv, i2)
    tot = w1 + w2
    tm = logits.shape[1]
    eid = lax.broadcasted_iota(jnp.int32, (N_EXPERTS, tm), 0)
    oh1 = (eid == gi * EXPERTS_PER_GROUP + i1).astype(F32)
    oh2 = (eid == gi * EXPERTS_PER_GROUP + i2).astype(F32)
    before1 = _dot(oh1.astype(BF16), tri_ref[...])
    before2 = _dot(oh2.astype(BF16), tri_ref[...])
    c1 = jnp.sum(oh1, axis=1, keepdims=True)
    cnt = c1 + jnp.sum(oh2, axis=1, keepdims=True)
    rows = jnp.floor((cnt + (SLAB - 1)) * (1.0 / SLAB)) * SLAB
    rows_b = jnp.broadcast_to(rows, (N_EXPERTS, LANES))
    start = jnp.dot(low_ref[...], rows_b, precision=lax.Precision.HIGHEST, preferred_element_type=F32)[:, 0:1]
    slot1 = jnp.sum(oh1 * (start + before1), axis=0, keepdims=True)
    slot2 = jnp.sum(oh2 * (start + c1 + before2), axis=0, keepdims=True)
    zf = jnp.zeros((_IDX_ROWS - 2 * TOP_K, tm), F32)
    info_ref[...] = jnp.concatenate([w1 / tot, w2 / tot, slot1, slot2, zf], axis=0)
    cnt_ref[0] = rows_b.astype(jnp.int32)


def _slab_copies(i, gs_ref, ls_ref, rc_ref, make_copy, op):
    for e in range(N_EXPERTS):
        idx = i * N_EXPERTS + e
        ls = ls_ref[idx]
        gs = gs_ref[idx]

        def body(j, carry, ls=ls, gs=gs):
            op(make_copy(pl.multiple_of(ls + j * SLAB, SLAB), pl.multiple_of(gs + j * SLAB, SLAB)))
            return carry

        lax.fori_loop(0, rc_ref[idx] // SLAB, body, 0)


_DISPATCH_CHUNK = 256
_ZERO_ROWS = 128


def _dispatch_kernel(gs_ref, ls_ref, rc_ref, ts_ref, tn_ref, x_ref, info_ref, xp_ref, xs_ref, zero_ref, sem, zsem):
    i = pl.program_id(0)
    last = pl.num_programs(0) - 1
    buf = i % 2
    slot0 = info_ref[2:3, :].astype(jnp.int32)
    slot1 = info_ref[3:4, :].astype(jnp.int32)
    xb = x_ref[...].astype(BF16)
    tm = xb.shape[0]
    for c in range(TILE_SLOTS // _DISPATCH_CHUNK):
        sid = lax.broadcasted_iota(jnp.int32, (_DISPATCH_CHUNK, tm), 0) + c * _DISPATCH_CHUNK
        sel = jnp.where((sid == slot0) | (sid == slot1), 1.0, 0.0).astype(BF16)
        xs_ref[buf, c * _DISPATCH_CHUNK:(c + 1) * _DISPATCH_CHUNK, :] = _dot(sel, xb).astype(BF16)

    def copies(tile, which, op):
        def make_copy(tile_row, global_row):
            return pltpu.make_async_copy(xs_ref.at[which, pl.ds(tile_row, SLAB), :],
                                         xp_ref.at[pl.ds(global_row, SLAB), :], sem.at[which])
        _slab_copies(tile, gs_ref, ls_ref, rc_ref, make_copy, op)

    @pl.when(i > 0)
    def _():
        copies(i - 1, 1 - buf, lambda cp: cp.wait())

    copies(i, buf, lambda cp: cp.start())

    @pl.when(i == last)
    def _():
        zero_ref[...] = jnp.zeros_like(zero_ref)

        def tail(op):
            for e in range(N_EXPERTS + 1):
                t0 = ts_ref[e]
                rows = SLAB if e < N_EXPERTS else _ZERO_ROWS

                def body(j, carry, t0=t0, rows=rows):
                    op(pltpu.make_async_copy(zero_ref.at[pl.ds(0, rows), :],
                                             xp_ref.at[pl.ds(pl.multiple_of(t0 + j * rows, rows), rows), :], zsem))
                    return carry

                lax.fori_loop(0, tn_ref[e] // rows, body, 0)

        tail(lambda cp: cp.start())
        copies(i, buf, lambda cp: cp.wait())
        tail(lambda cp: cp.wait())


def _dispatch(x, info, gstart, lstart, rc, tail_start, tail_rows, n_rows):
    T = x.shape[0]
    tm = TOKEN_TILE
    grid_spec = pltpu.PrefetchScalarGridSpec(
        num_scalar_prefetch=5,
        grid=(T // tm,),
        in_specs=[pl.BlockSpec((tm, D_MODEL), lambda i, *_: (i, 0)),
                  pl.BlockSpec((_IDX_ROWS, tm), lambda i, *_: (0, i))],
        out_specs=pl.BlockSpec(memory_space=pl.ANY),
        scratch_shapes=[pltpu.VMEM((2, TILE_SLOTS, D_MODEL), BF16), pltpu.VMEM((_ZERO_ROWS, D_MODEL), BF16),
                        pltpu.SemaphoreType.DMA((2,)), pltpu.SemaphoreType.DMA],
    )
    return pl.pallas_call(
        _dispatch_kernel,
        grid_spec=grid_spec,
        out_shape=jax.ShapeDtypeStruct((n_rows, D_MODEL), BF16),
        compiler_params=_params("arbitrary"),
        name="moe_dispatch",
    )(gstart, lstart, rc, tail_start, tail_rows, x, info)


def _ffn_kernel(be_ref, nu_ref, x_ref, wg_ref, wu_ref, wd_ref, o_ref):
    i = pl.program_id(0)

    @pl.when(i < nu_ref[0])
    def _():
        xb = x_ref[...]
        gate = _dot(xb, wg_ref[...])
        up = _dot(xb, wu_ref[...])
        h = gate * (1.0 / (1.0 + jnp.exp(-gate))) * up
        o_ref[...] = _dot(h.astype(BF16), wd_ref[...]).astype(BF16)

    @pl.when(i >= nu_ref[0])
    def _():
        o_ref[...] = jnp.zeros_like(o_ref)


def _expert_ffn(xp, blk_e, n_used, w_gate, w_up, w_down, layer):
    P = xp.shape[0]
    wspec = lambda shp: pl.BlockSpec((None, None) + shp, lambda i, be, nu: (layer, be[i], 0, 0))
    grid_spec = pltpu.PrefetchScalarGridSpec(
        num_scalar_prefetch=2,
        grid=(P // ROW_BLOCK,),
        in_specs=[pl.BlockSpec((ROW_BLOCK, D_MODEL), lambda i, be, nu: (jnp.minimum(i, nu[0] - 1), 0)),
                  wspec((D_MODEL, D_FF_EXPERT)), wspec((D_MODEL, D_FF_EXPERT)), wspec((D_FF_EXPERT, D_MODEL))],
        out_specs=pl.BlockSpec((ROW_BLOCK, D_MODEL), lambda i, be, nu: (i, 0)),
    )
    return pl.pallas_call(
        _ffn_kernel,
        grid_spec=grid_spec,
        out_shape=jax.ShapeDtypeStruct((P, D_MODEL), BF16),
        compiler_params=_params("arbitrary"),
        name="expert_ffn",
    )(blk_e, n_used, xp, w_gate, w_up, w_down)


def _combine_kernel(gs_ref, ls_ref, rc_ref, x_ref, info_ref, eye_ref, g_ref, b_ref, yp_ref, o_ref, ys_ref, sem):
    i = pl.program_id(0)
    buf = i % 2

    def copies(tile, which, op):
        def make_copy(tile_row, global_row):
            return pltpu.make_async_copy(yp_ref.at[pl.ds(global_row, SLAB), :],
                                         ys_ref.at[which, pl.ds(tile_row, SLAB), :], sem.at[which])
        _slab_copies(tile, gs_ref, ls_ref, rc_ref, make_copy, op)

    @pl.when(i == 0)
    def _():
        ys_ref[...] = jnp.zeros_like(ys_ref)
        copies(i, buf, lambda cp: cp.start())

    @pl.when(i + 1 < pl.num_programs(0))
    def _():
        copies(i + 1, 1 - buf, lambda cp: cp.start())

    cols = lax.dot_general(info_ref[...], eye_ref[...], (((0,), (0,)), ((), ())),
                           precision=lax.Precision.HIGHEST, preferred_element_type=F32)
    tm = cols.shape[0]
    sid = lax.broadcasted_iota(jnp.int32, (tm, TILE_SLOTS), 1)
    weights = (jnp.where(sid == cols[:, 2:3].astype(jnp.int32), cols[:, 0:1], 0.0)
               + jnp.where(sid == cols[:, 3:4].astype(jnp.int32), cols[:, 1:2], 0.0)).astype(BF16)
    copies(i, buf, lambda cp: cp.wait())
    y = _dot(weights, ys_ref[buf])
    o_ref[...] = _layer_norm(ALPHA * x_ref[...] + y, g_ref[...], b_ref[...])


def _combine_ln(x, yp, info, gstart, lstart, rc, eye, g, b):
    T = x.shape[0]
    tm = TOKEN_TILE
    full = lambda a: pl.BlockSpec(a.shape, lambda i, *_: (0,) * a.ndim)
    grid_spec = pltpu.PrefetchScalarGridSpec(
        num_scalar_prefetch=3,
        grid=(T // tm,),
        in_specs=[pl.BlockSpec((tm, D_MODEL), lambda i, *_: (i, 0)),
                  pl.BlockSpec((_IDX_ROWS, tm), lambda i, *_: (0, i)), full(eye), full(g), full(b),
                  pl.BlockSpec(memory_space=pl.ANY)],
        out_specs=pl.BlockSpec((tm, D_MODEL), lambda i, *_: (i, 0)),
        scratch_shapes=[pltpu.VMEM((2, TILE_SLOTS, D_MODEL), BF16), pltpu.SemaphoreType.DMA((2,))],
    )
    return pl.pallas_call(
        _combine_kernel,
        grid_spec=grid_spec,
        out_shape=jax.ShapeDtypeStruct((T, D_MODEL), F32),
        compiler_params=_params("arbitrary"),
        name="moe_combine_ln3",
    )(gstart, lstart, rc, x, info, eye, g, b, yp)


def _moe(x, info, rows, consts, w_gate, w_up, w_down, layer, g, b):
    T = x.shape[0]
    n_tiles = T // TOKEN_TILE
    P = T * TOP_K + n_tiles * N_EXPERTS * SLAB + N_EXPERTS * ROW_BLOCK
    rc = rows[:, :, 0]
    lstart = jnp.cumsum(rc, axis=1) - rc
    region = (jnp.sum(rc, axis=0) + ROW_BLOCK - 1) // ROW_BLOCK * ROW_BLOCK
    region_end = jnp.cumsum(region)
    gstart = (region_end - region)[None, :] + jnp.cumsum(rc, axis=0) - rc
    n_blk = P // ROW_BLOCK
    blk_row0 = jnp.arange(n_blk, dtype=jnp.int32) * ROW_BLOCK
    blk_e = jnp.minimum(jnp.sum(region_end[None, :] <= blk_row0[:, None], axis=1), N_EXPERTS - 1).astype(jnp.int32)
    n_used = (region_end[-1:] // ROW_BLOCK).astype(jnp.int32)
    flat = lambda t: t.reshape(-1).astype(jnp.int32)
    used = jnp.sum(rc, axis=0)
    tail_start = flat(jnp.concatenate([region_end - region + used, region_end[-1:]]))
    tail_rows = flat(jnp.concatenate([region - used, P - region_end[-1:]]))
    gstart, lstart, rc = flat(gstart), flat(lstart), flat(rc)
    xp = _dispatch(x, info, gstart, lstart, rc, tail_start, tail_rows, P)
    yp = _expert_ffn(xp, blk_e, n_used, w_gate, w_up, w_down, layer)
    return _combine_ln(x, yp, info, gstart, lstart, rc, consts["eye"], g, b)


def _rope_tables(seq):
    inv_freq = ROPE_THETA ** (-jnp.arange(0, ROT_DIM, 2, dtype=F32) / ROT_DIM)
    ang = jnp.arange(seq, dtype=F32)[:, None] * inv_freq[None, :]
    cos, sin = jnp.cos(ang), jnp.sin(ang)
    half = ROT_DIM // 2
    one = jnp.ones((seq, HEAD_DIM - ROT_DIM), F32)
    zero = jnp.zeros((seq, HEAD_DIM - ROT_DIM), F32)
    zh = jnp.zeros((seq, half), F32)
    cos_h = jnp.concatenate([cos, cos, one], axis=1)
    sa_h = jnp.concatenate([-sin, zh, zero], axis=1)
    sb_h = jnp.concatenate([zh, sin, zero], axis=1)
    rep = lambda t: jnp.concatenate([t] * (LANES // HEAD_DIM), axis=1)
    return rep(cos_h), rep(sa_h), rep(sb_h)


def _constants():
    r = lax.broadcasted_iota(jnp.int32, (CHUNK, CHUNK), 0)
    c = lax.broadcasted_iota(jnp.int32, (CHUNK, CHUNK), 1)
    rr = lax.broadcasted_iota(jnp.int32, (TOKEN_TILE, TOKEN_TILE), 0)
    cc = lax.broadcasted_iota(jnp.int32, (TOKEN_TILE, TOKEN_TILE), 1)
    er = lax.broadcasted_iota(jnp.int32, (N_EXPERTS, N_EXPERTS), 0)
    ec = lax.broadcasted_iota(jnp.int32, (N_EXPERTS, N_EXPERTS), 1)
    return {
        "tri_u": (r <= c).astype(F32),
        "tri_l": (r >= c).astype(F32),
        "tri_strict": (rr < cc).astype(BF16),
        "low_strict": (ec < er).astype(F32),
        "eye": (lax.broadcasted_iota(jnp.int32, (_IDX_ROWS, LANES), 0)
                == lax.broadcasted_iota(jnp.int32, (_IDX_ROWS, LANES), 1)).astype(F32),
    }


def _trunk(x, mem, wts, consts):
    batch, seq, _ = x.shape
    T = batch * seq
    x = x.reshape(T, D_MODEL)
    mem2 = mem.reshape(batch * N_MEM, D_MODEL)
    rope_tabs = _rope_tables(seq)
    for l in range(DEPTH):
        qa, ka, va, mk, mqt, mvt, mot, gp, rn = _in_proj(
            x, wts["w_rows"][l], wts["w_t"][l], wts["wgt"][l], wts["gb"][l], wts["conv_wk"][l], wts["conv_bk"][l],
            wts["conv_q"][l], rope_tabs, consts["tri_u"], consts["tri_l"], seq)
        att = _attention(qa, ka, va, wts["sink"][l], wts["att_g"][l], batch, seq)
        hft, hbt = _mlstm(mqt, mvt, mk, rn, gp, batch, seq)
        x = _out_proj(x, att, hft, hbt, mot, wts["mlstm_g"][l], wts["w_out_a"][l], wts["w_out_m"][l],
                      wts["ln1_g"][l], wts["ln1_b"][l], seq)
        k_mem, v_mem = _kv_proj(mem2, wts["wkv"][l])
        x, info, rows = _xattn(x, k_mem, v_mem, wts["wq"][l], wts["wo"][l], wts["ln2_g"][l], wts["ln2_b"][l],
                               wts["router_wt"], wts["router_b"], consts["tri_strict"], consts["low_strict"],
                               batch, seq)
        x = _moe(x, info, rows, consts, wts["w_gate"], wts["w_up"], wts["w_down"], l,
                 wts["ln3_g"][l], wts["ln3_b"][l])
    return x.reshape(batch, seq, D_MODEL)


def _hi_lo_rows(w):
    hi = w.astype(BF16)
    return jnp.concatenate([hi, (w - hi.astype(F32)).astype(BF16)], axis=0)


def _prepare_weights(w_in, gate_bias, conv_w, conv_b, attn_sink, attn_norm_g, mlstm_norm_g, w_out, ln1_g, ln1_b,
                     wq_mem, wkv_mem, wo_mem, ln2_g, ln2_b, router_w, router_bias, w_gate, w_up, w_down, ln3_g, ln3_b):
    row = lambda t: t.astype(F32).reshape(DEPTH, 1, t.shape[-1])
    nh = N_MLSTM_HEADS
    gate_order = jnp.array(list(range(0, nh)) + list(range(2 * nh, 3 * nh))
                           + list(range(nh, 2 * nh)) + list(range(3 * nh, 4 * nh)), jnp.int32)
    cq = jnp.concatenate([jnp.swapaxes(conv_w[:, :, :MLSTM_WIDTH], 1, 2), conv_b[:, :MLSTM_WIDTH, None],
                          jnp.zeros((DEPTH, MLSTM_WIDTH, 4), conv_w.dtype)], axis=2).astype(F32)
    feature_major = jnp.concatenate([w_in[:, :, OFF_MQ:OFF_MQ + MLSTM_WIDTH], w_in[:, :, OFF_MV:OFF_G]], axis=2)
    return {
        "w_rows": jnp.concatenate([w_in[:, :, :OFF_MQ], w_in[:, :, OFF_MQ + MLSTM_WIDTH:OFF_MV]],
                                  axis=2).astype(BF16),
        "w_t": jnp.swapaxes(feature_major, 1, 2).astype(BF16),
        "wgt": jnp.swapaxes(w_in[:, :, OFF_G:], 1, 2)[:, gate_order].astype(BF16),
        "gb": gate_bias.astype(F32)[:, gate_order].reshape(DEPTH, N_GATE_COLS, 1),
        "conv_wk": conv_w[:, :, MLSTM_WIDTH:].astype(F32),
        "conv_bk": row(conv_b[:, MLSTM_WIDTH:]),
        "conv_q": cq,
        "sink": attn_sink.astype(F32),
        "att_g": row(attn_norm_g),
        "mlstm_g": mlstm_norm_g.astype(F32).reshape(DEPTH, MLSTM_WIDTH, 1),
        "w_out_a": w_out[:, :ATT_WIDTH].astype(BF16),
        "w_out_m": w_out[:, ATT_WIDTH:].astype(BF16),
        "ln1_g": row(ln1_g), "ln1_b": row(ln1_b),
        "wq": wq_mem.astype(BF16), "wkv": wkv_mem.astype(BF16), "wo": wo_mem.astype(BF16),
        "ln2_g": row(ln2_g), "ln2_b": row(ln2_b),
        "router_wt": _hi_lo_rows(router_w.astype(F32).T),
        "router_b": router_bias.astype(F32).reshape(N_EXPERTS, 1),
        "w_gate": w_gate.astype(BF16), "w_up": w_up.astype(BF16), "w_down": w_down.astype(BF16),
        "ln3_g": row(ln3_g), "ln3_b": row(ln3_b),
    }


def kernel(x_prompt, x_sample, mem_prompt, mem_sample, w_in, gate_bias, conv_w, conv_b, attn_sink, attn_norm_g, mlstm_norm_g, w_out, ln1_g, ln1_b, wq_mem, wkv_mem, wo_mem, ln2_g, ln2_b, router_w, router_bias, w_gate, w_up, w_down, ln3_g, ln3_b):
    wts = _prepare_weights(w_in, gate_bias, conv_w, conv_b, attn_sink, attn_norm_g, mlstm_norm_g, w_out, ln1_g, ln1_b,
                           wq_mem, wkv_mem, wo_mem, ln2_g, ln2_b, router_w, router_bias, w_gate, w_up, w_down,
                           ln3_g, ln3_b)
    consts = _constants()
    return (_trunk(x_prompt, mem_prompt, wts, consts), _trunk(x_sample, mem_sample, wts, consts))
```

```python
import functools
import math

import jax
import jax.numpy as jnp
from jax import lax
from jax.experimental import pallas as pl
from jax.experimental.pallas import tpu as pltpu

F32 = jnp.float32
BF16 = jnp.bfloat16

D_MODEL = 1024
DEPTH = 4
HEAD_DIM = 64
N_ATT_HEADS = 8
N_KV_HEADS = 2
ATT_WIDTH = N_ATT_HEADS * HEAD_DIM
KV_WIDTH = N_KV_HEADS * HEAD_DIM
BLOCK = 128
ROT_DIM = HEAD_DIM // 4
ROPE_THETA = 500000.0
MLSTM_WIDTH = D_MODEL - ATT_WIDTH
N_MLSTM_HEADS = 4
MLSTM_HEAD_DIM = MLSTM_WIDTH // N_MLSTM_HEADS
CHUNK = 128
OFF_AQ = 0
OFF_AK = OFF_AQ + ATT_WIDTH
OFF_AV = OFF_AK + KV_WIDTH
OFF_MQ = OFF_AV + KV_WIDTH
OFF_MV = OFF_MQ + 2 * MLSTM_WIDTH
OFF_MO = OFF_MV + MLSTM_WIDTH
OFF_G = OFF_MO + MLSTM_WIDTH
N_GATE_COLS = 4 * N_MLSTM_HEADS
N_MEM = 256
N_X_HEADS = 4
X_HEAD_DIM = D_MODEL // N_X_HEADS
N_EXPERTS = 16
N_GROUPS = 4
EXPERTS_PER_GROUP = N_EXPERTS // N_GROUPS
TOP_K = 2
D_FF_EXPERT = 512
ALPHA = (2.0 * DEPTH) ** 0.25
LN_EPS = 1e-5
RMS_EPS = 1e-6
NEG = -1e30

LANES = 128
TOKEN_TILE = 512
IN_PROJ_TILE = 1024
ROW_BLOCK = 512
VMEM_LIMIT = 56 * 1024 * 1024


def _params(*sem):
    return pltpu.CompilerParams(dimension_semantics=sem, vmem_limit_bytes=VMEM_LIMIT)


def _layer_norm(z, g, b):
    mu = jnp.mean(z, axis=-1, keepdims=True)
    zc = z - mu
    var = jnp.mean(zc * zc, axis=-1, keepdims=True)
    return zc * lax.rsqrt(var + LN_EPS) * g + b


def _dot(a, b):
    return jnp.dot(a, b, preferred_element_type=F32)


def _dot_nt(a, b):
    return lax.dot_general(a, b, (((1,), (1,)), ((), ())), preferred_element_type=F32)


def _dot_tn(a, b):
    return lax.dot_general(a, b, (((0,), (0,)), ((), ())), preferred_element_type=F32)


_X_HALO = 8


_W_MK = OFF_MQ
_W_END = _W_MK + MLSTM_WIDTH
_GP_ROWS = 3 * 2 * N_MLSTM_HEADS


def _in_proj_kernel(x_ref, xp_ref, xn_ref, w_ref, wt_ref, gb_ref, cwk_ref, cbk_ref, cq_ref,
                    cos_ref, sa_ref, sb_ref, tri_u_ref, tri_l_ref,
                    qa_ref, ka_ref, va_ref, mk_ref, mqt_ref, mvt_ref, mot_ref, gp_ref, rn_ref, *, n_seq_tiles):
    xb = x_ref[...].astype(BF16)
    cos = cos_ref[...]
    sa = sa_ref[...]
    sb = sb_ref[...]

    def mm(lo, hi):
        return _dot(xb, w_ref[:, lo:hi])

    def rope(t):
        return t * cos + pltpu.roll(t, LANES - ROT_DIM // 2, 1) * sa + pltpu.roll(t, ROT_DIM // 2, 1) * sb

    q = mm(OFF_AQ, OFF_AK)
    scale = 1.0 / math.sqrt(HEAD_DIM)
    for j in range(ATT_WIDTH // LANES):
        qa_ref[:, j * LANES:(j + 1) * LANES] = (rope(q[:, j * LANES:(j + 1) * LANES]) * scale).astype(BF16)
    kv = mm(OFF_AK, OFF_MQ)
    ka_ref[...] = rope(kv[:, :KV_WIDTH]).astype(BF16)
    va_ref[...] = kv[:, KV_WIDTH:].astype(BF16)
    tm = xb.shape[0]
    pos = pl.program_id(0) % n_seq_tiles
    has_prev = pos > 0
    has_next = pos < n_seq_tiles - 1
    halo = jnp.concatenate([xp_ref[...], xn_ref[...]], axis=0).astype(BF16)

    def silu(y):
        return y * (1.0 / (1.0 + jnp.exp(-y)))

    u = mm(_W_MK, _W_END)
    uh = _dot(halo, w_ref[:, _W_MK:_W_END])
    rowi = lax.broadcasted_iota(jnp.int32, u.shape, 0)
    u_prev = jnp.where(rowi == 0, jnp.where(has_prev, uh[_X_HALO - 1:_X_HALO, :], 0.0), pltpu.roll(u, 1, 0))
    u_next = jnp.where(rowi == tm - 1, jnp.where(has_next, uh[_X_HALO:_X_HALO + 1, :], 0.0),
                       pltpu.roll(u, tm - 1, 0))
    yk = silu(cwk_ref[0:1, :] * u_prev + cwk_ref[1:2, :] * u + cwk_ref[2:3, :] * u_next + cbk_ref[...])
    mk_ref[...] = (yk * (MLSTM_HEAD_DIM ** -0.5)).astype(BF16)

    feat = _dot_nt(wt_ref[...], xb)
    ut = feat[:MLSTM_WIDTH]
    uht = _dot_nt(wt_ref[0:MLSTM_WIDTH, :], halo)
    lanei = lax.broadcasted_iota(jnp.int32, ut.shape, 1)
    ut_prev = jnp.where(lanei == 0, jnp.where(has_prev, uht[:, _X_HALO - 1:_X_HALO], 0.0), pltpu.roll(ut, 1, 1))
    ut_next = jnp.where(lanei == tm - 1, jnp.where(has_next, uht[:, _X_HALO:_X_HALO + 1], 0.0),
                        pltpu.roll(ut, tm - 1, 1))
    cq = cq_ref[...]
    mqt_ref[...] = silu(cq[:, 0:1] * ut_prev + cq[:, 1:2] * ut + cq[:, 2:3] * ut_next + cq[:, 3:4]).astype(BF16)
    mvt_ref[...] = feat[MLSTM_WIDTH:2 * MLSTM_WIDTH].astype(BF16)
    mot_ref[...] = feat[2 * MLSTM_WIDTH:3 * MLSTM_WIDTH].astype(BF16)

    gt = feat[3 * MLSTM_WIDTH:] + gb_ref[...]
    half = 2 * N_MLSTM_HEADS
    gi, gf = gt[:half], gt[half:]
    ls = jnp.minimum(gf, 0.0) - jnp.log1p(jnp.exp(-jnp.abs(gf)))
    n_chunks = tm // CHUNK
    stack = lambda t: jnp.concatenate([t[:, c * CHUNK:(c + 1) * CHUNK] for c in range(n_chunks)], axis=0)
    ls_rows, gi_rows = stack(ls), stack(gi)
    pre = jnp.dot(ls_rows, tri_u_ref[...], precision=lax.Precision.HIGHEST, preferred_element_type=F32)
    suf = jnp.dot(ls_rows, tri_l_ref[...], precision=lax.Precision.HIGHEST, preferred_element_type=F32)
    is_fwd = (lax.broadcasted_iota(jnp.int32, ls_rows.shape, 0) % half) < N_MLSTM_HEADS
    lane = lax.broadcasted_iota(jnp.int32, ls_rows.shape, 1)
    b = jnp.where(is_fwd, pre, suf)
    r = gi_rows - b
    cm_f, cm_b = r, r
    k = 1
    while k < CHUNK:
        cm_f = jnp.maximum(cm_f, jnp.where(lane >= k, pltpu.roll(cm_f, k, 1), -jnp.inf))
        cm_b = jnp.maximum(cm_b, jnp.where(lane < CHUNK - k, pltpu.roll(cm_b, CHUNK - k, 1), -jnp.inf))
        k *= 2
    cm = jnp.where(is_fwd, cm_f, cm_b)
    pad = jnp.zeros((CHUNK - half, CHUNK), F32)
    for c in range(n_chunks):
        rows = slice(c * half, (c + 1) * half)
        gp_ref[c] = jnp.concatenate([b[rows], cm[rows], r[rows]], axis=0)
        rn_ref[c * CHUNK:(c + 1) * CHUNK, :] = jnp.concatenate([r[rows], pad], axis=0).T


def _in_proj(x, w_rows, w_t, gb, cwk, cbk, cq, rope_tabs, tri_u, tri_l, seq):
    T = x.shape[0]
    tm = IN_PROJ_TILE
    n_seq_tiles = seq // tm
    halo_per_tile = tm // _X_HALO
    n_halo = T // _X_HALO
    cos_t, sa_t, sb_t = rope_tabs
    row_spec = lambda w: pl.BlockSpec((tm, w), lambda i: (i, 0))
    full = lambda a: pl.BlockSpec(a.shape, lambda i: (0,) * a.ndim)
    tab_spec = pl.BlockSpec((tm, LANES), lambda i: (i % n_seq_tiles, 0))
    prev_spec = pl.BlockSpec((_X_HALO, D_MODEL), lambda i: (jnp.maximum(i * halo_per_tile - 1, 0), 0))
    next_spec = pl.BlockSpec((_X_HALO, D_MODEL), lambda i: (jnp.minimum((i + 1) * halo_per_tile, n_halo - 1), 0))
    widths = (ATT_WIDTH, KV_WIDTH, KV_WIDTH, MLSTM_WIDTH)
    col_spec = _feature_major_spec(MLSTM_WIDTH, tm, n_seq_tiles)
    feat_major = jax.ShapeDtypeStruct((MLSTM_SEQS, MLSTM_WIDTH, T // MLSTM_SEQS), BF16)
    out_shapes = tuple(jax.ShapeDtypeStruct((T, w), BF16) for w in widths) + (
        feat_major, feat_major, feat_major,
        jax.ShapeDtypeStruct((T // CHUNK, _GP_ROWS, CHUNK), F32), jax.ShapeDtypeStruct((T, LANES), F32))
    out_specs = tuple(row_spec(w) for w in widths) + (
        col_spec, col_spec, col_spec,
        pl.BlockSpec((tm // CHUNK, _GP_ROWS, CHUNK), lambda i: (i, 0, 0)), row_spec(LANES))
    return pl.pallas_call(
        functools.partial(_in_proj_kernel, n_seq_tiles=n_seq_tiles),
        grid=(T // tm,),
        in_specs=[row_spec(D_MODEL), prev_spec, next_spec, full(w_rows), full(w_t), full(gb),
                  full(cwk), full(cbk), full(cq), tab_spec, tab_spec, tab_spec, full(tri_u), full(tri_l)],
        out_specs=out_specs,
        out_shape=out_shapes,
        compiler_params=_params("parallel"),
        name="in_proj",
    )(x, x, x, w_rows, w_t, gb, cwk, cbk, cq, cos_t, sa_t, sb_t, tri_u, tri_l)


ATT_Q_TILE = 512
_Q_BLOCKS = ATT_Q_TILE // BLOCK


def _attn_kernel(sink_ref, q_ref, kp_ref, k_ref, kn_ref, vp_ref, v_ref, vn_ref, g_ref, o_ref, *, n_tiles):
    i = pl.program_id(1)
    lane = lax.broadcasted_iota(jnp.int32, (ATT_Q_TILE + 2 * BLOCK, LANES), 1)
    low = lane < HEAD_DIM

    def split(prev_ref, own_ref, next_ref):
        t = jnp.concatenate([prev_ref[...], own_ref[...], next_ref[...]], axis=0).astype(F32)
        r = pltpu.roll(t, HEAD_DIM, 1)
        zero = jnp.zeros_like(t)
        lo = (jnp.where(low, t, zero).astype(BF16), jnp.where(low, r, zero).astype(BF16))
        hi = (jnp.where(low, zero, r).astype(BF16), jnp.where(low, zero, t).astype(BF16))
        return lo, hi

    k_lo, k_hi = split(kp_ref, k_ref, kn_ref)
    v_lo, v_hi = split(vp_ref, v_ref, vn_ref)

    rowi = lax.broadcasted_iota(jnp.int32, (BLOCK, 3 * BLOCK), 0)
    coli = lax.broadcasted_iota(jnp.int32, (BLOCK, 3 * BLOCK), 1)
    out_of_band = (coli < rowi) | (coli - 2 * BLOCK > rowi)
    band_bias = jnp.where(out_of_band, NEG, 0.0).astype(F32)
    first_bias = jnp.where(i == 0, NEG, 0.0).astype(F32)
    last_bias = jnp.where(i == n_tiles - 1, NEG, 0.0).astype(F32)
    bias_first = band_bias + jnp.where(coli < BLOCK, first_bias, 0.0)
    bias_last = band_bias + jnp.where(coli >= 2 * BLOCK, last_bias, 0.0)
    lane_o = lax.broadcasted_iota(jnp.int32, (BLOCK, LANES), 1)
    g = g_ref[...]

    for r in range(_Q_BLOCKS):
        bias = bias_first if r == 0 else (bias_last if r == _Q_BLOCKS - 1 else band_bias)
        rows = slice(r * BLOCK, (r + 1) * BLOCK)
        win = slice(r * BLOCK, (r + 3) * BLOCK)
        tiles = []
        for c in range(N_KV_HEADS):
            q2 = jnp.concatenate([q_ref[rows, (2 * c) * LANES:(2 * c + 1) * LANES],
                                  q_ref[rows, (2 * c + 1) * LANES:(2 * c + 2) * LANES]], axis=0)
            kc = jnp.concatenate([k_lo[c][win], k_hi[c][win]], axis=0)
            vc = jnp.concatenate([v_lo[c][win], v_hi[c][win]], axis=0)
            s = _dot_nt(q2, kc)
            p_rows, inv_rows = [], []
            for t in range(2):
                ps, invs = [], []
                for hh in range(2):
                    head = 4 * c + 2 * t + hh
                    sink = sink_ref[head]
                    sh = s[t * BLOCK:(t + 1) * BLOCK, hh * 3 * BLOCK:(hh + 1) * 3 * BLOCK] + bias
                    m = jnp.maximum(jnp.max(sh, axis=-1, keepdims=True), sink)
                    p = jnp.exp(sh - m)
                    den = jnp.sum(p, axis=-1, keepdims=True) + jnp.exp(sink - m)
                    ps.append(p.astype(BF16))
                    invs.append(1.0 / den)
                p_rows.append(jnp.concatenate(ps, axis=1))
                inv_rows.append(jnp.where(lane_o < HEAD_DIM, invs[0], invs[1]))
            o2 = _dot(jnp.concatenate(p_rows, axis=0), vc)
            tiles.append(o2[:BLOCK] * inv_rows[0])
            tiles.append(o2[BLOCK:] * inv_rows[1])
        o = jnp.concatenate(tiles, axis=1)
        ms = jnp.mean(o * o, axis=-1, keepdims=True)
        o_ref[rows, :] = (o * lax.rsqrt(ms + RMS_EPS) * g).astype(BF16)


def _attention(qa, ka, va, sink, att_g, batch, seq):
    T = qa.shape[0]
    n_tiles = seq // ATT_Q_TILE
    blocks_per_seq = seq // BLOCK
    own = lambda w: pl.BlockSpec((ATT_Q_TILE, w), lambda b, i: (b * n_tiles + i, 0))
    prev = pl.BlockSpec((BLOCK, KV_WIDTH),
                        lambda b, i: (b * blocks_per_seq + jnp.maximum(i * _Q_BLOCKS - 1, 0), 0))
    nxt = pl.BlockSpec((BLOCK, KV_WIDTH),
                       lambda b, i: (b * blocks_per_seq + jnp.minimum((i + 1) * _Q_BLOCKS, blocks_per_seq - 1), 0))
    return pl.pallas_call(
        functools.partial(_attn_kernel, n_tiles=n_tiles),
        grid=(batch, n_tiles),
        in_specs=[pl.BlockSpec(memory_space=pltpu.SMEM), own(ATT_WIDTH), prev, own(KV_WIDTH), nxt,
                  prev, own(KV_WIDTH), nxt, pl.BlockSpec((1, ATT_WIDTH), lambda b, i: (0, 0))],
        out_specs=own(ATT_WIDTH),
        out_shape=jax.ShapeDtypeStruct((T, ATT_WIDTH), BF16),
        compiler_params=_params("parallel", "parallel"),
        name="band_attention",
    )(sink, qa, ka, ka, ka, va, va, va, att_g)


MLSTM_SEQS = 2
_STATE_ROWS = MLSTM_HEAD_DIM + 8


def _mlstm_kernel(qt_f, vt_f, k_f, rn_f, gp_f, qt_b, vt_b, k_b, rn_b, gp_b, of_ref, ob_ref, c_state, m_state):
    @pl.when(pl.program_id(1) == 0)
    def _():
        c_state[...] = jnp.zeros_like(c_state)
        m_state[...] = jnp.zeros_like(m_state)

    key = lax.broadcasted_iota(jnp.int32, (CHUNK, CHUNK), 0)
    qry = lax.broadcasted_iota(jnp.int32, (CHUNK, CHUNK), 1)
    nh = N_MLSTM_HEADS

    units = []
    for sq in range(MLSTM_SEQS):
        for fwd, qt_ref, vt_ref, k_ref, rn_ref, gp_ref, o_ref in ((True, qt_f, vt_f, k_f, rn_f, gp_f, of_ref),
                                                                 (False, qt_b, vt_b, k_b, rn_b, gp_b, ob_ref)):
            gates = gp_ref[sq, 0]
            off = 0 if fwd else nh
            a_pos = CHUNK - 1 if fwd else 0
            for h in range(nh):
                hs = slice(h * MLSTM_HEAD_DIM, (h + 1) * MLSTM_HEAD_DIM)
                b = gates[off + h:off + h + 1, :]
                st = (2 * sq + (0 if fwd else 1)) * nh + h
                units.append(dict(
                    sq=sq, hs=hs, st=st, o_ref=o_ref, qt_ref=qt_ref, vt_ref=vt_ref, k_ref=k_ref,
                    visible=(key <= qry) if fwd else (key >= qry),
                    b=b, cm=gates[2 * nh + off + h:2 * nh + off + h + 1, :],
                    r_row=gates[4 * nh + off + h:4 * nh + off + h + 1, :],
                    r_keys=jnp.broadcast_to(rn_ref[sq, :, off + h:off + h + 1], (CHUNK, CHUNK)),
                    a=b[:, a_pos:a_pos + 1],
                    m_in=m_state[st:st + 1, 0:1]))

    for u in units:
        inter_log = u["b"] + u["m_in"]
        m_t = jnp.maximum(inter_log, u["b"] + u["cm"])
        u["decay"] = jnp.exp(jnp.where(u["visible"], u["r_keys"] + (u["b"] - m_t), NEG))
        u["inter_w"] = jnp.exp(inter_log - m_t)
        u["floor"] = jnp.exp(-m_t)
    for u in units:
        qt = u["qt_ref"][u["sq"], u["hs"], :]
        u["k"] = u["k_ref"][u["sq"], :, u["hs"]]
        u["vt"] = u["vt_ref"][u["sq"], u["hs"], :]
        u["scores"] = _dot(u["k"], qt)
        u["c_in"] = c_state[u["st"]]
        u["inter"] = _dot(u["c_in"].astype(BF16), qt)
    for u in units:
        sw = u["scores"] * u["decay"]
        num = _dot(u["vt"], sw.astype(BF16)) + u["inter_w"] * u["inter"][:MLSTM_HEAD_DIM]
        den = (jnp.sum(sw, axis=0, keepdims=True)
               + u["inter_w"] * u["inter"][MLSTM_HEAD_DIM:MLSTM_HEAD_DIM + 1])
        u["o_ref"][u["sq"], u["hs"], :] = (num / jnp.maximum(jnp.abs(den), u["floor"])).astype(u["o_ref"].dtype)
    for u in units:
        a, m_in, st = u["a"], u["m_in"], u["st"]
        g_max = a + jnp.max(u["r_row"], axis=-1, keepdims=True)
        kw = u["k"].astype(F32) * jnp.exp(u["r_keys"] + (a - g_max))
        m_new = jnp.maximum(a + m_in, g_max)
        keep = jnp.exp(a + m_in - m_new)
        add = jnp.exp(g_max - m_new)
        c_state[st, 0:MLSTM_HEAD_DIM, :] = keep * u["c_in"][:MLSTM_HEAD_DIM] + add * _dot(u["vt"], kw.astype(BF16))
        c_state[st, MLSTM_HEAD_DIM:MLSTM_HEAD_DIM + 1, :] = (
            keep * u["c_in"][MLSTM_HEAD_DIM:MLSTM_HEAD_DIM + 1] + add * jnp.sum(kw, axis=0, keepdims=True))
        m_state[st:st + 1, :] = jnp.broadcast_to(m_new, (1, LANES))


def _feature_major_spec(width, tm, tiles_per_seq):
    def index(i):
        s = i // tiles_per_seq
        return (s % MLSTM_SEQS, 0, (s // MLSTM_SEQS) * tiles_per_seq + i % tiles_per_seq)
    return pl.BlockSpec((None, width, tm), index)


def _mlstm(mqt, mvt, mk, rn, gp, batch, seq):
    T = mk.shape[0]
    nc = seq // CHUNK
    groups = batch // MLSTM_SEQS
    mk = mk.reshape(groups, MLSTM_SEQS, seq, MLSTM_WIDTH)
    rn = rn.reshape(groups, MLSTM_SEQS, seq, LANES)
    gp = gp.reshape(groups, MLSTM_SEQS, nc, _GP_ROWS, CHUNK)

    def specs(chunk_of):
        feat = pl.BlockSpec((MLSTM_SEQS, MLSTM_WIDTH, CHUNK), lambda b, i: (0, 0, b * nc + chunk_of(i)))
        return [feat, feat,
                pl.BlockSpec((None, MLSTM_SEQS, CHUNK, MLSTM_WIDTH), lambda b, i: (b, 0, chunk_of(i), 0)),
                pl.BlockSpec((None, MLSTM_SEQS, CHUNK, LANES), lambda b, i: (b, 0, chunk_of(i), 0)),
                pl.BlockSpec((None, MLSTM_SEQS, 1, _GP_ROWS, CHUNK), lambda b, i: (b, 0, chunk_of(i), 0, 0))]

    fwd_chunk = lambda i: i
    bwd_chunk = lambda i: nc - 1 - i
    out = jax.ShapeDtypeStruct((MLSTM_SEQS, MLSTM_WIDTH, T // MLSTM_SEQS), BF16)
    n_state = 2 * MLSTM_SEQS * N_MLSTM_HEADS
    return pl.pallas_call(
        _mlstm_kernel,
        grid=(groups, nc),
        in_specs=specs(fwd_chunk) + specs(bwd_chunk),
        out_specs=(specs(fwd_chunk)[0], specs(bwd_chunk)[0]),
        out_shape=(out, out),
        scratch_shapes=[pltpu.VMEM((n_state, _STATE_ROWS, MLSTM_HEAD_DIM), F32),
                        pltpu.VMEM((n_state, LANES), F32)],
        compiler_params=_params("parallel", "arbitrary"),
        name="mlstm",
    )(mqt, mvt, mk, rn, gp, mqt, mvt, mk, rn, gp)


def _mixer_out(x, att_ref, hf_ref, hb_ref, mo_ref, mg_ref, wa_ref, wm_ref, g_ref, b_ref):
    h = hf_ref[...].astype(F32) + hb_ref[...].astype(F32)
    parts = []
    for hd in range(N_MLSTM_HEADS):
        hh = h[hd * MLSTM_HEAD_DIM:(hd + 1) * MLSTM_HEAD_DIM]
        ms = jnp.mean(hh * hh, axis=0, keepdims=True)
        parts.append(hh * lax.rsqrt(ms + RMS_EPS))
    hn = jnp.concatenate(parts, axis=0) * mg_ref[...]
    gate = 1.0 / (1.0 + jnp.exp(-mo_ref[...].astype(F32)))
    y = _dot(att_ref[...], wa_ref[...]) + _dot_tn((hn * gate).astype(BF16), wm_ref[...])
    return _layer_norm(ALPHA * x + y, g_ref[...], b_ref[...])


def _kv_proj_kernel(m_ref, w_ref, k_ref, v_ref):
    mb = m_ref[...].astype(BF16)
    k_ref[...] = _dot(mb, w_ref[:, :D_MODEL]).astype(BF16)
    v_ref[...] = _dot(mb, w_ref[:, D_MODEL:]).astype(BF16)


def _kv_proj(mem, wkv):
    M = mem.shape[0]
    tm = TOKEN_TILE
    row = pl.BlockSpec((tm, D_MODEL), lambda i: (i, 0))
    out = jax.ShapeDtypeStruct((M, D_MODEL), BF16)
    return pl.pallas_call(
        _kv_proj_kernel,
        grid=(M // tm,),
        in_specs=[row, pl.BlockSpec(wkv.shape, lambda i: (0, 0))],
        out_specs=(row, row),
        out_shape=(out, out),
        compiler_params=_params("parallel"),
        name="mem_kv_proj",
    )(mem, wkv)


def _xattn_kernel(x_ref, att_ref, hf_ref, hb_ref, mo_ref, mg_ref, wa_ref, wm_ref, g1_ref, b1_ref,
                  k_ref, v_ref, wq_ref, wo_ref, g_ref, b_ref, rwt_ref, rb_ref, tri_ref, low_ref,
                  o_ref, info_ref, cnt_ref):
    x = _mixer_out(x_ref[...], att_ref, hf_ref, hb_ref, mo_ref, mg_ref, wa_ref, wm_ref, g1_ref, b1_ref)
    q = _dot(x.astype(BF16), wq_ref[...])
    scale = 1.0 / math.sqrt(X_HEAD_DIM)
    outs = []
    for h in range(N_X_HEADS):
        hs = slice(h * X_HEAD_DIM, (h + 1) * X_HEAD_DIM)
        s = _dot_nt((q[:, hs] * scale).astype(BF16), k_ref[:, hs])
        m = jnp.max(s, axis=-1, keepdims=True)
        p = jnp.exp(s - m)
        inv = 1.0 / jnp.sum(p, axis=-1, keepdims=True)
        outs.append((_dot(p.astype(BF16), v_ref[:, hs]) * inv).astype(BF16))
    y = _dot(jnp.concatenate(outs, axis=1), wo_ref[...])
    x2 = _layer_norm(ALPHA * x + y, g_ref[...], b_ref[...])
    o_ref[...] = x2
    _route(x2, rwt_ref, rb_ref, tri_ref, low_ref, info_ref, cnt_ref)


def _mixer_out_xattn(x, att, hft, hbt, mot, mg, wa, wm, g1, b1, k_mem, v_mem, wq, wo, g, b,
                     router_wt, router_b, tri_strict, low_strict, batch, seq):
    T = x.shape[0]
    tm = TOKEN_TILE
    nt = seq // tm
    row = pl.BlockSpec((tm, D_MODEL), lambda bb, i: (bb * nt + i, 0))
    att_row = pl.BlockSpec((tm, ATT_WIDTH), lambda bb, i: (bb * nt + i, 0))
    flat_feat = _feature_major_spec(MLSTM_WIDTH, tm, nt)
    feat = pl.BlockSpec(flat_feat.block_shape, lambda bb, i: flat_feat.index_map(bb * nt + i))
    mem = pl.BlockSpec((N_MEM, D_MODEL), lambda bb, i: (bb, 0))
    full = lambda a: pl.BlockSpec(a.shape, lambda bb, i: (0,) * a.ndim)
    return pl.pallas_call(
        _xattn_kernel,
        grid=(batch, nt),
        in_specs=[row, att_row, feat, feat, feat, full(mg), full(wa), full(wm), full(g1), full(b1),
                  mem, mem, full(wq), full(wo), full(g), full(b), full(router_wt), full(router_b),
                  full(tri_strict), full(low_strict)],
        out_specs=(row, pl.BlockSpec((_IDX_ROWS, tm), lambda bb, i: (0, bb * nt + i)),
                   pl.BlockSpec((1, N_EXPERTS, LANES), lambda bb, i: (bb * nt + i, 0, 0))),
        out_shape=(jax.ShapeDtypeStruct((T, D_MODEL), F32), jax.ShapeDtypeStruct((_IDX_ROWS, T), F32),
                   jax.ShapeDtypeStruct((T // tm, N_EXPERTS, LANES), jnp.int32)),
        compiler_params=_params("parallel", "parallel"),
        name="mixer_out_xattn_route",
    )(x, att, hft, hbt, mot, mg, wa, wm, g1, b1, k_mem, v_mem, wq, wo, g, b, router_wt, router_b,
      tri_strict, low_strict)


_IDX_ROWS = 8
SLAB = 16
TILE_SLOTS = TOP_K * TOKEN_TILE + N_EXPERTS * SLAB


def _route(x, wt_ref, b_ref, tri_ref, low_ref, info_ref, cnt_ref):
    x_hi = x.astype(BF16)
    x_lo = (x - x_hi.astype(F32)).astype(BF16)
    by_hi = _dot_nt(wt_ref[...], x_hi)
    logits = by_hi[:N_EXPERTS] + by_hi[N_EXPERTS:] + _dot_nt(wt_ref[0:N_EXPERTS, :], x_lo)
    s = 1.0 / (1.0 + jnp.exp(-logits))
    sel = s + b_ref[...]
    srow = lambda e: s[e:e + 1, :]
    brow = lambda e: sel[e:e + 1, :]
    best = None
    gi = None
    for gidx in range(N_GROUPS):
        vals = [brow(gidx * EXPERTS_PER_GROUP + j) for j in range(EXPERTS_PER_GROUP)]
        top2 = None
        for a in range(EXPERTS_PER_GROUP):
            for b in range(a + 1, EXPERTS_PER_GROUP):
                pair = vals[a] + vals[b]
                top2 = pair if top2 is None else jnp.maximum(top2, pair)
        if best is None:
            best, gi = top2, jnp.zeros(top2.shape, jnp.int32)
        else:
            better = top2 > best
            gi = jnp.where(better, gidx, gi)
            best = jnp.where(better, top2, best)

    def in_group(rowfn, j):
        out = rowfn(j)
        for gidx in range(1, N_GROUPS):
            out = jnp.where(gi == gidx, rowfn(gidx * EXPERTS_PER_GROUP + j), out)
        return out

    bv = [in_group(brow, j) for j in range(EXPERTS_PER_GROUP)]
    sv = [in_group(srow, j) for j in range(EXPERTS_PER_GROUP)]

    def argmax_first(vals):
        bi = jnp.zeros(vals[0].shape, jnp.int32)
        bm = vals[0]
        for j in range(1, len(vals)):
            better = vals[j] > bm
            bi = jnp.where(better, j, bi)
            bm = jnp.where(better, vals[j], bm)
        return bi

    i1 = argmax_first(bv)
    i2 = argmax_first([jnp.where(i1 == j, -jnp.inf, bv[j]) for j in range(EXPERTS_PER_GROUP)])

    def pick(vals, idx):
        out = vals[0]
        for j in range(1, len(vals)):
            out = jnp.where(idx == j, vals[j], out)
        return out

    w1 = pick(sv, i1)
    w2 = pick(sv, i2)
    tot = w1 + w2
    tm = logits.shape[1]
    eid = lax.broadcasted_iota(jnp.int32, (N_EXPERTS, tm), 0)
    oh1 = (eid == gi * EXPERTS_PER_GROUP + i1).astype(F32)
    oh2 = (eid == gi * EXPERTS_PER_GROUP + i2).astype(F32)
    before1 = _dot(oh1.astype(BF16), tri_ref[...])
    before2 = _dot(oh2.astype(BF16), tri_ref[...])
    c1 = jnp.sum(oh1, axis=1, keepdims=True)
    cnt = c1 + jnp.sum(oh2, axis=1, keepdims=True)
    rows = jnp.floor((cnt + (SLAB - 1)) * (1.0 / SLAB)) * SLAB
    rows_b = jnp.broadcast_to(rows, (N_EXPERTS, LANES))
    start = jnp.dot(low_ref[...], rows_b, precision=lax.Precision.HIGHEST, preferred_element_type=F32)[:, 0:1]
    slot1 = jnp.sum(oh1 * (start + before1), axis=0, keepdims=True)
    slot2 = jnp.sum(oh2 * (start + c1 + before2), axis=0, keepdims=True)
    zf = jnp.zeros((_IDX_ROWS - 2 * TOP_K, tm), F32)
    info_ref[...] = jnp.concatenate([w1 / tot, w2 / tot, slot1, slot2, zf], axis=0)
    cnt_ref[0] = rows_b.astype(jnp.int32)


def _slab_copies(i, gs_ref, ls_ref, rc_ref, make_copy, op):
    for e in range(N_EXPERTS):
        idx = i * N_EXPERTS + e
        ls = ls_ref[idx]
        gs = gs_ref[idx]

        def body(j, carry, ls=ls, gs=gs):
            op(make_copy(pl.multiple_of(ls + j * SLAB, SLAB), pl.multiple_of(gs + j * SLAB, SLAB)))
            return carry

        lax.fori_loop(0, rc_ref[idx] // SLAB, body, 0)


_DISPATCH_CHUNK = 256
_ZERO_ROWS = 128


def _dispatch_kernel(gs_ref, ls_ref, rc_ref, ts_ref, tn_ref, x_ref, info_ref, xp_ref, xs_ref, zero_ref, sem, zsem):
    i = pl.program_id(0)
    last = pl.num_programs(0) - 1
    buf = i % 2
    slot0 = info_ref[2:3, :].astype(jnp.int32)
    slot1 = info_ref[3:4, :].astype(jnp.int32)
    xb = x_ref[...].astype(BF16)
    tm = xb.shape[0]
    for c in range(TILE_SLOTS // _DISPATCH_CHUNK):
        sid = lax.broadcasted_iota(jnp.int32, (_DISPATCH_CHUNK, tm), 0) + c * _DISPATCH_CHUNK
        sel = jnp.where((sid == slot0) | (sid == slot1), 1.0, 0.0).astype(BF16)
        xs_ref[buf, c * _DISPATCH_CHUNK:(c + 1) * _DISPATCH_CHUNK, :] = _dot(sel, xb).astype(BF16)

    def copies(tile, which, op):
        def make_copy(tile_row, global_row):
            return pltpu.make_async_copy(xs_ref.at[which, pl.ds(tile_row, SLAB), :],
                                         xp_ref.at[pl.ds(global_row, SLAB), :], sem.at[which])
        _slab_copies(tile, gs_ref, ls_ref, rc_ref, make_copy, op)

    @pl.when(i > 0)
    def _():
        copies(i - 1, 1 - buf, lambda cp: cp.wait())

    copies(i, buf, lambda cp: cp.start())

    @pl.when(i == last)
    def _():
        zero_ref[...] = jnp.zeros_like(zero_ref)

        def tail(op):
            for e in range(N_EXPERTS + 1):
                t0 = ts_ref[e]
                rows = SLAB if e < N_EXPERTS else _ZERO_ROWS

                def body(j, carry, t0=t0, rows=rows):
                    op(pltpu.make_async_copy(zero_ref.at[pl.ds(0, rows), :],
                                             xp_ref.at[pl.ds(pl.multiple_of(t0 + j * rows, rows), rows), :], zsem))
                    return carry

                lax.fori_loop(0, tn_ref[e] // rows, body, 0)

        tail(lambda cp: cp.start())
        copies(i, buf, lambda cp: cp.wait())
        tail(lambda cp: cp.wait())


def _dispatch(x, info, gstart, lstart, rc, tail_start, tail_rows, n_rows):
    T = x.shape[0]
    tm = TOKEN_TILE
    grid_spec = pltpu.PrefetchScalarGridSpec(
        num_scalar_prefetch=5,
        grid=(T // tm,),
        in_specs=[pl.BlockSpec((tm, D_MODEL), lambda i, *_: (i, 0)),
                  pl.BlockSpec((_IDX_ROWS, tm), lambda i, *_: (0, i))],
        out_specs=pl.BlockSpec(memory_space=pl.ANY),
        scratch_shapes=[pltpu.VMEM((2, TILE_SLOTS, D_MODEL), BF16), pltpu.VMEM((_ZERO_ROWS, D_MODEL), BF16),
                        pltpu.SemaphoreType.DMA((2,)), pltpu.SemaphoreType.DMA],
    )
    return pl.pallas_call(
        _dispatch_kernel,
        grid_spec=grid_spec,
        out_shape=jax.ShapeDtypeStruct((n_rows, D_MODEL), BF16),
        compiler_params=_params("arbitrary"),
        name="moe_dispatch",
    )(gstart, lstart, rc, tail_start, tail_rows, x, info)


def _ffn_kernel(be_ref, nu_ref, x_ref, wg_ref, wu_ref, wd_ref, o_ref):
    i = pl.program_id(0)

    @pl.when(i < nu_ref[0])
    def _():
        xb = x_ref[...]
        gate = _dot(xb, wg_ref[...])
        up = _dot(xb, wu_ref[...])
        h = gate * (1.0 / (1.0 + jnp.exp(-gate))) * up
        o_ref[...] = _dot(h.astype(BF16), wd_ref[...]).astype(BF16)

    @pl.when(i >= nu_ref[0])
    def _():
        o_ref[...] = jnp.zeros_like(o_ref)


def _expert_ffn(xp, blk_e, n_used, w_gate, w_up, w_down, layer):
    P = xp.shape[0]
    wspec = lambda shp: pl.BlockSpec((None, None) + shp, lambda i, be, nu: (layer, be[i], 0, 0))
    grid_spec = pltpu.PrefetchScalarGridSpec(
        num_scalar_prefetch=2,
        grid=(P // ROW_BLOCK,),
        in_specs=[pl.BlockSpec((ROW_BLOCK, D_MODEL), lambda i, be, nu: (jnp.minimum(i, nu[0] - 1), 0)),
                  wspec((D_MODEL, D_FF_EXPERT)), wspec((D_MODEL, D_FF_EXPERT)), wspec((D_FF_EXPERT, D_MODEL))],
        out_specs=pl.BlockSpec((ROW_BLOCK, D_MODEL), lambda i, be, nu: (i, 0)),
    )
    return pl.pallas_call(
        _ffn_kernel,
        grid_spec=grid_spec,
        out_shape=jax.ShapeDtypeStruct((P, D_MODEL), BF16),
        compiler_params=_params("arbitrary"),
        name="expert_ffn",
    )(blk_e, n_used, xp, w_gate, w_up, w_down)


def _combine_kernel(gs_ref, ls_ref, rc_ref, x_ref, info_ref, eye_ref, g_ref, b_ref, yp_ref, o_ref, ys_ref, sem):
    i = pl.program_id(0)
    buf = i % 2

    def copies(tile, which, op):
        def make_copy(tile_row, global_row):
            return pltpu.make_async_copy(yp_ref.at[pl.ds(global_row, SLAB), :],
                                         ys_ref.at[which, pl.ds(tile_row, SLAB), :], sem.at[which])
        _slab_copies(tile, gs_ref, ls_ref, rc_ref, make_copy, op)

    @pl.when(i == 0)
    def _():
        ys_ref[...] = jnp.zeros_like(ys_ref)
        copies(i, buf, lambda cp: cp.start())

    @pl.when(i + 1 < pl.num_programs(0))
    def _():
        copies(i + 1, 1 - buf, lambda cp: cp.start())

    cols = lax.dot_general(info_ref[...], eye_ref[...], (((0,), (0,)), ((), ())),
                           precision=lax.Precision.HIGHEST, preferred_element_type=F32)
    tm = cols.shape[0]
    sid = lax.broadcasted_iota(jnp.int32, (tm, TILE_SLOTS), 1)
    weights = (jnp.where(sid == cols[:, 2:3].astype(jnp.int32), cols[:, 0:1], 0.0)
               + jnp.where(sid == cols[:, 3:4].astype(jnp.int32), cols[:, 1:2], 0.0)).astype(BF16)
    copies(i, buf, lambda cp: cp.wait())
    y = _dot(weights, ys_ref[buf])
    o_ref[...] = _layer_norm(ALPHA * x_ref[...] + y, g_ref[...], b_ref[...])


def _combine_ln(x, yp, info, gstart, lstart, rc, eye, g, b):
    T = x.shape[0]
    tm = TOKEN_TILE
    full = lambda a: pl.BlockSpec(a.shape, lambda i, *_: (0,) * a.ndim)
    grid_spec = pltpu.PrefetchScalarGridSpec(
        num_scalar_prefetch=3,
        grid=(T // tm,),
        in_specs=[pl.BlockSpec((tm, D_MODEL), lambda i, *_: (i, 0)),
                  pl.BlockSpec((_IDX_ROWS, tm), lambda i, *_: (0, i)), full(eye), full(g), full(b),
                  pl.BlockSpec(memory_space=pl.ANY)],
        out_specs=pl.BlockSpec((tm, D_MODEL), lambda i, *_: (i, 0)),
        scratch_shapes=[pltpu.VMEM((2, TILE_SLOTS, D_MODEL), BF16), pltpu.SemaphoreType.DMA((2,))],
    )
    return pl.pallas_call(
        _combine_kernel,
        grid_spec=grid_spec,
        out_shape=jax.ShapeDtypeStruct((T, D_MODEL), F32),
        compiler_params=_params("arbitrary"),
        name="moe_combine_ln3",
    )(gstart, lstart, rc, x, info, eye, g, b, yp)


def _moe(x, info, rows, consts, w_gate, w_up, w_down, layer, g, b):
    T = x.shape[0]
    n_tiles = T // TOKEN_TILE
    P = T * TOP_K + n_tiles * N_EXPERTS * SLAB + N_EXPERTS * ROW_BLOCK
    rc = rows[:, :, 0]
    lstart = jnp.cumsum(rc, axis=1) - rc
    region = (jnp.sum(rc, axis=0) + ROW_BLOCK - 1) // ROW_BLOCK * ROW_BLOCK
    region_end = jnp.cumsum(region)
    gstart = (region_end - region)[None, :] + jnp.cumsum(rc, axis=0) - rc
    n_blk = P // ROW_BLOCK
    blk_row0 = jnp.arange(n_blk, dtype=jnp.int32) * ROW_BLOCK
    blk_e = jnp.minimum(jnp.sum(region_end[None, :] <= blk_row0[:, None], axis=1), N_EXPERTS - 1).astype(jnp.int32)
    n_used = (region_end[-1:] // ROW_BLOCK).astype(jnp.int32)
    flat = lambda t: t.reshape(-1).astype(jnp.int32)
    used = jnp.sum(rc, axis=0)
    tail_start = flat(jnp.concatenate([region_end - region + used, region_end[-1:]]))
    tail_rows = flat(jnp.concatenate([region - used, P - region_end[-1:]]))
    gstart, lstart, rc = flat(gstart), flat(lstart), flat(rc)
    xp = _dispatch(x, info, gstart, lstart, rc, tail_start, tail_rows, P)
    yp = _expert_ffn(xp, blk_e, n_used, w_gate, w_up, w_down, layer)
    return _combine_ln(x, yp, info, gstart, lstart, rc, consts["eye"], g, b)


def _rope_tables(seq):
    inv_freq = ROPE_THETA ** (-jnp.arange(0, ROT_DIM, 2, dtype=F32) / ROT_DIM)
    ang = jnp.arange(seq, dtype=F32)[:, None] * inv_freq[None, :]
    cos, sin = jnp.cos(ang), jnp.sin(ang)
    half = ROT_DIM // 2
    one = jnp.ones((seq, HEAD_DIM - ROT_DIM), F32)
    zero = jnp.zeros((seq, HEAD_DIM - ROT_DIM), F32)
    zh = jnp.zeros((seq, half), F32)
    cos_h = jnp.concatenate([cos, cos, one], axis=1)
    sa_h = jnp.concatenate([-sin, zh, zero], axis=1)
    sb_h = jnp.concatenate([zh, sin, zero], axis=1)
    rep = lambda t: jnp.concatenate([t] * (LANES // HEAD_DIM), axis=1)
    return rep(cos_h), rep(sa_h), rep(sb_h)


def _constants():
    r = lax.broadcasted_iota(jnp.int32, (CHUNK, CHUNK), 0)
    c = lax.broadcasted_iota(jnp.int32, (CHUNK, CHUNK), 1)
    rr = lax.broadcasted_iota(jnp.int32, (TOKEN_TILE, TOKEN_TILE), 0)
    cc = lax.broadcasted_iota(jnp.int32, (TOKEN_TILE, TOKEN_TILE), 1)
    er = lax.broadcasted_iota(jnp.int32, (N_EXPERTS, N_EXPERTS), 0)
    ec = lax.broadcasted_iota(jnp.int32, (N_EXPERTS, N_EXPERTS), 1)
    return {
        "tri_u": (r <= c).astype(F32),
        "tri_l": (r >= c).astype(F32),
        "tri_strict": (rr < cc).astype(BF16),
        "low_strict": (ec < er).astype(F32),
        "eye": (lax.broadcasted_iota(jnp.int32, (_IDX_ROWS, LANES), 0)
                == lax.broadcasted_iota(jnp.int32, (_IDX_ROWS, LANES), 1)).astype(F32),
    }


def _trunk(x, mem, wts, consts):
    batch, seq, _ = x.shape
    T = batch * seq
    x = x.reshape(T, D_MODEL)
    mem2 = mem.reshape(batch * N_MEM, D_MODEL)
    rope_tabs = _rope_tables(seq)
    for l in range(DEPTH):
        qa, ka, va, mk, mqt, mvt, mot, gp, rn = _in_proj(
            x, wts["w_rows"][l], wts["w_t"][l], wts["gb"][l], wts["conv_wk"][l], wts["conv_bk"][l],
            wts["conv_q"][l], rope_tabs, consts["tri_u"], consts["tri_l"], seq)
        att = _attention(qa, ka, va, wts["sink"][l], wts["att_g"][l], batch, seq)
        hft, hbt = _mlstm(mqt, mvt, mk, rn, gp, batch, seq)
        k_mem, v_mem = _kv_proj(mem2, wts["wkv"][l])
        x, info, rows = _mixer_out_xattn(
            x, att, hft, hbt, mot, wts["mlstm_g"][l], wts["w_out_a"][l], wts["w_out_m"][l], wts["ln1_g"][l],
            wts["ln1_b"][l], k_mem, v_mem, wts["wq"][l], wts["wo"][l], wts["ln2_g"][l], wts["ln2_b"][l],
            wts["router_wt"], wts["router_b"], consts["tri_strict"], consts["low_strict"], batch, seq)
        x = _moe(x, info, rows, consts, wts["w_gate"], wts["w_up"], wts["w_down"], l,
                 wts["ln3_g"][l], wts["ln3_b"][l])
    return x.reshape(batch, seq, D_MODEL)


def _hi_lo_rows(w):
    hi = w.astype(BF16)
    return jnp.concatenate([hi, (w - hi.astype(F32)).astype(BF16)], axis=0)


def _prepare_weights(w_in, gate_bias, conv_w, conv_b, attn_sink, attn_norm_g, mlstm_norm_g, w_out, ln1_g, ln1_b,
                     wq_mem, wkv_mem, wo_mem, ln2_g, ln2_b, router_w, router_bias, w_gate, w_up, w_down, ln3_g, ln3_b):
    row = lambda t: t.astype(F32).reshape(DEPTH, 1, t.shape[-1])
    nh = N_MLSTM_HEADS
    gate_order = jnp.array(list(range(0, nh)) + list(range(2 * nh, 3 * nh))
                           + list(range(nh, 2 * nh)) + list(range(3 * nh, 4 * nh)), jnp.int32)
    cq = jnp.concatenate([jnp.swapaxes(conv_w[:, :, :MLSTM_WIDTH], 1, 2), conv_b[:, :MLSTM_WIDTH, None],
                          jnp.zeros((DEPTH, MLSTM_WIDTH, 4), conv_w.dtype)], axis=2).astype(F32)
    feature_major = jnp.concatenate([w_in[:, :, OFF_MQ:OFF_MQ + MLSTM_WIDTH], w_in[:, :, OFF_MV:OFF_G],
                                     w_in[:, :, OFF_G:][:, :, gate_order]], axis=2)
    return {
        "w_rows": jnp.concatenate([w_in[:, :, :OFF_MQ], w_in[:, :, OFF_MQ + MLSTM_WIDTH:OFF_MV]],
                                  axis=2).astype(BF16),
        "w_t": jnp.swapaxes(feature_major, 1, 2).astype(BF16),
        "gb": gate_bias.astype(F32)[:, gate_order].reshape(DEPTH, N_GATE_COLS, 1),
        "conv_wk": conv_w[:, :, MLSTM_WIDTH:].astype(F32),
        "conv_bk": row(conv_b[:, MLSTM_WIDTH:]),
        "conv_q": cq,
        "sink": attn_sink.astype(F32),
        "att_g": row(attn_norm_g),
        "mlstm_g": mlstm_norm_g.astype(F32).reshape(DEPTH, MLSTM_WIDTH, 1),
        "w_out_a": w_out[:, :ATT_WIDTH].astype(BF16),
        "w_out_m": w_out[:, ATT_WIDTH:].astype(BF16),
        "ln1_g": row(ln1_g), "ln1_b": row(ln1_b),
        "wq": wq_mem.astype(BF16), "wkv": wkv_mem.astype(BF16), "wo": wo_mem.astype(BF16),
        "ln2_g": row(ln2_g), "ln2_b": row(ln2_b),
        "router_wt": _hi_lo_rows(router_w.astype(F32).T),
        "router_b": router_bias.astype(F32).reshape(N_EXPERTS, 1),
        "w_gate": w_gate.astype(BF16), "w_up": w_up.astype(BF16), "w_down": w_down.astype(BF16),
        "ln3_g": row(ln3_g), "ln3_b": row(ln3_b),
    }


def kernel(x_prompt, x_sample, mem_prompt, mem_sample, w_in, gate_bias, conv_w, conv_b, attn_sink, attn_norm_g, mlstm_norm_g, w_out, ln1_g, ln1_b, wq_mem, wkv_mem, wo_mem, ln2_g, ln2_b, router_w, router_bias, w_gate, w_up, w_down, ln3_g, ln3_b):
    wts = _prepare_weights(w_in, gate_bias, conv_w, conv_b, attn_sink, attn_norm_g, mlstm_norm_g, w_out, ln1_g, ln1_b,
                           wq_mem, wkv_mem, wo_mem, ln2_g, ln2_b, router_w, router_bias, w_gate, w_up, w_down,
                           ln3_g, ln3_b)
    consts = _constants()
    return (_trunk(x_prompt, mem_prompt, wts, consts), _trunk(x_sample, mem_sample, wts, consts))
```

```python
import functools
import math

import jax
import jax.numpy as jnp
from jax import lax
from jax.experimental import pallas as pl
from jax.experimental.pallas import tpu as pltpu

F32 = jnp.float32
BF16 = jnp.bfloat16

D_MODEL = 1024
DEPTH = 4
HEAD_DIM = 64
N_ATT_HEADS = 8
N_KV_HEADS = 2
ATT_WIDTH = N_ATT_HEADS * HEAD_DIM
KV_WIDTH = N_KV_HEADS * HEAD_DIM
BLOCK = 128
ROT_DIM = HEAD_DIM // 4
ROPE_THETA = 500000.0
MLSTM_WIDTH = D_MODEL - ATT_WIDTH
N_MLSTM_HEADS = 4
MLSTM_HEAD_DIM = MLSTM_WIDTH // N_MLSTM_HEADS
CHUNK = 128
OFF_AQ = 0
OFF_AK = OFF_AQ + ATT_WIDTH
OFF_AV = OFF_AK + KV_WIDTH
OFF_MQ = OFF_AV + KV_WIDTH
OFF_MV = OFF_MQ + 2 * MLSTM_WIDTH
OFF_MO = OFF_MV + MLSTM_WIDTH
OFF_G = OFF_MO + MLSTM_WIDTH
N_GATE_COLS = 4 * N_MLSTM_HEADS
N_MEM = 256
N_X_HEADS = 4
X_HEAD_DIM = D_MODEL // N_X_HEADS
N_EXPERTS = 16
N_GROUPS = 4
EXPERTS_PER_GROUP = N_EXPERTS // N_GROUPS
TOP_K = 2
D_FF_EXPERT = 512
ALPHA = (2.0 * DEPTH) ** 0.25
LN_EPS = 1e-5
RMS_EPS = 1e-6
NEG = -1e30

LANES = 128
TOKEN_TILE = 512
IN_PROJ_TILE = 1024
XATTN_TILE = 1024
ROW_BLOCK = 512
VMEM_LIMIT = 56 * 1024 * 1024


def _params(*sem):
    return pltpu.CompilerParams(dimension_semantics=sem, vmem_limit_bytes=VMEM_LIMIT)


def _layer_norm(z, g, b):
    mu = jnp.mean(z, axis=-1, keepdims=True)
    zc = z - mu
    var = jnp.mean(zc * zc, axis=-1, keepdims=True)
    return zc * lax.rsqrt(var + LN_EPS) * g + b


def _dot(a, b):
    return jnp.dot(a, b, preferred_element_type=F32)


def _dot_nt(a, b):
    return lax.dot_general(a, b, (((1,), (1,)), ((), ())), preferred_element_type=F32)


def _dot_tn(a, b):
    return lax.dot_general(a, b, (((0,), (0,)), ((), ())), preferred_element_type=F32)


_X_HALO = 8


_W_MK = OFF_MQ
_W_END = _W_MK + MLSTM_WIDTH
_GP_ROWS = 3 * 2 * N_MLSTM_HEADS


def _in_proj_kernel(x_ref, xp_ref, xn_ref, w_ref, wt_ref, gb_ref, cwk_ref, cbk_ref, cq_ref,
                    cos_ref, sa_ref, sb_ref, tri_u_ref, tri_l_ref,
                    qa_ref, ka_ref, va_ref, mk_ref, mqt_ref, mvt_ref, mot_ref, gp_ref, rn_ref, *, n_seq_tiles):
    xb = x_ref[...].astype(BF16)
    cos = cos_ref[...]
    sa = sa_ref[...]
    sb = sb_ref[...]

    def mm(lo, hi):
        return _dot(xb, w_ref[:, lo:hi])

    def rope(t):
        return t * cos + pltpu.roll(t, LANES - ROT_DIM // 2, 1) * sa + pltpu.roll(t, ROT_DIM // 2, 1) * sb

    q = mm(OFF_AQ, OFF_AK)
    scale = 1.0 / math.sqrt(HEAD_DIM)
    for j in range(ATT_WIDTH // LANES):
        qa_ref[:, j * LANES:(j + 1) * LANES] = (rope(q[:, j * LANES:(j + 1) * LANES]) * scale).astype(BF16)
    kv = mm(OFF_AK, OFF_MQ)
    ka_ref[...] = rope(kv[:, :KV_WIDTH]).astype(BF16)
    va_ref[...] = kv[:, KV_WIDTH:].astype(BF16)
    tm = xb.shape[0]
    pos = pl.program_id(0) % n_seq_tiles
    has_prev = pos > 0
    has_next = pos < n_seq_tiles - 1
    halo = jnp.concatenate([xp_ref[...], xn_ref[...]], axis=0).astype(BF16)

    def silu(y):
        return y * (1.0 / (1.0 + jnp.exp(-y)))

    u = mm(_W_MK, _W_END)
    uh = _dot(halo, w_ref[:, _W_MK:_W_END])
    rowi = lax.broadcasted_iota(jnp.int32, u.shape, 0)
    u_prev = jnp.where(rowi == 0, jnp.where(has_prev, uh[_X_HALO - 1:_X_HALO, :], 0.0), pltpu.roll(u, 1, 0))
    u_next = jnp.where(rowi == tm - 1, jnp.where(has_next, uh[_X_HALO:_X_HALO + 1, :], 0.0),
                       pltpu.roll(u, tm - 1, 0))
    yk = silu(cwk_ref[0:1, :] * u_prev + cwk_ref[1:2, :] * u + cwk_ref[2:3, :] * u_next + cbk_ref[...])
    mk_ref[...] = (yk * (MLSTM_HEAD_DIM ** -0.5)).astype(BF16)

    feat = _dot_nt(wt_ref[...], xb)
    ut = feat[:MLSTM_WIDTH]
    uht = _dot_nt(wt_ref[0:MLSTM_WIDTH, :], halo)
    lanei = lax.broadcasted_iota(jnp.int32, ut.shape, 1)
    ut_prev = jnp.where(lanei == 0, jnp.where(has_prev, uht[:, _X_HALO - 1:_X_HALO], 0.0), pltpu.roll(ut, 1, 1))
    ut_next = jnp.where(lanei == tm - 1, jnp.where(has_next, uht[:, _X_HALO:_X_HALO + 1], 0.0),
                        pltpu.roll(ut, tm - 1, 1))
    cq = cq_ref[...]
    mqt_ref[...] = silu(cq[:, 0:1] * ut_prev + cq[:, 1:2] * ut + cq[:, 2:3] * ut_next + cq[:, 3:4]).astype(BF16)
    mvt_ref[...] = feat[MLSTM_WIDTH:2 * MLSTM_WIDTH].astype(BF16)
    mot_ref[...] = feat[2 * MLSTM_WIDTH:3 * MLSTM_WIDTH].astype(BF16)

    gt = feat[3 * MLSTM_WIDTH:] + gb_ref[...]
    half = 2 * N_MLSTM_HEADS
    gi, gf = gt[:half], gt[half:]
    ls = jnp.minimum(gf, 0.0) - jnp.log1p(jnp.exp(-jnp.abs(gf)))
    n_chunks = tm // CHUNK
    stack = lambda t: jnp.concatenate([t[:, c * CHUNK:(c + 1) * CHUNK] for c in range(n_chunks)], axis=0)
    ls_rows, gi_rows = stack(ls), stack(gi)
    pre = jnp.dot(ls_rows, tri_u_ref[...], precision=lax.Precision.HIGHEST, preferred_element_type=F32)
    suf = jnp.dot(ls_rows, tri_l_ref[...], precision=lax.Precision.HIGHEST, preferred_element_type=F32)
    is_fwd = (lax.broadcasted_iota(jnp.int32, ls_rows.shape, 0) % half) < N_MLSTM_HEADS
    lane = lax.broadcasted_iota(jnp.int32, ls_rows.shape, 1)
    b = jnp.where(is_fwd, pre, suf)
    r = gi_rows - b
    cm_f, cm_b = r, r
    k = 1
    while k < CHUNK:
        cm_f = jnp.maximum(cm_f, jnp.where(lane >= k, pltpu.roll(cm_f, k, 1), -jnp.inf))
        cm_b = jnp.maximum(cm_b, jnp.where(lane < CHUNK - k, pltpu.roll(cm_b, CHUNK - k, 1), -jnp.inf))
        k *= 2
    cm = jnp.where(is_fwd, cm_f, cm_b)
    pad = jnp.zeros((CHUNK - half, CHUNK), F32)
    for c in range(n_chunks):
        rows = slice(c * half, (c + 1) * half)
        gp_ref[c] = jnp.concatenate([b[rows], cm[rows], r[rows]], axis=0)
        rn_ref[c * CHUNK:(c + 1) * CHUNK, :] = jnp.concatenate([r[rows], pad], axis=0).T


def _in_proj(x, w_rows, w_t, gb, cwk, cbk, cq, rope_tabs, tri_u, tri_l, seq):
    T = x.shape[0]
    tm = IN_PROJ_TILE
    n_seq_tiles = seq // tm
    halo_per_tile = tm // _X_HALO
    n_halo = T // _X_HALO
    cos_t, sa_t, sb_t = rope_tabs
    row_spec = lambda w: pl.BlockSpec((tm, w), lambda i: (i, 0))
    full = lambda a: pl.BlockSpec(a.shape, lambda i: (0,) * a.ndim)
    tab_spec = pl.BlockSpec((tm, LANES), lambda i: (i % n_seq_tiles, 0))
    prev_spec = pl.BlockSpec((_X_HALO, D_MODEL), lambda i: (jnp.maximum(i * halo_per_tile - 1, 0), 0))
    next_spec = pl.BlockSpec((_X_HALO, D_MODEL), lambda i: (jnp.minimum((i + 1) * halo_per_tile, n_halo - 1), 0))
    widths = (ATT_WIDTH, KV_WIDTH, KV_WIDTH, MLSTM_WIDTH)
    col_spec = _feature_major_spec(MLSTM_WIDTH, tm, n_seq_tiles)
    feat_major = jax.ShapeDtypeStruct((MLSTM_SEQS, MLSTM_WIDTH, T // MLSTM_SEQS), BF16)
    out_shapes = tuple(jax.ShapeDtypeStruct((T, w), BF16) for w in widths) + (
        feat_major, feat_major, feat_major,
        jax.ShapeDtypeStruct((T // CHUNK, _GP_ROWS, CHUNK), F32), jax.ShapeDtypeStruct((T, LANES), F32))
    out_specs = tuple(row_spec(w) for w in widths) + (
        col_spec, col_spec, col_spec,
        pl.BlockSpec((tm // CHUNK, _GP_ROWS, CHUNK), lambda i: (i, 0, 0)), row_spec(LANES))
    return pl.pallas_call(
        functools.partial(_in_proj_kernel, n_seq_tiles=n_seq_tiles),
        grid=(T // tm,),
        in_specs=[row_spec(D_MODEL), prev_spec, next_spec, full(w_rows), full(w_t), full(gb),
                  full(cwk), full(cbk), full(cq), tab_spec, tab_spec, tab_spec, full(tri_u), full(tri_l)],
        out_specs=out_specs,
        out_shape=out_shapes,
        compiler_params=_params("parallel"),
        name="in_proj",
    )(x, x, x, w_rows, w_t, gb, cwk, cbk, cq, cos_t, sa_t, sb_t, tri_u, tri_l)


ATT_Q_TILE = 512
_Q_BLOCKS = ATT_Q_TILE // BLOCK


def _attn_kernel(sink_ref, q_ref, kp_ref, k_ref, kn_ref, vp_ref, v_ref, vn_ref, g_ref, o_ref, *, n_tiles):
    i = pl.program_id(1)
    lane = lax.broadcasted_iota(jnp.int32, (ATT_Q_TILE + 2 * BLOCK, LANES), 1)
    low = lane < HEAD_DIM

    def split(prev_ref, own_ref, next_ref):
        t = jnp.concatenate([prev_ref[...], own_ref[...], next_ref[...]], axis=0).astype(F32)
        r = pltpu.roll(t, HEAD_DIM, 1)
        zero = jnp.zeros_like(t)
        lo = (jnp.where(low, t, zero).astype(BF16), jnp.where(low, r, zero).astype(BF16))
        hi = (jnp.where(low, zero, r).astype(BF16), jnp.where(low, zero, t).astype(BF16))
        return lo, hi

    k_lo, k_hi = split(kp_ref, k_ref, kn_ref)
    v_lo, v_hi = split(vp_ref, v_ref, vn_ref)

    rowi = lax.broadcasted_iota(jnp.int32, (BLOCK, 3 * BLOCK), 0)
    coli = lax.broadcasted_iota(jnp.int32, (BLOCK, 3 * BLOCK), 1)
    out_of_band = (coli < rowi) | (coli - 2 * BLOCK > rowi)
    band_bias = jnp.where(out_of_band, NEG, 0.0).astype(F32)
    first_bias = jnp.where(i == 0, NEG, 0.0).astype(F32)
    last_bias = jnp.where(i == n_tiles - 1, NEG, 0.0).astype(F32)
    bias_first = band_bias + jnp.where(coli < BLOCK, first_bias, 0.0)
    bias_last = band_bias + jnp.where(coli >= 2 * BLOCK, last_bias, 0.0)
    lane_o = lax.broadcasted_iota(jnp.int32, (BLOCK, LANES), 1)
    g = g_ref[...]

    for r in range(_Q_BLOCKS):
        bias = bias_first if r == 0 else (bias_last if r == _Q_BLOCKS - 1 else band_bias)
        rows = slice(r * BLOCK, (r + 1) * BLOCK)
        win = slice(r * BLOCK, (r + 3) * BLOCK)
        tiles = []
        for c in range(N_KV_HEADS):
            q2 = jnp.concatenate([q_ref[rows, (2 * c) * LANES:(2 * c + 1) * LANES],
                                  q_ref[rows, (2 * c + 1) * LANES:(2 * c + 2) * LANES]], axis=0)
            kc = jnp.concatenate([k_lo[c][win], k_hi[c][win]], axis=0)
            vc = jnp.concatenate([v_lo[c][win], v_hi[c][win]], axis=0)
            s = _dot_nt(q2, kc)
            p_rows, inv_rows = [], []
            for t in range(2):
                ps, invs = [], []
                for hh in range(2):
                    head = 4 * c + 2 * t + hh
                    sink = sink_ref[head]
                    sh = s[t * BLOCK:(t + 1) * BLOCK, hh * 3 * BLOCK:(hh + 1) * 3 * BLOCK] + bias
                    m = jnp.maximum(jnp.max(sh, axis=-1, keepdims=True), sink)
                    p = jnp.exp(sh - m)
                    den = jnp.sum(p, axis=-1, keepdims=True) + jnp.exp(sink - m)
                    ps.append(p.astype(BF16))
                    invs.append(1.0 / den)
                p_rows.append(jnp.concatenate(ps, axis=1))
                inv_rows.append(jnp.where(lane_o < HEAD_DIM, invs[0], invs[1]))
            o2 = _dot(jnp.concatenate(p_rows, axis=0), vc)
            tiles.append(o2[:BLOCK] * inv_rows[0])
            tiles.append(o2[BLOCK:] * inv_rows[1])
        o = jnp.concatenate(tiles, axis=1)
        ms = jnp.mean(o * o, axis=-1, keepdims=True)
        o_ref[rows, :] = (o * lax.rsqrt(ms + RMS_EPS) * g).astype(BF16)


def _attention(qa, ka, va, sink, att_g, batch, seq):
    T = qa.shape[0]
    n_tiles = seq // ATT_Q_TILE
    blocks_per_seq = seq // BLOCK
    own = lambda w: pl.BlockSpec((ATT_Q_TILE, w), lambda b, i: (b * n_tiles + i, 0))
    prev = pl.BlockSpec((BLOCK, KV_WIDTH),
                        lambda b, i: (b * blocks_per_seq + jnp.maximum(i * _Q_BLOCKS - 1, 0), 0))
    nxt = pl.BlockSpec((BLOCK, KV_WIDTH),
                       lambda b, i: (b * blocks_per_seq + jnp.minimum((i + 1) * _Q_BLOCKS, blocks_per_seq - 1), 0))
    return pl.pallas_call(
        functools.partial(_attn_kernel, n_tiles=n_tiles),
        grid=(batch, n_tiles),
        in_specs=[pl.BlockSpec(memory_space=pltpu.SMEM), own(ATT_WIDTH), prev, own(KV_WIDTH), nxt,
                  prev, own(KV_WIDTH), nxt, pl.BlockSpec((1, ATT_WIDTH), lambda b, i: (0, 0))],
        out_specs=own(ATT_WIDTH),
        out_shape=jax.ShapeDtypeStruct((T, ATT_WIDTH), BF16),
        compiler_params=_params("parallel", "parallel"),
        name="band_attention",
    )(sink, qa, ka, ka, ka, va, va, va, att_g)


MLSTM_SEQS = 4
_STATE_ROWS = MLSTM_HEAD_DIM + 8


def _mlstm_kernel(qt_f, vt_f, k_f, rn_f, gp_f, qt_b, vt_b, k_b, rn_b, gp_b, of_ref, ob_ref, c_state, m_state):
    @pl.when(pl.program_id(1) == 0)
    def _():
        c_state[...] = jnp.zeros_like(c_state)
        m_state[...] = jnp.zeros_like(m_state)

    key = lax.broadcasted_iota(jnp.int32, (CHUNK, CHUNK), 0)
    qry = lax.broadcasted_iota(jnp.int32, (CHUNK, CHUNK), 1)
    nh = N_MLSTM_HEADS

    units = []
    for sq in range(MLSTM_SEQS):
        for fwd, qt_ref, vt_ref, k_ref, rn_ref, gp_ref, o_ref in ((True, qt_f, vt_f, k_f, rn_f, gp_f, of_ref),
                                                                 (False, qt_b, vt_b, k_b, rn_b, gp_b, ob_ref)):
            gates = gp_ref[sq, 0]
            off = 0 if fwd else nh
            a_pos = CHUNK - 1 if fwd else 0
            for h in range(nh):
                hs = slice(h * MLSTM_HEAD_DIM, (h + 1) * MLSTM_HEAD_DIM)
                b = gates[off + h:off + h + 1, :]
                st = (2 * sq + (0 if fwd else 1)) * nh + h
                units.append(dict(
                    sq=sq, hs=hs, st=st, o_ref=o_ref, qt_ref=qt_ref, vt_ref=vt_ref, k_ref=k_ref,
                    visible=(key <= qry) if fwd else (key >= qry),
                    b=b, cm=gates[2 * nh + off + h:2 * nh + off + h + 1, :],
                    r_row=gates[4 * nh + off + h:4 * nh + off + h + 1, :],
                    r_keys=jnp.broadcast_to(rn_ref[sq, :, off + h:off + h + 1], (CHUNK, CHUNK)),
                    a=b[:, a_pos:a_pos + 1],
                    m_in=m_state[st:st + 1, 0:1]))

    for u in units:
        inter_log = u["b"] + u["m_in"]
        m_t = jnp.maximum(inter_log, u["b"] + u["cm"])
        u["decay"] = jnp.exp(jnp.where(u["visible"], u["r_keys"] + (u["b"] - m_t), NEG))
        u["inter_w"] = jnp.exp(inter_log - m_t)
        u["floor"] = jnp.exp(-m_t)
    for u in units:
        qt = u["qt_ref"][u["sq"], u["hs"], :]
        u["k"] = u["k_ref"][u["sq"], :, u["hs"]]
        u["vt"] = u["vt_ref"][u["sq"], u["hs"], :]
        u["scores"] = _dot(u["k"], qt)
        u["c_in"] = c_state[u["st"]]
        u["inter"] = _dot(u["c_in"].astype(BF16), qt)
    for u in units:
        sw = u["scores"] * u["decay"]
        num = _dot(u["vt"], sw.astype(BF16)) + u["inter_w"] * u["inter"][:MLSTM_HEAD_DIM]
        den = (jnp.sum(sw, axis=0, keepdims=True)
               + u["inter_w"] * u["inter"][MLSTM_HEAD_DIM:MLSTM_HEAD_DIM + 1])
        u["o_ref"][u["sq"], u["hs"], :] = (num / jnp.maximum(jnp.abs(den), u["floor"])).astype(u["o_ref"].dtype)
    for u in units:
        a, m_in, st = u["a"], u["m_in"], u["st"]
        g_max = a + jnp.max(u["r_row"], axis=-1, keepdims=True)
        kw = u["k"].astype(F32) * jnp.exp(u["r_keys"] + (a - g_max))
        m_new = jnp.maximum(a + m_in, g_max)
        keep = jnp.exp(a + m_in - m_new)
        add = jnp.exp(g_max - m_new)
        c_state[st, 0:MLSTM_HEAD_DIM, :] = keep * u["c_in"][:MLSTM_HEAD_DIM] + add * _dot(u["vt"], kw.astype(BF16))
        c_state[st, MLSTM_HEAD_DIM:MLSTM_HEAD_DIM + 1, :] = (
            keep * u["c_in"][MLSTM_HEAD_DIM:MLSTM_HEAD_DIM + 1] + add * jnp.sum(kw, axis=0, keepdims=True))
        m_state[st:st + 1, :] = jnp.broadcast_to(m_new, (1, LANES))


def _feature_major_spec(width, tm, tiles_per_seq):
    def index(i):
        s = i // tiles_per_seq
        return (s % MLSTM_SEQS, 0, (s // MLSTM_SEQS) * tiles_per_seq + i % tiles_per_seq)
    return pl.BlockSpec((None, width, tm), index)


def _mlstm(mqt, mvt, mk, rn, gp, batch, seq):
    T = mk.shape[0]
    nc = seq // CHUNK
    groups = batch // MLSTM_SEQS
    mk = mk.reshape(groups, MLSTM_SEQS, seq, MLSTM_WIDTH)
    rn = rn.reshape(groups, MLSTM_SEQS, seq, LANES)
    gp = gp.reshape(groups, MLSTM_SEQS, nc, _GP_ROWS, CHUNK)

    def specs(chunk_of):
        feat = pl.BlockSpec((MLSTM_SEQS, MLSTM_WIDTH, CHUNK), lambda b, i: (0, 0, b * nc + chunk_of(i)))
        return [feat, feat,
                pl.BlockSpec((None, MLSTM_SEQS, CHUNK, MLSTM_WIDTH), lambda b, i: (b, 0, chunk_of(i), 0)),
                pl.BlockSpec((None, MLSTM_SEQS, CHUNK, LANES), lambda b, i: (b, 0, chunk_of(i), 0)),
                pl.BlockSpec((None, MLSTM_SEQS, 1, _GP_ROWS, CHUNK), lambda b, i: (b, 0, chunk_of(i), 0, 0))]

    fwd_chunk = lambda i: i
    bwd_chunk = lambda i: nc - 1 - i
    out = jax.ShapeDtypeStruct((MLSTM_SEQS, MLSTM_WIDTH, T // MLSTM_SEQS), BF16)
    n_state = 2 * MLSTM_SEQS * N_MLSTM_HEADS
    return pl.pallas_call(
        _mlstm_kernel,
        grid=(groups, nc),
        in_specs=specs(fwd_chunk) + specs(bwd_chunk),
        out_specs=(specs(fwd_chunk)[0], specs(bwd_chunk)[0]),
        out_shape=(out, out),
        scratch_shapes=[pltpu.VMEM((n_state, _STATE_ROWS, MLSTM_HEAD_DIM), F32),
                        pltpu.VMEM((n_state, LANES), F32)],
        compiler_params=_params("parallel", "arbitrary"),
        name="mlstm",
    )(mqt, mvt, mk, rn, gp, mqt, mvt, mk, rn, gp)


def _mixer_out(x, att_ref, hf_ref, hb_ref, mo_ref, mg_ref, wa_ref, wm_ref, g_ref, b_ref):
    h = hf_ref[...].astype(F32) + hb_ref[...].astype(F32)
    parts = []
    for hd in range(N_MLSTM_HEADS):
        hh = h[hd * MLSTM_HEAD_DIM:(hd + 1) * MLSTM_HEAD_DIM]
        ms = jnp.mean(hh * hh, axis=0, keepdims=True)
        parts.append(hh * lax.rsqrt(ms + RMS_EPS))
    hn = jnp.concatenate(parts, axis=0) * mg_ref[...]
    gate = 1.0 / (1.0 + jnp.exp(-mo_ref[...].astype(F32)))
    y = _dot(att_ref[...], wa_ref[...]) + _dot_tn((hn * gate).astype(BF16), wm_ref[...])
    return _layer_norm(ALPHA * x + y, g_ref[...], b_ref[...])


def _kv_proj_kernel(m_ref, w_ref, k_ref, v_ref):
    mb = m_ref[...].astype(BF16)
    k_ref[...] = _dot(mb, w_ref[:, :D_MODEL]).astype(BF16)
    v_ref[...] = _dot(mb, w_ref[:, D_MODEL:]).astype(BF16)


def _kv_proj(mem, wkv):
    M = mem.shape[0]
    tm = TOKEN_TILE
    row = pl.BlockSpec((tm, D_MODEL), lambda i: (i, 0))
    out = jax.ShapeDtypeStruct((M, D_MODEL), BF16)
    return pl.pallas_call(
        _kv_proj_kernel,
        grid=(M // tm,),
        in_specs=[row, pl.BlockSpec(wkv.shape, lambda i: (0, 0))],
        out_specs=(row, row),
        out_shape=(out, out),
        compiler_params=_params("parallel"),
        name="mem_kv_proj",
    )(mem, wkv)


def _xattn_kernel(x_ref, att_ref, hf_ref, hb_ref, mo_ref, mg_ref, wa_ref, wm_ref, g1_ref, b1_ref,
                  k_ref, v_ref, wq_ref, wo_ref, g_ref, b_ref, rwt_ref, rb_ref, tri_ref, low_ref,
                  o_ref, info_ref, cnt_ref):
    x = _mixer_out(x_ref[...], att_ref, hf_ref, hb_ref, mo_ref, mg_ref, wa_ref, wm_ref, g1_ref, b1_ref)
    q = _dot(x.astype(BF16), wq_ref[...])
    scale = 1.0 / math.sqrt(X_HEAD_DIM)
    outs = []
    for h in range(N_X_HEADS):
        hs = slice(h * X_HEAD_DIM, (h + 1) * X_HEAD_DIM)
        s = _dot_nt((q[:, hs] * scale).astype(BF16), k_ref[:, hs])
        m = jnp.max(s, axis=-1, keepdims=True)
        p = jnp.exp(s - m)
        inv = 1.0 / jnp.sum(p, axis=-1, keepdims=True)
        outs.append((_dot(p.astype(BF16), v_ref[:, hs]) * inv).astype(BF16))
    y = _dot(jnp.concatenate(outs, axis=1), wo_ref[...])
    x2 = _layer_norm(ALPHA * x + y, g_ref[...], b_ref[...])
    o_ref[...] = x2
    for t in range(x2.shape[0] // TOKEN_TILE):
        rows = slice(t * TOKEN_TILE, (t + 1) * TOKEN_TILE)
        _route(x2[rows], rwt_ref, rb_ref, tri_ref, low_ref, info_ref.at[:, rows], cnt_ref.at[t:t + 1])


def _mixer_out_xattn(x, att, hft, hbt, mot, mg, wa, wm, g1, b1, k_mem, v_mem, wq, wo, g, b,
                     router_wt, router_b, tri_strict, low_strict, batch, seq):
    T = x.shape[0]
    tm = XATTN_TILE
    nt = seq // tm
    row = pl.BlockSpec((tm, D_MODEL), lambda bb, i: (bb * nt + i, 0))
    att_row = pl.BlockSpec((tm, ATT_WIDTH), lambda bb, i: (bb * nt + i, 0))
    flat_feat = _feature_major_spec(MLSTM_WIDTH, tm, nt)
    feat = pl.BlockSpec(flat_feat.block_shape, lambda bb, i: flat_feat.index_map(bb * nt + i))
    mem = pl.BlockSpec((N_MEM, D_MODEL), lambda bb, i: (bb, 0))
    full = lambda a: pl.BlockSpec(a.shape, lambda bb, i: (0,) * a.ndim)
    return pl.pallas_call(
        _xattn_kernel,
        grid=(batch, nt),
        in_specs=[row, att_row, feat, feat, feat, full(mg), full(wa), full(wm), full(g1), full(b1),
                  mem, mem, full(wq), full(wo), full(g), full(b), full(router_wt), full(router_b),
                  full(tri_strict), full(low_strict)],
        out_specs=(row, pl.BlockSpec((_IDX_ROWS, tm), lambda bb, i: (0, bb * nt + i)),
                   pl.BlockSpec((tm // TOKEN_TILE, N_EXPERTS, LANES), lambda bb, i: (bb * nt + i, 0, 0))),
        out_shape=(jax.ShapeDtypeStruct((T, D_MODEL), F32), jax.ShapeDtypeStruct((_IDX_ROWS, T), F32),
                   jax.ShapeDtypeStruct((T // TOKEN_TILE, N_EXPERTS, LANES), jnp.int32)),
        compiler_params=_params("parallel", "parallel"),
        name="mixer_out_xattn_route",
    )(x, att, hft, hbt, mot, mg, wa, wm, g1, b1, k_mem, v_mem, wq, wo, g, b, router_wt, router_b,
      tri_strict, low_strict)


_IDX_ROWS = 8
SLAB = 16
TILE_SLOTS = TOP_K * TOKEN_TILE + N_EXPERTS * SLAB


def _route(x, wt_ref, b_ref, tri_ref, low_ref, info_ref, cnt_ref):
    x_hi = x.astype(BF16)
    x_lo = (x - x_hi.astype(F32)).astype(BF16)
    by_hi = _dot_nt(wt_ref[...], x_hi)
    logits = by_hi[:N_EXPERTS] + by_hi[N_EXPERTS:] + _dot_nt(wt_ref[0:N_EXPERTS, :], x_lo)
    s = 1.0 / (1.0 + jnp.exp(-logits))
    sel = s + b_ref[...]
    srow = lambda e: s[e:e + 1, :]
    brow = lambda e: sel[e:e + 1, :]
    best = None
    gi = None
    for gidx in range(N_GROUPS):
        vals = [brow(gidx * EXPERTS_PER_GROUP + j) for j in range(EXPERTS_PER_GROUP)]
        top2 = None
        for a in range(EXPERTS_PER_GROUP):
            for b in range(a + 1, EXPERTS_PER_GROUP):
                pair = vals[a] + vals[b]
                top2 = pair if top2 is None else jnp.maximum(top2, pair)
        if best is None:
            best, gi = top2, jnp.zeros(top2.shape, jnp.int32)
        else:
            better = top2 > best
            gi = jnp.where(better, gidx, gi)
            best = jnp.where(better, top2, best)

    def in_group(rowfn, j):
        out = rowfn(j)
        for gidx in range(1, N_GROUPS):
            out = jnp.where(gi == gidx, rowfn(gidx * EXPERTS_PER_GROUP + j), out)
        return out

    bv = [in_group(brow, j) for j in range(EXPERTS_PER_GROUP)]
    sv = [in_group(srow, j) for j in range(EXPERTS_PER_GROUP)]

    def argmax_first(vals):
        bi = jnp.zeros(vals[0].shape, jnp.int32)
        bm = vals[0]
        for j in range(1, len(vals)):
            better = vals[j] > bm
            bi = jnp.where(better, j, bi)
            bm = jnp.where(better, vals[j], bm)
        return bi

    i1 = argmax_first(bv)
    i2 = argmax_first([jnp.where(i1 == j, -jnp.inf, bv[j]) for j in range(EXPERTS_PER_GROUP)])

    def pick(vals, idx):
        out = vals[0]
        for j in range(1, len(vals)):
            out = jnp.where(idx == j, vals[j], out)
        return out

    w1 = pick(sv, i1)
    w2 = pick(sv, i2)
    tot = w1 + w2
    tm = logits.shape[1]
    eid = lax.broadcasted_iota(jnp.int32, (N_EXPERTS, tm), 0)
    oh1 = (eid == gi * EXPERTS_PER_GROUP + i1).astype(F32)
    oh2 = (eid == gi * EXPERTS_PER_GROUP + i2).astype(F32)
    before1 = _dot(oh1.astype(BF16), tri_ref[...])
    before2 = _dot(oh2.astype(BF16), tri_ref[...])
    c1 = jnp.sum(oh1, axis=1, keepdims=True)
    cnt = c1 + jnp.sum(oh2, axis=1, keepdims=True)
    rows = jnp.floor((cnt + (SLAB - 1)) * (1.0 / SLAB)) * SLAB
    rows_b = jnp.broadcast_to(rows, (N_EXPERTS, LANES))
    start = jnp.dot(low_ref[...], rows_b, precision=lax.Precision.HIGHEST, preferred_element_type=F32)[:, 0:1]
    slot1 = jnp.sum(oh1 * (start + before1), axis=0, keepdims=True)
    slot2 = jnp.sum(oh2 * (start + c1 + before2), axis=0, keepdims=True)
    zf = jnp.zeros((_IDX_ROWS - 2 * TOP_K, tm), F32)
    info_ref[...] = jnp.concatenate([w1 / tot, w2 / tot, slot1, slot2, zf], axis=0)
    cnt_ref[0] = rows_b.astype(jnp.int32)


def _slab_copies(i, gs_ref, ls_ref, rc_ref, make_copy, op):
    for e in range(N_EXPERTS):
        idx = i * N_EXPERTS + e
        ls = ls_ref[idx]
        gs = gs_ref[idx]

        def body(j, carry, ls=ls, gs=gs):
            op(make_copy(pl.multiple_of(ls + j * SLAB, SLAB), pl.multiple_of(gs + j * SLAB, SLAB)))
            return carry

        lax.fori_loop(0, rc_ref[idx] // SLAB, body, 0)


_DISPATCH_CHUNK = 256
_ZERO_ROWS = 128


def _dispatch_kernel(gs_ref, ls_ref, rc_ref, ts_ref, tn_ref, x_ref, info_ref, xp_ref, xs_ref, zero_ref, sem, zsem):
    i = pl.program_id(0)
    last = pl.num_programs(0) - 1
    buf = i % 2
    slot0 = info_ref[2:3, :].astype(jnp.int32)
    slot1 = info_ref[3:4, :].astype(jnp.int32)
    xb = x_ref[...].astype(BF16)
    tm = xb.shape[0]
    for c in range(TILE_SLOTS // _DISPATCH_CHUNK):
        sid = lax.broadcasted_iota(jnp.int32, (_DISPATCH_CHUNK, tm), 0) + c * _DISPATCH_CHUNK
        sel = jnp.where((sid == slot0) | (sid == slot1), 1.0, 0.0).astype(BF16)
        xs_ref[buf, c * _DISPATCH_CHUNK:(c + 1) * _DISPATCH_CHUNK, :] = _dot(sel, xb).astype(BF16)

    def copies(tile, which, op):
        def make_copy(tile_row, global_row):
            return pltpu.make_async_copy(xs_ref.at[which, pl.ds(tile_row, SLAB), :],
                                         xp_ref.at[pl.ds(global_row, SLAB), :], sem.at[which])
        _slab_copies(tile, gs_ref, ls_ref, rc_ref, make_copy, op)

    @pl.when(i > 0)
    def _():
        copies(i - 1, 1 - buf, lambda cp: cp.wait())

    copies(i, buf, lambda cp: cp.start())

    @pl.when(i == last)
    def _():
        zero_ref[...] = jnp.zeros_like(zero_ref)

        def tail(op):
            for e in range(N_EXPERTS + 1):
                t0 = ts_ref[e]
                rows = SLAB if e < N_EXPERTS else _ZERO_ROWS

                def body(j, carry, t0=t0, rows=rows):
                    op(pltpu.make_async_copy(zero_ref.at[pl.ds(0, rows), :],
                                             xp_ref.at[pl.ds(pl.multiple_of(t0 + j * rows, rows), rows), :], zsem))
                    return carry

                lax.fori_loop(0, tn_ref[e] // rows, body, 0)

        tail(lambda cp: cp.start())
        copies(i, buf, lambda cp: cp.wait())
        tail(lambda cp: cp.wait())


def _dispatch(x, info, gstart, lstart, rc, tail_start, tail_rows, n_rows):
    T = x.shape[0]
    tm = TOKEN_TILE
    grid_spec = pltpu.PrefetchScalarGridSpec(
        num_scalar_prefetch=5,
        grid=(T // tm,),
        in_specs=[pl.BlockSpec((tm, D_MODEL), lambda i, *_: (i, 0)),
                  pl.BlockSpec((_IDX_ROWS, tm), lambda i, *_: (0, i))],
        out_specs=pl.BlockSpec(memory_space=pl.ANY),
        scratch_shapes=[pltpu.VMEM((2, TILE_SLOTS, D_MODEL), BF16), pltpu.VMEM((_ZERO_ROWS, D_MODEL), BF16),
                        pltpu.SemaphoreType.DMA((2,)), pltpu.SemaphoreType.DMA],
    )
    return pl.pallas_call(
        _dispatch_kernel,
        grid_spec=grid_spec,
        out_shape=jax.ShapeDtypeStruct((n_rows, D_MODEL), BF16),
        compiler_params=_params("arbitrary"),
        name="moe_dispatch",
    )(gstart, lstart, rc, tail_start, tail_rows, x, info)


def _ffn_kernel(be_ref, nu_ref, x_ref, wg_ref, wu_ref, wd_ref, o_ref):
    i = pl.program_id(0)

    @pl.when(i < nu_ref[0])
    def _():
        xb = x_ref[...]
        gate = _dot(xb, wg_ref[...])
        up = _dot(xb, wu_ref[...])
        h = gate * (1.0 / (1.0 + jnp.exp(-gate))) * up
        o_ref[...] = _dot(h.astype(BF16), wd_ref[...]).astype(BF16)

    @pl.when(i >= nu_ref[0])
    def _():
        o_ref[...] = jnp.zeros_like(o_ref)


def _expert_ffn(xp, blk_e, n_used, w_gate, w_up, w_down, layer):
    P = xp.shape[0]
    wspec = lambda shp: pl.BlockSpec((None, None) + shp, lambda i, be, nu: (layer, be[i], 0, 0))
    grid_spec = pltpu.PrefetchScalarGridSpec(
        num_scalar_prefetch=2,
        grid=(P // ROW_BLOCK,),
        in_specs=[pl.BlockSpec((ROW_BLOCK, D_MODEL), lambda i, be, nu: (jnp.minimum(i, nu[0] - 1), 0)),
                  wspec((D_MODEL, D_FF_EXPERT)), wspec((D_MODEL, D_FF_EXPERT)), wspec((D_FF_EXPERT, D_MODEL))],
        out_specs=pl.BlockSpec((ROW_BLOCK, D_MODEL), lambda i, be, nu: (i, 0)),
    )
    return pl.pallas_call(
        _ffn_kernel,
        grid_spec=grid_spec,
        out_shape=jax.ShapeDtypeStruct((P, D_MODEL), BF16),
        compiler_params=_params("arbitrary"),
        name="expert_ffn",
    )(blk_e, n_used, xp, w_gate, w_up, w_down)


def _combine_kernel(gs_ref, ls_ref, rc_ref, x_ref, info_ref, eye_ref, g_ref, b_ref, yp_ref, o_ref, ys_ref, sem):
    i = pl.program_id(0)
    buf = i % 2

    def copies(tile, which, op):
        def make_copy(tile_row, global_row):
            return pltpu.make_async_copy(yp_ref.at[pl.ds(global_row, SLAB), :],
                                         ys_ref.at[which, pl.ds(tile_row, SLAB), :], sem.at[which])
        _slab_copies(tile, gs_ref, ls_ref, rc_ref, make_copy, op)

    @pl.when(i == 0)
    def _():
        ys_ref[...] = jnp.zeros_like(ys_ref)
        copies(i, buf, lambda cp: cp.start())

    @pl.when(i + 1 < pl.num_programs(0))
    def _():
        copies(i + 1, 1 - buf, lambda cp: cp.start())

    cols = lax.dot_general(info_ref[...], eye_ref[...], (((0,), (0,)), ((), ())),
                           precision=lax.Precision.HIGHEST, preferred_element_type=F32)
    tm = cols.shape[0]
    sid = lax.broadcasted_iota(jnp.int32, (tm, TILE_SLOTS), 1)
    weights = (jnp.where(sid == cols[:, 2:3].astype(jnp.int32), cols[:, 0:1], 0.0)
               + jnp.where(sid == cols[:, 3:4].astype(jnp.int32), cols[:, 1:2], 0.0)).astype(BF16)
    copies(i, buf, lambda cp: cp.wait())
    y = _dot(weights, ys_ref[buf])
    o_ref[...] = _layer_norm(ALPHA * x_ref[...] + y, g_ref[...], b_ref[...])


def _combine_ln(x, yp, info, gstart, lstart, rc, eye, g, b):
    T = x.shape[0]
    tm = TOKEN_TILE
    full = lambda a: pl.BlockSpec(a.shape, lambda i, *_: (0,) * a.ndim)
    grid_spec = pltpu.PrefetchScalarGridSpec(
        num_scalar_prefetch=3,
        grid=(T // tm,),
        in_specs=[pl.BlockSpec((tm, D_MODEL), lambda i, *_: (i, 0)),
                  pl.BlockSpec((_IDX_ROWS, tm), lambda i, *_: (0, i)), full(eye), full(g), full(b),
                  pl.BlockSpec(memory_space=pl.ANY)],
        out_specs=pl.BlockSpec((tm, D_MODEL), lambda i, *_: (i, 0)),
        scratch_shapes=[pltpu.VMEM((2, TILE_SLOTS, D_MODEL), BF16), pltpu.SemaphoreType.DMA((2,))],
    )
    return pl.pallas_call(
        _combine_kernel,
        grid_spec=grid_spec,
        out_shape=jax.ShapeDtypeStruct((T, D_MODEL), F32),
        compiler_params=_params("arbitrary"),
        name="moe_combine_ln3",
    )(gstart, lstart, rc, x, info, eye, g, b, yp)


def _moe(x, info, rows, consts, w_gate, w_up, w_down, layer, g, b):
    T = x.shape[0]
    n_tiles = T // TOKEN_TILE
    P = T * TOP_K + n_tiles * N_EXPERTS * SLAB + N_EXPERTS * ROW_BLOCK
    rc = rows[:, :, 0]
    lstart = jnp.cumsum(rc, axis=1) - rc
    region = (jnp.sum(rc, axis=0) + ROW_BLOCK - 1) // ROW_BLOCK * ROW_BLOCK
    region_end = jnp.cumsum(region)
    gstart = (region_end - region)[None, :] + jnp.cumsum(rc, axis=0) - rc
    n_blk = P // ROW_BLOCK
    blk_row0 = jnp.arange(n_blk, dtype=jnp.int32) * ROW_BLOCK
    blk_e = jnp.minimum(jnp.sum(region_end[None, :] <= blk_row0[:, None], axis=1), N_EXPERTS - 1).astype(jnp.int32)
    n_used = (region_end[-1:] // ROW_BLOCK).astype(jnp.int32)
    flat = lambda t: t.reshape(-1).astype(jnp.int32)
    used = jnp.sum(rc, axis=0)
    tail_start = flat(jnp.concatenate([region_end - region + used, region_end[-1:]]))
    tail_rows = flat(jnp.concatenate([region - used, P - region_end[-1:]]))
    gstart, lstart, rc = flat(gstart), flat(lstart), flat(rc)
    xp = _dispatch(x, info, gstart, lstart, rc, tail_start, tail_rows, P)
    yp = _expert_ffn(xp, blk_e, n_used, w_gate, w_up, w_down, layer)
    return _combine_ln(x, yp, info, gstart, lstart, rc, consts["eye"], g, b)


def _rope_tables(seq):
    inv_freq = ROPE_THETA ** (-jnp.arange(0, ROT_DIM, 2, dtype=F32) / ROT_DIM)
    ang = jnp.arange(seq, dtype=F32)[:, None] * inv_freq[None, :]
    cos, sin = jnp.cos(ang), jnp.sin(ang)
    half = ROT_DIM // 2
    one = jnp.ones((seq, HEAD_DIM - ROT_DIM), F32)
    zero = jnp.zeros((seq, HEAD_DIM - ROT_DIM), F32)
    zh = jnp.zeros((seq, half), F32)
    cos_h = jnp.concatenate([cos, cos, one], axis=1)
    sa_h = jnp.concatenate([-sin, zh, zero], axis=1)
    sb_h = jnp.concatenate([zh, sin, zero], axis=1)
    rep = lambda t: jnp.concatenate([t] * (LANES // HEAD_DIM), axis=1)
    return rep(cos_h), rep(sa_h), rep(sb_h)


def _constants():
    r = lax.broadcasted_iota(jnp.int32, (CHUNK, CHUNK), 0)
    c = lax.broadcasted_iota(jnp.int32, (CHUNK, CHUNK), 1)
    rr = lax.broadcasted_iota(jnp.int32, (TOKEN_TILE, TOKEN_TILE), 0)
    cc = lax.broadcasted_iota(jnp.int32, (TOKEN_TILE, TOKEN_TILE), 1)
    er = lax.broadcasted_iota(jnp.int32, (N_EXPERTS, N_EXPERTS), 0)
    ec = lax.broadcasted_iota(jnp.int32, (N_EXPERTS, N_EXPERTS), 1)
    return {
        "tri_u": (r <= c).astype(F32),
        "tri_l": (r >= c).astype(F32),
        "tri_strict": (rr < cc).astype(BF16),
        "low_strict": (ec < er).astype(F32),
        "eye": (lax.broadcasted_iota(jnp.int32, (_IDX_ROWS, LANES), 0)
                == lax.broadcasted_iota(jnp.int32, (_IDX_ROWS, LANES), 1)).astype(F32),
    }


def _trunk(x, mem, wts, consts):
    batch, seq, _ = x.shape
    T = batch * seq
    x = x.reshape(T, D_MODEL)
    mem2 = mem.reshape(batch * N_MEM, D_MODEL)
    rope_tabs = _rope_tables(seq)
    for l in range(DEPTH):
        qa, ka, va, mk, mqt, mvt, mot, gp, rn = _in_proj(
            x, wts["w_rows"][l], wts["w_t"][l], wts["gb"][l], wts["conv_wk"][l], wts["conv_bk"][l],
            wts["conv_q"][l], rope_tabs, consts["tri_u"], consts["tri_l"], seq)
        att = _attention(qa, ka, va, wts["sink"][l], wts["att_g"][l], batch, seq)
        hft, hbt = _mlstm(mqt, mvt, mk, rn, gp, batch, seq)
        k_mem, v_mem = _kv_proj(mem2, wts["wkv"][l])
        x, info, rows = _mixer_out_xattn(
            x, att, hft, hbt, mot, wts["mlstm_g"][l], wts["w_out_a"][l], wts["w_out_m"][l], wts["ln1_g"][l],
            wts["ln1_b"][l], k_mem, v_mem, wts["wq"][l], wts["wo"][l], wts["ln2_g"][l], wts["ln2_b"][l],
            wts["router_wt"], wts["router_b"], consts["tri_strict"], consts["low_strict"], batch, seq)
        x = _moe(x, info, rows, consts, wts["w_gate"], wts["w_up"], wts["w_down"], l,
                 wts["ln3_g"][l], wts["ln3_b"][l])
    return x.reshape(batch, seq, D_MODEL)


def _hi_lo_rows(w):
    hi = w.astype(BF16)
    return jnp.concatenate([hi, (w - hi.astype(F32)).astype(BF16)], axis=0)


def _prepare_weights(w_in, gate_bias, conv_w, conv_b, attn_sink, attn_norm_g, mlstm_norm_g, w_out, ln1_g, ln1_b,
                     wq_mem, wkv_mem, wo_mem, ln2_g, ln2_b, router_w, router_bias, w_gate, w_up, w_down, ln3_g, ln3_b):
    row = lambda t: t.astype(F32).reshape(DEPTH, 1, t.shape[-1])
    nh = N_MLSTM_HEADS
    gate_order = jnp.array(list(range(0, nh)) + list(range(2 * nh, 3 * nh))
                           + list(range(nh, 2 * nh)) + list(range(3 * nh, 4 * nh)), jnp.int32)
    cq = jnp.concatenate([jnp.swapaxes(conv_w[:, :, :MLSTM_WIDTH], 1, 2), conv_b[:, :MLSTM_WIDTH, None],
                          jnp.zeros((DEPTH, MLSTM_WIDTH, 4), conv_w.dtype)], axis=2).astype(F32)
    feature_major = jnp.concatenate([w_in[:, :, OFF_MQ:OFF_MQ + MLSTM_WIDTH], w_in[:, :, OFF_MV:OFF_G],
                                     w_in[:, :, OFF_G:][:, :, gate_order]], axis=2)
    return {
        "w_rows": jnp.concatenate([w_in[:, :, :OFF_MQ], w_in[:, :, OFF_MQ + MLSTM_WIDTH:OFF_MV]],
                                  axis=2).astype(BF16),
        "w_t": jnp.swapaxes(feature_major, 1, 2).astype(BF16),
        "gb": gate_bias.astype(F32)[:, gate_order].reshape(DEPTH, N_GATE_COLS, 1),
        "conv_wk": conv_w[:, :, MLSTM_WIDTH:].astype(F32),
        "conv_bk": row(conv_b[:, MLSTM_WIDTH:]),
        "conv_q": cq,
        "sink": attn_sink.astype(F32),
        "att_g": row(attn_norm_g),
        "mlstm_g": mlstm_norm_g.astype(F32).reshape(DEPTH, MLSTM_WIDTH, 1),
        "w_out_a": w_out[:, :ATT_WIDTH].astype(BF16),
        "w_out_m": w_out[:, ATT_WIDTH:].astype(BF16),
        "ln1_g": row(ln1_g), "ln1_b": row(ln1_b),
        "wq": wq_mem.astype(BF16), "wkv": wkv_mem.astype(BF16), "wo": wo_mem.astype(BF16),
        "ln2_g": row(ln2_g), "ln2_b": row(ln2_b),
        "router_wt": _hi_lo_rows(router_w.astype(F32).T),
        "router_b": router_bias.astype(F32).reshape(N_EXPERTS, 1),
        "w_gate": w_gate.astype(BF16), "w_up": w_up.astype(BF16), "w_down": w_down.astype(BF16),
        "ln3_g": row(ln3_g), "ln3_b": row(ln3_b),
    }


def kernel(x_prompt, x_sample, mem_prompt, mem_sample, w_in, gate_bias, conv_w, conv_b, attn_sink, attn_norm_g, mlstm_norm_g, w_out, ln1_g, ln1_b, wq_mem, wkv_mem, wo_mem, ln2_g, ln2_b, router_w, router_bias, w_gate, w_up, w_down, ln3_g, ln3_b):
    wts = _prepare_weights(w_in, gate_bias, conv_w, conv_b, attn_sink, attn_norm_g, mlstm_norm_g, w_out, ln1_g, ln1_b,
                           wq_mem, wkv_mem, wo_mem, ln2_g, ln2_b, router_w, router_bias, w_gate, w_up, w_down,
                           ln3_g, ln3_b)
    consts = _constants()
    return (_trunk(x_prompt, mem_prompt, wts, consts), _trunk(x_sample, mem_sample, wts, consts))
```

```python
import functools
import math

import jax
import jax.numpy as jnp
from jax import lax
from jax.experimental import pallas as pl
from jax.experimental.pallas import tpu as pltpu

F32 = jnp.float32
BF16 = jnp.bfloat16

D_MODEL = 1024
DEPTH = 4
HEAD_DIM = 64
N_ATT_HEADS = 8
N_KV_HEADS = 2
ATT_WIDTH = N_ATT_HEADS * HEAD_DIM
KV_WIDTH = N_KV_HEADS * HEAD_DIM
BLOCK = 128
ROT_DIM = HEAD_DIM // 4
ROPE_THETA = 500000.0
MLSTM_WIDTH = D_MODEL - ATT_WIDTH
N_MLSTM_HEADS = 4
MLSTM_HEAD_DIM = MLSTM_WIDTH // N_MLSTM_HEADS
CHUNK = 128
OFF_AQ = 0
OFF_AK = OFF_AQ + ATT_WIDTH
OFF_AV = OFF_AK + KV_WIDTH
OFF_MQ = OFF_AV + KV_WIDTH
OFF_MV = OFF_MQ + 2 * MLSTM_WIDTH
OFF_MO = OFF_MV + MLSTM_WIDTH
OFF_G = OFF_MO + MLSTM_WIDTH
N_GATE_COLS = 4 * N_MLSTM_HEADS
N_MEM = 256
N_X_HEADS = 4
X_HEAD_DIM = D_MODEL // N_X_HEADS
N_EXPERTS = 16
N_GROUPS = 4
EXPERTS_PER_GROUP = N_EXPERTS // N_GROUPS
TOP_K = 2
D_FF_EXPERT = 512
ALPHA = (2.0 * DEPTH) ** 0.25
LN_EPS = 1e-5
RMS_EPS = 1e-6
NEG = -1e30

LANES = 128
TOKEN_TILE = 512
IN_PROJ_TILE = 1024
XATTN_TILE = 1024
ROW_BLOCK = 512
VMEM_LIMIT = 56 * 1024 * 1024


def _params(*sem):
    return pltpu.CompilerParams(dimension_semantics=sem, vmem_limit_bytes=VMEM_LIMIT)


def _layer_norm(z, g, b):
    mu = jnp.mean(z, axis=-1, keepdims=True)
    zc = z - mu
    var = jnp.mean(zc * zc, axis=-1, keepdims=True)
    return zc * lax.rsqrt(var + LN_EPS) * g + b


def _dot(a, b):
    return jnp.dot(a, b, preferred_element_type=F32)


def _dot_nt(a, b):
    return lax.dot_general(a, b, (((1,), (1,)), ((), ())), preferred_element_type=F32)


def _dot_tn(a, b):
    return lax.dot_general(a, b, (((0,), (0,)), ((), ())), preferred_element_type=F32)


_X_HALO = 8


_W_MK = OFF_MQ
_W_END = _W_MK + MLSTM_WIDTH
_GP_ROWS = 3 * 2 * N_MLSTM_HEADS


def _in_proj_kernel(x_ref, xp_ref, xn_ref, w_ref, wt_ref, gb_ref, cwk_ref, cbk_ref, cq_ref,
                    cos_ref, sa_ref, sb_ref, tri_u_ref, tri_l_ref,
                    qa_ref, ka_ref, va_ref, mk_ref, mqt_ref, mvt_ref, mot_ref, gp_ref, rn_ref, *, n_seq_tiles):
    xb = x_ref[...].astype(BF16)
    cos = cos_ref[...]
    sa = sa_ref[...]
    sb = sb_ref[...]

    def mm(lo, hi):
        return _dot(xb, w_ref[:, lo:hi])

    def rope(t):
        return t * cos + pltpu.roll(t, LANES - ROT_DIM // 2, 1) * sa + pltpu.roll(t, ROT_DIM // 2, 1) * sb

    q = mm(OFF_AQ, OFF_AK)
    scale = math.log2(math.e) / math.sqrt(HEAD_DIM)
    for j in range(ATT_WIDTH // LANES):
        qa_ref[:, j * LANES:(j + 1) * LANES] = (rope(q[:, j * LANES:(j + 1) * LANES]) * scale).astype(BF16)
    kv = mm(OFF_AK, OFF_MQ)
    ka_ref[...] = rope(kv[:, :KV_WIDTH]).astype(BF16)
    va_ref[...] = kv[:, KV_WIDTH:].astype(BF16)
    tm = xb.shape[0]
    pos = pl.program_id(0) % n_seq_tiles
    has_prev = pos > 0
    has_next = pos < n_seq_tiles - 1
    halo = jnp.concatenate([xp_ref[...], xn_ref[...]], axis=0).astype(BF16)

    def silu(y):
        return y * (1.0 / (1.0 + jnp.exp(-y)))

    u = mm(_W_MK, _W_END)
    uh = _dot(halo, w_ref[:, _W_MK:_W_END])
    rowi = lax.broadcasted_iota(jnp.int32, u.shape, 0)
    u_prev = jnp.where(rowi == 0, jnp.where(has_prev, uh[_X_HALO - 1:_X_HALO, :], 0.0), pltpu.roll(u, 1, 0))
    u_next = jnp.where(rowi == tm - 1, jnp.where(has_next, uh[_X_HALO:_X_HALO + 1, :], 0.0),
                       pltpu.roll(u, tm - 1, 0))
    yk = silu(cwk_ref[0:1, :] * u_prev + cwk_ref[1:2, :] * u + cwk_ref[2:3, :] * u_next + cbk_ref[...])
    mk_ref[...] = (yk * (MLSTM_HEAD_DIM ** -0.5)).astype(BF16)

    feat = _dot_nt(wt_ref[...], xb)
    ut = feat[:MLSTM_WIDTH]
    uht = _dot_nt(wt_ref[0:MLSTM_WIDTH, :], halo)
    lanei = lax.broadcasted_iota(jnp.int32, ut.shape, 1)
    ut_prev = jnp.where(lanei == 0, jnp.where(has_prev, uht[:, _X_HALO - 1:_X_HALO], 0.0), pltpu.roll(ut, 1, 1))
    ut_next = jnp.where(lanei == tm - 1, jnp.where(has_next, uht[:, _X_HALO:_X_HALO + 1], 0.0),
                        pltpu.roll(ut, tm - 1, 1))
    cq = cq_ref[...]
    mqt_ref[...] = silu(cq[:, 0:1] * ut_prev + cq[:, 1:2] * ut + cq[:, 2:3] * ut_next + cq[:, 3:4]).astype(BF16)
    mvt_ref[...] = feat[MLSTM_WIDTH:2 * MLSTM_WIDTH].astype(BF16)
    mot_ref[...] = feat[2 * MLSTM_WIDTH:3 * MLSTM_WIDTH].astype(BF16)

    gt = feat[3 * MLSTM_WIDTH:] + gb_ref[...]
    half = 2 * N_MLSTM_HEADS
    gi, gf = gt[:half], gt[half:]
    ls = jnp.minimum(gf, 0.0) - jnp.log1p(jnp.exp(-jnp.abs(gf)))
    n_chunks = tm // CHUNK
    stack = lambda t: jnp.concatenate([t[:, c * CHUNK:(c + 1) * CHUNK] for c in range(n_chunks)], axis=0)
    ls_rows, gi_rows = stack(ls), stack(gi)
    pre = jnp.dot(ls_rows, tri_u_ref[...], precision=lax.Precision.HIGHEST, preferred_element_type=F32)
    suf = jnp.dot(ls_rows, tri_l_ref[...], precision=lax.Precision.HIGHEST, preferred_element_type=F32)
    is_fwd = (lax.broadcasted_iota(jnp.int32, ls_rows.shape, 0) % half) < N_MLSTM_HEADS
    lane = lax.broadcasted_iota(jnp.int32, ls_rows.shape, 1)
    b = jnp.where(is_fwd, pre, suf)
    r = gi_rows - b
    cm_f, cm_b = r, r
    k = 1
    while k < CHUNK:
        cm_f = jnp.maximum(cm_f, jnp.where(lane >= k, pltpu.roll(cm_f, k, 1), -jnp.inf))
        cm_b = jnp.maximum(cm_b, jnp.where(lane < CHUNK - k, pltpu.roll(cm_b, CHUNK - k, 1), -jnp.inf))
        k *= 2
    cm = jnp.where(is_fwd, cm_f, cm_b)
    pad = jnp.zeros((CHUNK - half, CHUNK), F32)
    for c in range(n_chunks):
        rows = slice(c * half, (c + 1) * half)
        gp_ref[c] = jnp.concatenate([b[rows], cm[rows], r[rows]], axis=0)
        rn_ref[c * CHUNK:(c + 1) * CHUNK, :] = jnp.concatenate([r[rows], pad], axis=0).T


def _in_proj(x, w_rows, w_t, gb, cwk, cbk, cq, rope_tabs, tri_u, tri_l, seq):
    T = x.shape[0]
    tm = IN_PROJ_TILE
    n_seq_tiles = seq // tm
    halo_per_tile = tm // _X_HALO
    n_halo = T // _X_HALO
    cos_t, sa_t, sb_t = rope_tabs
    row_spec = lambda w: pl.BlockSpec((tm, w), lambda i: (i, 0))
    full = lambda a: pl.BlockSpec(a.shape, lambda i: (0,) * a.ndim)
    tab_spec = pl.BlockSpec((tm, LANES), lambda i: (i % n_seq_tiles, 0))
    prev_spec = pl.BlockSpec((_X_HALO, D_MODEL), lambda i: (jnp.maximum(i * halo_per_tile - 1, 0), 0))
    next_spec = pl.BlockSpec((_X_HALO, D_MODEL), lambda i: (jnp.minimum((i + 1) * halo_per_tile, n_halo - 1), 0))
    widths = (ATT_WIDTH, KV_WIDTH, KV_WIDTH, MLSTM_WIDTH)
    col_spec = _feature_major_spec(MLSTM_WIDTH, tm, n_seq_tiles)
    feat_major = jax.ShapeDtypeStruct((MLSTM_SEQS, MLSTM_WIDTH, T // MLSTM_SEQS), BF16)
    out_shapes = tuple(jax.ShapeDtypeStruct((T, w), BF16) for w in widths) + (
        feat_major, feat_major, feat_major,
        jax.ShapeDtypeStruct((T // CHUNK, _GP_ROWS, CHUNK), F32), jax.ShapeDtypeStruct((T, LANES), F32))
    out_specs = tuple(row_spec(w) for w in widths) + (
        col_spec, col_spec, col_spec,
        pl.BlockSpec((tm // CHUNK, _GP_ROWS, CHUNK), lambda i: (i, 0, 0)), row_spec(LANES))
    return pl.pallas_call(
        functools.partial(_in_proj_kernel, n_seq_tiles=n_seq_tiles),
        grid=(T // tm,),
        in_specs=[row_spec(D_MODEL), prev_spec, next_spec, full(w_rows), full(w_t), full(gb),
                  full(cwk), full(cbk), full(cq), tab_spec, tab_spec, tab_spec, full(tri_u), full(tri_l)],
        out_specs=out_specs,
        out_shape=out_shapes,
        compiler_params=_params("parallel"),
        name="in_proj",
    )(x, x, x, w_rows, w_t, gb, cwk, cbk, cq, cos_t, sa_t, sb_t, tri_u, tri_l)


ATT_Q_TILE = 512
_Q_BLOCKS = ATT_Q_TILE // BLOCK


def _attn_kernel(sink_ref, q_ref, kp_ref, k_ref, kn_ref, vp_ref, v_ref, vn_ref, g_ref, o_ref, *, n_tiles):
    i = pl.program_id(1)
    lane = lax.broadcasted_iota(jnp.int32, (ATT_Q_TILE + 2 * BLOCK, LANES), 1)
    low = lane < HEAD_DIM

    def split(prev_ref, own_ref, next_ref):
        t = jnp.concatenate([prev_ref[...], own_ref[...], next_ref[...]], axis=0).astype(F32)
        r = pltpu.roll(t, HEAD_DIM, 1)
        zero = jnp.zeros_like(t)
        lo = (jnp.where(low, t, zero).astype(BF16), jnp.where(low, r, zero).astype(BF16))
        hi = (jnp.where(low, zero, r).astype(BF16), jnp.where(low, zero, t).astype(BF16))
        return lo, hi

    k_lo, k_hi = split(kp_ref, k_ref, kn_ref)
    v_lo, v_hi = split(vp_ref, v_ref, vn_ref)

    rowi = lax.broadcasted_iota(jnp.int32, (BLOCK, BLOCK), 0)
    coli = lax.broadcasted_iota(jnp.int32, (BLOCK, BLOCK), 1)
    prev_bias = jnp.where(coli < rowi, NEG, 0.0).astype(F32)
    next_bias = jnp.where(coli > rowi, NEG, 0.0).astype(F32)
    first_bias = jnp.where(i == 0, NEG, 0.0).astype(F32)
    last_bias = jnp.where(i == n_tiles - 1, NEG, 0.0).astype(F32)
    lane_o = lax.broadcasted_iota(jnp.int32, (BLOCK, LANES), 1)
    g = g_ref[...]
    log2e = math.log2(math.e)

    for r in range(_Q_BLOCKS):
        pb = prev_bias + first_bias if r == 0 else prev_bias
        nb = next_bias + last_bias if r == _Q_BLOCKS - 1 else next_bias
        rows = slice(r * BLOCK, (r + 1) * BLOCK)
        win = slice(r * BLOCK, (r + 3) * BLOCK)
        tiles = []
        for c in range(N_KV_HEADS):
            q2 = jnp.concatenate([q_ref[rows, (2 * c) * LANES:(2 * c + 1) * LANES],
                                  q_ref[rows, (2 * c + 1) * LANES:(2 * c + 2) * LANES]], axis=0)
            kc = jnp.concatenate([k_lo[c][win], k_hi[c][win]], axis=0)
            vc = jnp.concatenate([v_lo[c][win], v_hi[c][win]], axis=0)
            s = _dot_nt(q2, kc)
            p_rows, inv_rows = [], []
            for t in range(2):
                ps, invs = [], []
                for hh in range(2):
                    head = 4 * c + 2 * t + hh
                    sink = sink_ref[head] * log2e
                    blk = lambda j: s[t * BLOCK:(t + 1) * BLOCK, (3 * hh + j) * BLOCK:(3 * hh + j + 1) * BLOCK]
                    sp, so, sn = blk(0) + pb, blk(1), blk(2) + nb
                    m = jnp.maximum(jnp.max(jnp.maximum(jnp.maximum(sp, so), sn), axis=-1, keepdims=True), sink)
                    pp, po, pn = jnp.exp2(sp - m), jnp.exp2(so - m), jnp.exp2(sn - m)
                    den = jnp.sum(pp + po + pn, axis=-1, keepdims=True) + jnp.exp2(sink - m)
                    ps += [pp.astype(BF16), po.astype(BF16), pn.astype(BF16)]
                    invs.append(1.0 / den)
                p_rows.append(jnp.concatenate(ps, axis=1))
                inv_rows.append(jnp.where(lane_o < HEAD_DIM, invs[0], invs[1]))
            o2 = _dot(jnp.concatenate(p_rows, axis=0), vc)
            tiles.append(o2[:BLOCK] * inv_rows[0])
            tiles.append(o2[BLOCK:] * inv_rows[1])
        o = jnp.concatenate(tiles, axis=1)
        ms = jnp.mean(o * o, axis=-1, keepdims=True)
        o_ref[rows, :] = (o * lax.rsqrt(ms + RMS_EPS) * g).astype(BF16)


def _attention(qa, ka, va, sink, att_g, batch, seq):
    T = qa.shape[0]
    n_tiles = seq // ATT_Q_TILE
    blocks_per_seq = seq // BLOCK
    own = lambda w: pl.BlockSpec((ATT_Q_TILE, w), lambda b, i: (b * n_tiles + i, 0))
    prev = pl.BlockSpec((BLOCK, KV_WIDTH),
                        lambda b, i: (b * blocks_per_seq + jnp.maximum(i * _Q_BLOCKS - 1, 0), 0))
    nxt = pl.BlockSpec((BLOCK, KV_WIDTH),
                       lambda b, i: (b * blocks_per_seq + jnp.minimum((i + 1) * _Q_BLOCKS, blocks_per_seq - 1), 0))
    return pl.pallas_call(
        functools.partial(_attn_kernel, n_tiles=n_tiles),
        grid=(batch, n_tiles),
        in_specs=[pl.BlockSpec(memory_space=pltpu.SMEM), own(ATT_WIDTH), prev, own(KV_WIDTH), nxt,
                  prev, own(KV_WIDTH), nxt, pl.BlockSpec((1, ATT_WIDTH), lambda b, i: (0, 0))],
        out_specs=own(ATT_WIDTH),
        out_shape=jax.ShapeDtypeStruct((T, ATT_WIDTH), BF16),
        compiler_params=_params("parallel", "parallel"),
        name="band_attention",
    )(sink, qa, ka, ka, ka, va, va, va, att_g)


MLSTM_SEQS = 4
_STATE_ROWS = MLSTM_HEAD_DIM + 8


def _mlstm_kernel(qt_f, vt_f, k_f, rn_f, gp_f, qt_b, vt_b, k_b, rn_b, gp_b, of_ref, ob_ref, c_state, m_state):
    @pl.when(pl.program_id(1) == 0)
    def _():
        c_state[...] = jnp.zeros_like(c_state)
        m_state[...] = jnp.zeros_like(m_state)

    key = lax.broadcasted_iota(jnp.int32, (CHUNK, CHUNK), 0)
    qry = lax.broadcasted_iota(jnp.int32, (CHUNK, CHUNK), 1)
    nh = N_MLSTM_HEADS

    units = []
    for sq in range(MLSTM_SEQS):
        for fwd, qt_ref, vt_ref, k_ref, rn_ref, gp_ref, o_ref in ((True, qt_f, vt_f, k_f, rn_f, gp_f, of_ref),
                                                                 (False, qt_b, vt_b, k_b, rn_b, gp_b, ob_ref)):
            gates = gp_ref[sq, 0]
            off = 0 if fwd else nh
            a_pos = CHUNK - 1 if fwd else 0
            for h in range(nh):
                hs = slice(h * MLSTM_HEAD_DIM, (h + 1) * MLSTM_HEAD_DIM)
                b = gates[off + h:off + h + 1, :]
                st = (2 * sq + (0 if fwd else 1)) * nh + h
                units.append(dict(
                    sq=sq, hs=hs, st=st, o_ref=o_ref, qt_ref=qt_ref, vt_ref=vt_ref, k_ref=k_ref,
                    visible=(key <= qry) if fwd else (key >= qry),
                    b=b, cm=gates[2 * nh + off + h:2 * nh + off + h + 1, :],
                    r_row=gates[4 * nh + off + h:4 * nh + off + h + 1, :],
                    r_keys=jnp.broadcast_to(rn_ref[sq, :, off + h:off + h + 1], (CHUNK, CHUNK)),
                    a=b[:, a_pos:a_pos + 1],
                    m_in=m_state[st:st + 1, 0:1]))

    for u in units:
        inter_log = u["b"] + u["m_in"]
        m_t = jnp.maximum(inter_log, u["b"] + u["cm"])
        u["decay"] = jnp.exp(jnp.where(u["visible"], u["r_keys"] + (u["b"] - m_t), NEG))
        u["inter_w"] = jnp.exp(inter_log - m_t)
        u["floor"] = jnp.exp(-m_t)
    for u in units:
        qt = u["qt_ref"][u["sq"], u["hs"], :]
        u["k"] = u["k_ref"][u["sq"], :, u["hs"]]
        u["vt"] = u["vt_ref"][u["sq"], u["hs"], :]
        u["scores"] = _dot(u["k"], qt)
        u["c_in"] = c_state[u["st"]]
        u["inter"] = _dot(u["c_in"].astype(BF16), qt)
    for u in units:
        sw = u["scores"] * u["decay"]
        num = _dot(u["vt"], sw.astype(BF16)) + u["inter_w"] * u["inter"][:MLSTM_HEAD_DIM]
        den = (jnp.sum(sw, axis=0, keepdims=True)
               + u["inter_w"] * u["inter"][MLSTM_HEAD_DIM:MLSTM_HEAD_DIM + 1])
        u["o_ref"][u["sq"], u["hs"], :] = (num / jnp.maximum(jnp.abs(den), u["floor"])).astype(u["o_ref"].dtype)
    for u in units:
        a, m_in, st = u["a"], u["m_in"], u["st"]
        g_max = a + jnp.max(u["r_row"], axis=-1, keepdims=True)
        kw = u["k"].astype(F32) * jnp.exp(u["r_keys"] + (a - g_max))
        m_new = jnp.maximum(a + m_in, g_max)
        keep = jnp.exp(a + m_in - m_new)
        add = jnp.exp(g_max - m_new)
        c_state[st, 0:MLSTM_HEAD_DIM, :] = keep * u["c_in"][:MLSTM_HEAD_DIM] + add * _dot(u["vt"], kw.astype(BF16))
        c_state[st, MLSTM_HEAD_DIM:MLSTM_HEAD_DIM + 1, :] = (
            keep * u["c_in"][MLSTM_HEAD_DIM:MLSTM_HEAD_DIM + 1] + add * jnp.sum(kw, axis=0, keepdims=True))
        m_state[st:st + 1, :] = jnp.broadcast_to(m_new, (1, LANES))


def _feature_major_spec(width, tm, tiles_per_seq):
    def index(i):
        s = i // tiles_per_seq
        return (s % MLSTM_SEQS, 0, (s // MLSTM_SEQS) * tiles_per_seq + i % tiles_per_seq)
    return pl.BlockSpec((None, width, tm), index)


def _mlstm(mqt, mvt, mk, rn, gp, batch, seq):
    T = mk.shape[0]
    nc = seq // CHUNK
    groups = batch // MLSTM_SEQS
    mk = mk.reshape(groups, MLSTM_SEQS, seq, MLSTM_WIDTH)
    rn = rn.reshape(groups, MLSTM_SEQS, seq, LANES)
    gp = gp.reshape(groups, MLSTM_SEQS, nc, _GP_ROWS, CHUNK)

    def specs(chunk_of):
        feat = pl.BlockSpec((MLSTM_SEQS, MLSTM_WIDTH, CHUNK), lambda b, i: (0, 0, b * nc + chunk_of(i)))
        return [feat, feat,
                pl.BlockSpec((None, MLSTM_SEQS, CHUNK, MLSTM_WIDTH), lambda b, i: (b, 0, chunk_of(i), 0)),
                pl.BlockSpec((None, MLSTM_SEQS, CHUNK, LANES), lambda b, i: (b, 0, chunk_of(i), 0)),
                pl.BlockSpec((None, MLSTM_SEQS, 1, _GP_ROWS, CHUNK), lambda b, i: (b, 0, chunk_of(i), 0, 0))]

    fwd_chunk = lambda i: i
    bwd_chunk = lambda i: nc - 1 - i
    out = jax.ShapeDtypeStruct((MLSTM_SEQS, MLSTM_WIDTH, T // MLSTM_SEQS), BF16)
    n_state = 2 * MLSTM_SEQS * N_MLSTM_HEADS
    return pl.pallas_call(
        _mlstm_kernel,
        grid=(groups, nc),
        in_specs=specs(fwd_chunk) + specs(bwd_chunk),
        out_specs=(specs(fwd_chunk)[0], specs(bwd_chunk)[0]),
        out_shape=(out, out),
        scratch_shapes=[pltpu.VMEM((n_state, _STATE_ROWS, MLSTM_HEAD_DIM), F32),
                        pltpu.VMEM((n_state, LANES), F32)],
        compiler_params=_params("parallel", "arbitrary"),
        name="mlstm",
    )(mqt, mvt, mk, rn, gp, mqt, mvt, mk, rn, gp)


def _mixer_out(x, att_ref, hf_ref, hb_ref, mo_ref, mg_ref, wa_ref, wm_ref, g_ref, b_ref):
    h = hf_ref[...].astype(F32) + hb_ref[...].astype(F32)
    parts = []
    for hd in range(N_MLSTM_HEADS):
        hh = h[hd * MLSTM_HEAD_DIM:(hd + 1) * MLSTM_HEAD_DIM]
        ms = jnp.mean(hh * hh, axis=0, keepdims=True)
        parts.append(hh * lax.rsqrt(ms + RMS_EPS))
    hn = jnp.concatenate(parts, axis=0) * mg_ref[...]
    gate = 1.0 / (1.0 + jnp.exp(-mo_ref[...].astype(F32)))
    y = _dot(att_ref[...], wa_ref[...]) + _dot_tn((hn * gate).astype(BF16), wm_ref[...])
    return _layer_norm(ALPHA * x + y, g_ref[...], b_ref[...])


def _kv_proj_kernel(m_ref, w_ref, k_ref, v_ref):
    mb = m_ref[...].astype(BF16)
    k_ref[...] = _dot(mb, w_ref[:, :D_MODEL]).astype(BF16)
    v_ref[...] = _dot(mb, w_ref[:, D_MODEL:]).astype(BF16)


def _kv_proj(mem, wkv):
    M = mem.shape[0]
    tm = TOKEN_TILE
    row = pl.BlockSpec((tm, D_MODEL), lambda i: (i, 0))
    out = jax.ShapeDtypeStruct((M, D_MODEL), BF16)
    return pl.pallas_call(
        _kv_proj_kernel,
        grid=(M // tm,),
        in_specs=[row, pl.BlockSpec(wkv.shape, lambda i: (0, 0))],
        out_specs=(row, row),
        out_shape=(out, out),
        compiler_params=_params("parallel"),
        name="mem_kv_proj",
    )(mem, wkv)


def _xattn_kernel(x_ref, att_ref, hf_ref, hb_ref, mo_ref, mg_ref, wa_ref, wm_ref, g1_ref, b1_ref,
                  k_ref, v_ref, wq_ref, wo_ref, g_ref, b_ref, rwt_ref, rb_ref, tri_ref, low_ref,
                  o_ref, info_ref, cnt_ref):
    subs = [slice(t * TOKEN_TILE, (t + 1) * TOKEN_TILE) for t in range(x_ref.shape[0] // TOKEN_TILE)]
    xs = [_mixer_out(x_ref[rows], att_ref.at[rows], hf_ref.at[:, rows], hb_ref.at[:, rows], mo_ref.at[:, rows],
                     mg_ref, wa_ref, wm_ref, g1_ref, b1_ref) for rows in subs]
    scale = math.log2(math.e) / math.sqrt(X_HEAD_DIM)
    qs = [_dot(x.astype(BF16), wq_ref[...]) for x in xs]
    outs = [[] for _ in subs]
    for h in range(N_X_HEADS):
        hs = slice(h * X_HEAD_DIM, (h + 1) * X_HEAD_DIM)
        for t, q in enumerate(qs):
            s = _dot_nt((q[:, hs] * scale).astype(BF16), k_ref[:, hs])
            p = jnp.exp2(s - jnp.max(s, axis=-1, keepdims=True))
            inv = 1.0 / jnp.sum(p, axis=-1, keepdims=True)
            outs[t].append((_dot(p.astype(BF16), v_ref[:, hs]) * inv).astype(BF16))
    ys = [_dot(jnp.concatenate(o, axis=1), wo_ref[...]) for o in outs]
    for t, rows in enumerate(subs):
        x2 = _layer_norm(ALPHA * xs[t] + ys[t], g_ref[...], b_ref[...])
        o_ref[rows, :] = x2
        _route(x2, rwt_ref, rb_ref, tri_ref, low_ref, info_ref.at[:, rows], cnt_ref.at[t:t + 1])


def _mixer_out_xattn(x, att, hft, hbt, mot, mg, wa, wm, g1, b1, k_mem, v_mem, wq, wo, g, b,
                     router_wt, router_b, tri_strict, low_strict, batch, seq):
    T = x.shape[0]
    tm = XATTN_TILE
    nt = seq // tm
    row = pl.BlockSpec((tm, D_MODEL), lambda bb, i: (bb * nt + i, 0))
    att_row = pl.BlockSpec((tm, ATT_WIDTH), lambda bb, i: (bb * nt + i, 0))
    flat_feat = _feature_major_spec(MLSTM_WIDTH, tm, nt)
    feat = pl.BlockSpec(flat_feat.block_shape, lambda bb, i: flat_feat.index_map(bb * nt + i))
    mem = pl.BlockSpec((N_MEM, D_MODEL), lambda bb, i: (bb, 0))
    full = lambda a: pl.BlockSpec(a.shape, lambda bb, i: (0,) * a.ndim)
    return pl.pallas_call(
        _xattn_kernel,
        grid=(batch, nt),
        in_specs=[row, att_row, feat, feat, feat, full(mg), full(wa), full(wm), full(g1), full(b1),
                  mem, mem, full(wq), full(wo), full(g), full(b), full(router_wt), full(router_b),
                  full(tri_strict), full(low_strict)],
        out_specs=(row, pl.BlockSpec((_IDX_ROWS, tm), lambda bb, i: (0, bb * nt + i)),
                   pl.BlockSpec((tm // TOKEN_TILE, N_EXPERTS, LANES), lambda bb, i: (bb * nt + i, 0, 0))),
        out_shape=(jax.ShapeDtypeStruct((T, D_MODEL), F32), jax.ShapeDtypeStruct((_IDX_ROWS, T), F32),
                   jax.ShapeDtypeStruct((T // TOKEN_TILE, N_EXPERTS, LANES), jnp.int32)),
        compiler_params=_params("parallel", "parallel"),
        name="mixer_out_xattn_route",
    )(x, att, hft, hbt, mot, mg, wa, wm, g1, b1, k_mem, v_mem, wq, wo, g, b, router_wt, router_b,
      tri_strict, low_strict)


_IDX_ROWS = 8
SLAB = 16
TILE_SLOTS = TOP_K * TOKEN_TILE + N_EXPERTS * SLAB


def _route(x, wt_ref, b_ref, tri_ref, low_ref, info_ref, cnt_ref):
    x_hi = x.astype(BF16)
    x_lo = (x - x_hi.astype(F32)).astype(BF16)
    by_hi = _dot_nt(wt_ref[...], x_hi)
    logits = by_hi[:N_EXPERTS] + by_hi[N_EXPERTS:] + _dot_nt(wt_ref[0:N_EXPERTS, :], x_lo)
    s = 1.0 / (1.0 + jnp.exp(-logits))
    sel = s + b_ref[...]
    srow = lambda e: s[e:e + 1, :]
    brow = lambda e: sel[e:e + 1, :]
    best = None
    gi = None
    for gidx in range(N_GROUPS):
        vals = [brow(gidx * EXPERTS_PER_GROUP + j) for j in range(EXPERTS_PER_GROUP)]
        top2 = None
        for a in range(EXPERTS_PER_GROUP):
            for b in range(a + 1, EXPERTS_PER_GROUP):
                pair = vals[a] + vals[b]
                top2 = pair if top2 is None else jnp.maximum(top2, pair)
        if best is None:
            best, gi = top2, jnp.zeros(top2.shape, jnp.int32)
        else:
            better = top2 > best
            gi = jnp.where(better, gidx, gi)
            best = jnp.where(better, top2, best)

    def in_group(rowfn, j):
        out = rowfn(j)
        for gidx in range(1, N_GROUPS):
            out = jnp.where(gi == gidx, rowfn(gidx * EXPERTS_PER_GROUP + j), out)
        return out

    bv = [in_group(brow, j) for j in range(EXPERTS_PER_GROUP)]
    sv = [in_group(srow, j) for j in range(EXPERTS_PER_GROUP)]

    def argmax_first(vals):
        bi = jnp.zeros(vals[0].shape, jnp.int32)
        bm = vals[0]
        for j in range(1, len(vals)):
            better = vals[j] > bm
            bi = jnp.where(better, j, bi)
            bm = jnp.where(better, vals[j], bm)
        return bi

    i1 = argmax_first(bv)
    i2 = argmax_first([jnp.where(i1 == j, -jnp.inf, bv[j]) for j in range(EXPERTS_PER_GROUP)])

    def pick(vals, idx):
        out = vals[0]
        for j in range(1, len(vals)):
            out = jnp.where(idx == j, vals[j], out)
        return out

    w1 = pick(sv, i1)
    w2 = pick(sv, i2)
    tot = w1 + w2
    tm = logits.shape[1]
    eid = lax.broadcasted_iota(jnp.int32, (N_EXPERTS, tm), 0)
    oh1 = (eid == gi * EXPERTS_PER_GROUP + i1).astype(F32)
    oh2 = (eid == gi * EXPERTS_PER_GROUP + i2).astype(F32)
    before1 = _dot(oh1.astype(BF16), tri_ref[...])
    before2 = _dot(oh2.astype(BF16), tri_ref[...])
    c1 = jnp.sum(oh1, axis=1, keepdims=True)
    cnt = c1 + jnp.sum(oh2, axis=1, keepdims=True)
    rows = jnp.floor((cnt + (SLAB - 1)) * (1.0 / SLAB)) * SLAB
    rows_b = jnp.broadcast_to(rows, (N_EXPERTS, LANES))
    start = jnp.dot(low_ref[...], rows_b, precision=lax.Precision.HIGHEST, preferred_element_type=F32)[:, 0:1]
    slot1 = jnp.sum(oh1 * (start + before1), axis=0, keepdims=True)
    slot2 = jnp.sum(oh2 * (start + c1 + before2), axis=0, keepdims=True)
    zf = jnp.zeros((_IDX_ROWS - 2 * TOP_K, tm), F32)
    info_ref[...] = jnp.concatenate([w1 / tot, w2 / tot, slot1, slot2, zf], axis=0)
    cnt_ref[0] = rows_b.astype(jnp.int32)


_BIG_SLAB = 4 * SLAB


def _slab_copies(i, gs_ref, ls_ref, rc_ref, make_copy, op):
    for e in range(N_EXPERTS):
        idx = i * N_EXPERTS + e
        ls = ls_ref[idx]
        gs = gs_ref[idx]
        n_big = rc_ref[idx] // _BIG_SLAB
        done = n_big * _BIG_SLAB

        def big(j, carry, ls=ls, gs=gs):
            op(make_copy(pl.multiple_of(ls + j * _BIG_SLAB, SLAB), pl.multiple_of(gs + j * _BIG_SLAB, SLAB),
                         _BIG_SLAB))
            return carry

        def small(j, carry, ls=ls + done, gs=gs + done):
            op(make_copy(pl.multiple_of(ls + j * SLAB, SLAB), pl.multiple_of(gs + j * SLAB, SLAB), SLAB))
            return carry

        lax.fori_loop(0, n_big, big, 0)
        lax.fori_loop(0, (rc_ref[idx] - done) // SLAB, small, 0)


_DISPATCH_CHUNK = 256
_ZERO_ROWS = 128
_COMBINE_SUBTILES = 2


def _dispatch_kernel(gs_ref, ls_ref, rc_ref, ts_ref, tn_ref, x_ref, info_ref, xp_ref, xs_ref, zero_ref, sem, zsem):
    i = pl.program_id(0)
    last = pl.num_programs(0) - 1
    buf = i % 2
    slot0 = info_ref[2:3, :].astype(jnp.int32)
    slot1 = info_ref[3:4, :].astype(jnp.int32)
    xb = x_ref[...].astype(BF16)
    tm = xb.shape[0]
    for c in range(TILE_SLOTS // _DISPATCH_CHUNK):
        sid = lax.broadcasted_iota(jnp.int32, (_DISPATCH_CHUNK, tm), 0) + c * _DISPATCH_CHUNK
        sel = jnp.where((sid == slot0) | (sid == slot1), 1.0, 0.0).astype(BF16)
        xs_ref[buf, c * _DISPATCH_CHUNK:(c + 1) * _DISPATCH_CHUNK, :] = _dot(sel, xb).astype(BF16)

    def copies(tile, which, op):
        def make_copy(tile_row, global_row, rows):
            return pltpu.make_async_copy(xs_ref.at[which, pl.ds(tile_row, rows), :],
                                         xp_ref.at[pl.ds(global_row, rows), :], sem.at[which])
        _slab_copies(tile, gs_ref, ls_ref, rc_ref, make_copy, op)

    @pl.when(i > 0)
    def _():
        copies(i - 1, 1 - buf, lambda cp: cp.wait())

    copies(i, buf, lambda cp: cp.start())

    @pl.when(i == last)
    def _():
        zero_ref[...] = jnp.zeros_like(zero_ref)

        def tail(op):
            for e in range(N_EXPERTS + 1):
                t0 = ts_ref[e]
                rows = SLAB if e < N_EXPERTS else _ZERO_ROWS

                def body(j, carry, t0=t0, rows=rows):
                    op(pltpu.make_async_copy(zero_ref.at[pl.ds(0, rows), :],
                                             xp_ref.at[pl.ds(pl.multiple_of(t0 + j * rows, rows), rows), :], zsem))
                    return carry

                lax.fori_loop(0, tn_ref[e] // rows, body, 0)

        tail(lambda cp: cp.start())
        copies(i, buf, lambda cp: cp.wait())
        tail(lambda cp: cp.wait())


def _dispatch(x, info, gstart, lstart, rc, tail_start, tail_rows, n_rows):
    T = x.shape[0]
    tm = TOKEN_TILE
    grid_spec = pltpu.PrefetchScalarGridSpec(
        num_scalar_prefetch=5,
        grid=(T // tm,),
        in_specs=[pl.BlockSpec((tm, D_MODEL), lambda i, *_: (i, 0)),
                  pl.BlockSpec((_IDX_ROWS, tm), lambda i, *_: (0, i))],
        out_specs=pl.BlockSpec(memory_space=pl.ANY),
        scratch_shapes=[pltpu.VMEM((2, TILE_SLOTS, D_MODEL), BF16), pltpu.VMEM((_ZERO_ROWS, D_MODEL), BF16),
                        pltpu.SemaphoreType.DMA((2,)), pltpu.SemaphoreType.DMA],
    )
    return pl.pallas_call(
        _dispatch_kernel,
        grid_spec=grid_spec,
        out_shape=jax.ShapeDtypeStruct((n_rows, D_MODEL), BF16),
        compiler_params=_params("arbitrary"),
        name="moe_dispatch",
    )(gstart, lstart, rc, tail_start, tail_rows, x, info)


def _ffn_kernel(be_ref, nu_ref, x_ref, wg_ref, wu_ref, wd_ref, o_ref):
    i = pl.program_id(0)

    @pl.when(i < nu_ref[0])
    def _():
        xb = x_ref[...]
        gate = _dot(xb, wg_ref[...])
        up = _dot(xb, wu_ref[...])
        h = gate * (1.0 / (1.0 + jnp.exp(-gate))) * up
        o_ref[...] = _dot(h.astype(BF16), wd_ref[...]).astype(BF16)

    @pl.when(i >= nu_ref[0])
    def _():
        o_ref[...] = jnp.zeros_like(o_ref)


def _expert_ffn(xp, blk_e, n_used, w_gate, w_up, w_down, layer):
    P = xp.shape[0]
    wspec = lambda shp: pl.BlockSpec((None, None) + shp, lambda i, be, nu: (layer, be[i], 0, 0))
    grid_spec = pltpu.PrefetchScalarGridSpec(
        num_scalar_prefetch=2,
        grid=(P // ROW_BLOCK,),
        in_specs=[pl.BlockSpec((ROW_BLOCK, D_MODEL), lambda i, be, nu: (jnp.minimum(i, nu[0] - 1), 0)),
                  wspec((D_MODEL, D_FF_EXPERT)), wspec((D_MODEL, D_FF_EXPERT)), wspec((D_FF_EXPERT, D_MODEL))],
        out_specs=pl.BlockSpec((ROW_BLOCK, D_MODEL), lambda i, be, nu: (i, 0)),
    )
    return pl.pallas_call(
        _ffn_kernel,
        grid_spec=grid_spec,
        out_shape=jax.ShapeDtypeStruct((P, D_MODEL), BF16),
        compiler_params=_params("arbitrary"),
        name="expert_ffn",
    )(blk_e, n_used, xp, w_gate, w_up, w_down)


def _combine_kernel(gs_ref, ls_ref, rc_ref, x_ref, info_ref, eye_ref, g_ref, b_ref, yp_ref, o_ref, ys_ref, sem):
    i = pl.program_id(0)
    buf = i % 2

    def copies(tile, which, op):
        def make_copy(tile_row, global_row, rows):
            return pltpu.make_async_copy(yp_ref.at[pl.ds(global_row, rows), :],
                                         ys_ref.at[which, pl.ds(tile_row, rows), :], sem.at[which])
        _slab_copies(tile, gs_ref, ls_ref, rc_ref, make_copy, op)

    @pl.when(i == 0)
    def _():
        ys_ref[...] = jnp.zeros_like(ys_ref)
        copies(i, buf, lambda cp: cp.start())

    @pl.when(i + 1 < pl.num_programs(0))
    def _():
        copies(i + 1, 1 - buf, lambda cp: cp.start())

    cols = lax.dot_general(info_ref[...], eye_ref[...], (((0,), (0,)), ((), ())),
                           precision=lax.Precision.HIGHEST, preferred_element_type=F32)
    tm = cols.shape[0]
    sub = tm // _COMBINE_SUBTILES
    sid = lax.broadcasted_iota(jnp.int32, (sub, TILE_SLOTS), 1)
    weights = []
    for t in range(_COMBINE_SUBTILES):
        c = cols[t * sub:(t + 1) * sub]
        weights.append((jnp.where(sid == c[:, 2:3].astype(jnp.int32), c[:, 0:1], 0.0)
                        + jnp.where(sid == c[:, 3:4].astype(jnp.int32), c[:, 1:2], 0.0)).astype(BF16))
    copies(i, buf, lambda cp: cp.wait())
    ys = [_dot(w, ys_ref[buf]) for w in weights]
    for t in range(_COMBINE_SUBTILES):
        rows = slice(t * sub, (t + 1) * sub)
        o_ref[rows, :] = _layer_norm(ALPHA * x_ref[rows, :] + ys[t], g_ref[...], b_ref[...])


def _combine_ln(x, yp, info, gstart, lstart, rc, eye, g, b):
    T = x.shape[0]
    tm = TOKEN_TILE
    full = lambda a: pl.BlockSpec(a.shape, lambda i, *_: (0,) * a.ndim)
    grid_spec = pltpu.PrefetchScalarGridSpec(
        num_scalar_prefetch=3,
        grid=(T // tm,),
        in_specs=[pl.BlockSpec((tm, D_MODEL), lambda i, *_: (i, 0)),
                  pl.BlockSpec((_IDX_ROWS, tm), lambda i, *_: (0, i)), full(eye), full(g), full(b),
                  pl.BlockSpec(memory_space=pl.ANY)],
        out_specs=pl.BlockSpec((tm, D_MODEL), lambda i, *_: (i, 0)),
        scratch_shapes=[pltpu.VMEM((2, TILE_SLOTS, D_MODEL), BF16), pltpu.SemaphoreType.DMA((2,))],
    )
    return pl.pallas_call(
        _combine_kernel,
        grid_spec=grid_spec,
        out_shape=jax.ShapeDtypeStruct((T, D_MODEL), F32),
        compiler_params=_params("arbitrary"),
        name="moe_combine_ln3",
    )(gstart, lstart, rc, x, info, eye, g, b, yp)


def _moe(x, info, rows, consts, w_gate, w_up, w_down, layer, g, b):
    T = x.shape[0]
    n_tiles = T // TOKEN_TILE
    P = T * TOP_K + n_tiles * N_EXPERTS * SLAB + N_EXPERTS * ROW_BLOCK
    rc = rows[:, :, 0]
    lstart = jnp.cumsum(rc, axis=1) - rc
    region = (jnp.sum(rc, axis=0) + ROW_BLOCK - 1) // ROW_BLOCK * ROW_BLOCK
    region_end = jnp.cumsum(region)
    gstart = (region_end - region)[None, :] + jnp.cumsum(rc, axis=0) - rc
    n_blk = P // ROW_BLOCK
    blk_row0 = jnp.arange(n_blk, dtype=jnp.int32) * ROW_BLOCK
    blk_e = jnp.minimum(jnp.sum(region_end[None, :] <= blk_row0[:, None], axis=1), N_EXPERTS - 1).astype(jnp.int32)
    n_used = (region_end[-1:] // ROW_BLOCK).astype(jnp.int32)
    flat = lambda t: t.reshape(-1).astype(jnp.int32)
    used = jnp.sum(rc, axis=0)
    tail_start = flat(jnp.concatenate([region_end - region + used, region_end[-1:]]))
    tail_rows = flat(jnp.concatenate([region - used, P - region_end[-1:]]))
    gstart, lstart, rc = flat(gstart), flat(lstart), flat(rc)
    xp = _dispatch(x, info, gstart, lstart, rc, tail_start, tail_rows, P)
    yp = _expert_ffn(xp, blk_e, n_used, w_gate, w_up, w_down, layer)
    return _combine_ln(x, yp, info, gstart, lstart, rc, consts["eye"], g, b)


def _rope_tables(seq):
    inv_freq = ROPE_THETA ** (-jnp.arange(0, ROT_DIM, 2, dtype=F32) / ROT_DIM)
    ang = jnp.arange(seq, dtype=F32)[:, None] * inv_freq[None, :]
    cos, sin = jnp.cos(ang), jnp.sin(ang)
    half = ROT_DIM // 2
    one = jnp.ones((seq, HEAD_DIM - ROT_DIM), F32)
    zero = jnp.zeros((seq, HEAD_DIM - ROT_DIM), F32)
    zh = jnp.zeros((seq, half), F32)
    cos_h = jnp.concatenate([cos, cos, one], axis=1)
    sa_h = jnp.concatenate([-sin, zh, zero], axis=1)
    sb_h = jnp.concatenate([zh, sin, zero], axis=1)
    rep = lambda t: jnp.concatenate([t] * (LANES // HEAD_DIM), axis=1)
    return rep(cos_h), rep(sa_h), rep(sb_h)


def _constants():
    r = lax.broadcasted_iota(jnp.int32, (CHUNK, CHUNK), 0)
    c = lax.broadcasted_iota(jnp.int32, (CHUNK, CHUNK), 1)
    rr = lax.broadcasted_iota(jnp.int32, (TOKEN_TILE, TOKEN_TILE), 0)
    cc = lax.broadcasted_iota(jnp.int32, (TOKEN_TILE, TOKEN_TILE), 1)
    er = lax.broadcasted_iota(jnp.int32, (N_EXPERTS, N_EXPERTS), 0)
    ec = lax.broadcasted_iota(jnp.int32, (N_EXPERTS, N_EXPERTS), 1)
    return {
        "tri_u": (r <= c).astype(F32),
        "tri_l": (r >= c).astype(F32),
        "tri_strict": (rr < cc).astype(BF16),
        "low_strict": (ec < er).astype(F32),
        "eye": (lax.broadcasted_iota(jnp.int32, (_IDX_ROWS, LANES), 0)
                == lax.broadcasted_iota(jnp.int32, (_IDX_ROWS, LANES), 1)).astype(F32),
    }


def _trunk(x, mem, wts, consts):
    batch, seq, _ = x.shape
    T = batch * seq
    x = x.reshape(T, D_MODEL)
    mem2 = mem.reshape(batch * N_MEM, D_MODEL)
    rope_tabs = _rope_tables(seq)
    for l in range(DEPTH):
        qa, ka, va, mk, mqt, mvt, mot, gp, rn = _in_proj(
            x, wts["w_rows"][l], wts["w_t"][l], wts["gb"][l], wts["conv_wk"][l], wts["conv_bk"][l],
            wts["conv_q"][l], rope_tabs, consts["tri_u"], consts["tri_l"], seq)
        att = _attention(qa, ka, va, wts["sink"][l], wts["att_g"][l], batch, seq)
        hft, hbt = _mlstm(mqt, mvt, mk, rn, gp, batch, seq)
        k_mem, v_mem = _kv_proj(mem2, wts["wkv"][l])
        x, info, rows = _mixer_out_xattn(
            x, att, hft, hbt, mot, wts["mlstm_g"][l], wts["w_out_a"][l], wts["w_out_m"][l], wts["ln1_g"][l],
            wts["ln1_b"][l], k_mem, v_mem, wts["wq"][l], wts["wo"][l], wts["ln2_g"][l], wts["ln2_b"][l],
            wts["router_wt"], wts["router_b"], consts["tri_strict"], consts["low_strict"], batch, seq)
        x = _moe(x, info, rows, consts, wts["w_gate"], wts["w_up"], wts["w_down"], l,
                 wts["ln3_g"][l], wts["ln3_b"][l])
    return x.reshape(batch, seq, D_MODEL)


def _hi_lo_rows(w):
    hi = w.astype(BF16)
    return jnp.concatenate([hi, (w - hi.astype(F32)).astype(BF16)], axis=0)


def _prepare_weights(w_in, gate_bias, conv_w, conv_b, attn_sink, attn_norm_g, mlstm_norm_g, w_out, ln1_g, ln1_b,
                     wq_mem, wkv_mem, wo_mem, ln2_g, ln2_b, router_w, router_bias, w_gate, w_up, w_down, ln3_g, ln3_b):
    row = lambda t: t.astype(F32).reshape(DEPTH, 1, t.shape[-1])
    nh = N_MLSTM_HEADS
    gate_order = jnp.array(list(range(0, nh)) + list(range(2 * nh, 3 * nh))
                           + list(range(nh, 2 * nh)) + list(range(3 * nh, 4 * nh)), jnp.int32)
    cq = jnp.concatenate([jnp.swapaxes(conv_w[:, :, :MLSTM_WIDTH], 1, 2), conv_b[:, :MLSTM_WIDTH, None],
                          jnp.zeros((DEPTH, MLSTM_WIDTH, 4), conv_w.dtype)], axis=2).astype(F32)
    feature_major = jnp.concatenate([w_in[:, :, OFF_MQ:OFF_MQ + MLSTM_WIDTH], w_in[:, :, OFF_MV:OFF_G],
                                     w_in[:, :, OFF_G:][:, :, gate_order]], axis=2)
    return {
        "w_rows": jnp.concatenate([w_in[:, :, :OFF_MQ], w_in[:, :, OFF_MQ + MLSTM_WIDTH:OFF_MV]],
                                  axis=2).astype(BF16),
        "w_t": jnp.swapaxes(feature_major, 1, 2).astype(BF16),
        "gb": gate_bias.astype(F32)[:, gate_order].reshape(DEPTH, N_GATE_COLS, 1),
        "conv_wk": conv_w[:, :, MLSTM_WIDTH:].astype(F32),
        "conv_bk": row(conv_b[:, MLSTM_WIDTH:]),
        "conv_q": cq,
        "sink": attn_sink.astype(F32),
        "att_g": row(attn_norm_g),
        "mlstm_g": mlstm_norm_g.astype(F32).reshape(DEPTH, MLSTM_WIDTH, 1),
        "w_out_a": w_out[:, :ATT_WIDTH].astype(BF16),
        "w_out_m": w_out[:, ATT_WIDTH:].astype(BF16),
        "ln1_g": row(ln1_g), "ln1_b": row(ln1_b),
        "wq": wq_mem.astype(BF16), "wkv": wkv_mem.astype(BF16), "wo": wo_mem.astype(BF16),
        "ln2_g": row(ln2_g), "ln2_b": row(ln2_b),
        "router_wt": _hi_lo_rows(router_w.astype(F32).T),
        "router_b": router_bias.astype(F32).reshape(N_EXPERTS, 1),
        "w_gate": w_gate.astype(BF16), "w_up": w_up.astype(BF16), "w_down": w_down.astype(BF16),
        "ln3_g": row(ln3_g), "ln3_b": row(ln3_b),
    }


def kernel(x_prompt, x_sample, mem_prompt, mem_sample, w_in, gate_bias, conv_w, conv_b, attn_sink, attn_norm_g, mlstm_norm_g, w_out, ln1_g, ln1_b, wq_mem, wkv_mem, wo_mem, ln2_g, ln2_b, router_w, router_bias, w_gate, w_up, w_down, ln3_g, ln3_b):
    wts = _prepare_weights(w_in, gate_bias, conv_w, conv_b, attn_sink, attn_norm_g, mlstm_norm_g, w_out, ln1_g, ln1_b,
                           wq_mem, wkv_mem, wo_mem, ln2_g, ln2_b, router_w, router_bias, w_gate, w_up, w_down,
                           ln3_g, ln3_b)
    consts = _constants()
    return (_trunk(x_prompt, mem_prompt, wts, consts), _trunk(x_sample, mem_sample, wts, consts))
```

```python
import functools
import math

import jax
import jax.numpy as jnp
from jax import lax
from jax.experimental import pallas as pl
from jax.experimental.pallas import tpu as pltpu

F32 = jnp.float32
BF16 = jnp.bfloat16

D_MODEL = 1024
DEPTH = 4
HEAD_DIM = 64
N_ATT_HEADS = 8
N_KV_HEADS = 2
ATT_WIDTH = N_ATT_HEADS * HEAD_DIM
KV_WIDTH = N_KV_HEADS * HEAD_DIM
BLOCK = 128
ROT_DIM = HEAD_DIM // 4
ROPE_THETA = 500000.0
MLSTM_WIDTH = D_MODEL - ATT_WIDTH
N_MLSTM_HEADS = 4
MLSTM_HEAD_DIM = MLSTM_WIDTH // N_MLSTM_HEADS
CHUNK = 128
OFF_AQ = 0
OFF_AK = OFF_AQ + ATT_WIDTH
OFF_AV = OFF_AK + KV_WIDTH
OFF_MQ = OFF_AV + KV_WIDTH
OFF_MV = OFF_MQ + 2 * MLSTM_WIDTH
OFF_MO = OFF_MV + MLSTM_WIDTH
OFF_G = OFF_MO + MLSTM_WIDTH
N_GATE_COLS = 4 * N_MLSTM_HEADS
N_MEM = 256
N_X_HEADS = 4
X_HEAD_DIM = D_MODEL // N_X_HEADS
N_EXPERTS = 16
N_GROUPS = 4
EXPERTS_PER_GROUP = N_EXPERTS // N_GROUPS
TOP_K = 2
D_FF_EXPERT = 512
ALPHA = (2.0 * DEPTH) ** 0.25
LN_EPS = 1e-5
RMS_EPS = 1e-6
NEG = -1e30

LANES = 128
TOKEN_TILE = 512
IN_PROJ_TILE = 1024
XATTN_TILE = 1024
ROW_BLOCK = 512
VMEM_LIMIT = 56 * 1024 * 1024


def _params(*sem):
    return pltpu.CompilerParams(dimension_semantics=sem, vmem_limit_bytes=VMEM_LIMIT)


def _layer_norm(z, g, b):
    mu = jnp.mean(z, axis=-1, keepdims=True)
    zc = z - mu
    var = jnp.mean(zc * zc, axis=-1, keepdims=True)
    return zc * lax.rsqrt(var + LN_EPS) * g + b


def _dot(a, b):
    return jnp.dot(a, b, preferred_element_type=F32)


def _dot_nt(a, b):
    return lax.dot_general(a, b, (((1,), (1,)), ((), ())), preferred_element_type=F32)


def _dot_tn(a, b):
    return lax.dot_general(a, b, (((0,), (0,)), ((), ())), preferred_element_type=F32)


_X_HALO = 8


_W_MK = OFF_MQ
_W_END = _W_MK + MLSTM_WIDTH
_GP_ROWS = 3 * 2 * N_MLSTM_HEADS


def _in_proj_kernel(x_ref, xp_ref, xn_ref, w_ref, wt_ref, gb_ref, cwk_ref, cbk_ref, cq_ref,
                    cos_ref, sa_ref, sb_ref, tri_u_ref, tri_l_ref,
                    qa_ref, ka_ref, va_ref, mk_ref, mqt_ref, mvt_ref, mot_ref, gp_ref, rn_ref, *, n_seq_tiles):
    xb = x_ref[...].astype(BF16)
    cos = cos_ref[...]
    sa = sa_ref[...]
    sb = sb_ref[...]

    def mm(lo, hi):
        return _dot(xb, w_ref[:, lo:hi])

    def rope(t):
        return t * cos + pltpu.roll(t, LANES - ROT_DIM // 2, 1) * sa + pltpu.roll(t, ROT_DIM // 2, 1) * sb

    q = mm(OFF_AQ, OFF_AK)
    scale = math.log2(math.e) / math.sqrt(HEAD_DIM)
    for j in range(ATT_WIDTH // LANES):
        qa_ref[:, j * LANES:(j + 1) * LANES] = (rope(q[:, j * LANES:(j + 1) * LANES]) * scale).astype(BF16)
    kv = mm(OFF_AK, OFF_MQ)
    ka_ref[...] = rope(kv[:, :KV_WIDTH]).astype(BF16)
    va_ref[...] = kv[:, KV_WIDTH:].astype(BF16)
    tm = xb.shape[0]
    pos = pl.program_id(0) % n_seq_tiles
    has_prev = pos > 0
    has_next = pos < n_seq_tiles - 1
    halo = jnp.concatenate([xp_ref[...], xn_ref[...]], axis=0).astype(BF16)

    def silu(y):
        return y * (1.0 / (1.0 + jnp.exp(-y)))

    u = mm(_W_MK, _W_END)
    uh = _dot(halo, w_ref[:, _W_MK:_W_END])
    rowi = lax.broadcasted_iota(jnp.int32, u.shape, 0)
    u_prev = jnp.where(rowi == 0, jnp.where(has_prev, uh[_X_HALO - 1:_X_HALO, :], 0.0), pltpu.roll(u, 1, 0))
    u_next = jnp.where(rowi == tm - 1, jnp.where(has_next, uh[_X_HALO:_X_HALO + 1, :], 0.0),
                       pltpu.roll(u, tm - 1, 0))
    yk = silu(cwk_ref[0:1, :] * u_prev + cwk_ref[1:2, :] * u + cwk_ref[2:3, :] * u_next + cbk_ref[...])
    mk_ref[...] = (yk * (MLSTM_HEAD_DIM ** -0.5)).astype(BF16)

    feat = _dot_nt(wt_ref[...], xb)
    ut = feat[:MLSTM_WIDTH]
    uht = _dot_nt(wt_ref[0:MLSTM_WIDTH, :], halo)
    lanei = lax.broadcasted_iota(jnp.int32, ut.shape, 1)
    ut_prev = jnp.where(lanei == 0, jnp.where(has_prev, uht[:, _X_HALO - 1:_X_HALO], 0.0), pltpu.roll(ut, 1, 1))
    ut_next = jnp.where(lanei == tm - 1, jnp.where(has_next, uht[:, _X_HALO:_X_HALO + 1], 0.0),
                        pltpu.roll(ut, tm - 1, 1))
    cq = cq_ref[...]
    mqt_ref[...] = silu(cq[:, 0:1] * ut_prev + cq[:, 1:2] * ut + cq[:, 2:3] * ut_next + cq[:, 3:4]).astype(BF16)
    mvt_ref[...] = feat[MLSTM_WIDTH:2 * MLSTM_WIDTH].astype(BF16)
    mot_ref[...] = feat[2 * MLSTM_WIDTH:3 * MLSTM_WIDTH].astype(BF16)

    gt = feat[3 * MLSTM_WIDTH:] + gb_ref[...]
    half = 2 * N_MLSTM_HEADS
    gi, gf = gt[:half], gt[half:]
    ls = jnp.minimum(gf, 0.0) - jnp.log1p(jnp.exp(-jnp.abs(gf)))
    n_chunks = tm // CHUNK
    stack = lambda t: jnp.concatenate([t[:, c * CHUNK:(c + 1) * CHUNK] for c in range(n_chunks)], axis=0)
    ls_rows, gi_rows = stack(ls), stack(gi)
    pre = jnp.dot(ls_rows, tri_u_ref[...], precision=lax.Precision.HIGHEST, preferred_element_type=F32)
    suf = jnp.dot(ls_rows, tri_l_ref[...], precision=lax.Precision.HIGHEST, preferred_element_type=F32)
    is_fwd = (lax.broadcasted_iota(jnp.int32, ls_rows.shape, 0) % half) < N_MLSTM_HEADS
    lane = lax.broadcasted_iota(jnp.int32, ls_rows.shape, 1)
    b = jnp.where(is_fwd, pre, suf)
    r = gi_rows - b
    cm_f, cm_b = r, r
    k = 1
    while k < CHUNK:
        cm_f = jnp.maximum(cm_f, jnp.where(lane >= k, pltpu.roll(cm_f, k, 1), -jnp.inf))
        cm_b = jnp.maximum(cm_b, jnp.where(lane < CHUNK - k, pltpu.roll(cm_b, CHUNK - k, 1), -jnp.inf))
        k *= 2
    cm = jnp.where(is_fwd, cm_f, cm_b)
    log2e = math.log2(math.e)
    b, cm, r = b * log2e, cm * log2e, r * log2e
    pad = jnp.zeros((CHUNK - half, CHUNK), F32)
    for c in range(n_chunks):
        rows = slice(c * half, (c + 1) * half)
        gp_ref[c] = jnp.concatenate([b[rows], cm[rows], r[rows]], axis=0)
        rn_ref[c * CHUNK:(c + 1) * CHUNK, :] = jnp.concatenate([r[rows], pad], axis=0).T


def _in_proj(x, w_rows, w_t, gb, cwk, cbk, cq, rope_tabs, tri_u, tri_l, seq):
    T = x.shape[0]
    tm = IN_PROJ_TILE
    n_seq_tiles = seq // tm
    halo_per_tile = tm // _X_HALO
    n_halo = T // _X_HALO
    cos_t, sa_t, sb_t = rope_tabs
    row_spec = lambda w: pl.BlockSpec((tm, w), lambda i: (i, 0))
    full = lambda a: pl.BlockSpec(a.shape, lambda i: (0,) * a.ndim)
    tab_spec = pl.BlockSpec((tm, LANES), lambda i: (i % n_seq_tiles, 0))
    prev_spec = pl.BlockSpec((_X_HALO, D_MODEL), lambda i: (jnp.maximum(i * halo_per_tile - 1, 0), 0))
    next_spec = pl.BlockSpec((_X_HALO, D_MODEL), lambda i: (jnp.minimum((i + 1) * halo_per_tile, n_halo - 1), 0))
    widths = (ATT_WIDTH, KV_WIDTH, KV_WIDTH, MLSTM_WIDTH)
    col_spec = _feature_major_spec(MLSTM_WIDTH, tm, n_seq_tiles)
    feat_major = jax.ShapeDtypeStruct((MLSTM_SEQS, MLSTM_WIDTH, T // MLSTM_SEQS), BF16)
    out_shapes = tuple(jax.ShapeDtypeStruct((T, w), BF16) for w in widths) + (
        feat_major, feat_major, feat_major,
        jax.ShapeDtypeStruct((T // CHUNK, _GP_ROWS, CHUNK), F32), jax.ShapeDtypeStruct((T, LANES), F32))
    out_specs = tuple(row_spec(w) for w in widths) + (
        col_spec, col_spec, col_spec,
        pl.BlockSpec((tm // CHUNK, _GP_ROWS, CHUNK), lambda i: (i, 0, 0)), row_spec(LANES))
    return pl.pallas_call(
        functools.partial(_in_proj_kernel, n_seq_tiles=n_seq_tiles),
        grid=(T // tm,),
        in_specs=[row_spec(D_MODEL), prev_spec, next_spec, full(w_rows), full(w_t), full(gb),
                  full(cwk), full(cbk), full(cq), tab_spec, tab_spec, tab_spec, full(tri_u), full(tri_l)],
        out_specs=out_specs,
        out_shape=out_shapes,
        compiler_params=_params("parallel"),
        name="in_proj",
    )(x, x, x, w_rows, w_t, gb, cwk, cbk, cq, cos_t, sa_t, sb_t, tri_u, tri_l)


ATT_Q_TILE = 1024
_Q_BLOCKS = ATT_Q_TILE // BLOCK


def _attn_kernel(sink_ref, q_ref, kp_ref, k_ref, kn_ref, vp_ref, v_ref, vn_ref, g_ref, o_ref, *, n_tiles):
    i = pl.program_id(1)
    lane = lax.broadcasted_iota(jnp.int32, (ATT_Q_TILE + 2 * BLOCK, LANES), 1)
    low = lane < HEAD_DIM

    def split(prev_ref, own_ref, next_ref):
        t = jnp.concatenate([prev_ref[...], own_ref[...], next_ref[...]], axis=0).astype(F32)
        r = pltpu.roll(t, HEAD_DIM, 1)
        zero = jnp.zeros_like(t)
        lo = (jnp.where(low, t, zero).astype(BF16), jnp.where(low, r, zero).astype(BF16))
        hi = (jnp.where(low, zero, r).astype(BF16), jnp.where(low, zero, t).astype(BF16))
        return lo, hi

    k_lo, k_hi = split(kp_ref, k_ref, kn_ref)
    v_lo, v_hi = split(vp_ref, v_ref, vn_ref)

    rowi = lax.broadcasted_iota(jnp.int32, (BLOCK, BLOCK), 0)
    coli = lax.broadcasted_iota(jnp.int32, (BLOCK, BLOCK), 1)
    prev_bias = jnp.where(coli < rowi, NEG, 0.0).astype(F32)
    next_bias = jnp.where(coli > rowi, NEG, 0.0).astype(F32)
    first_bias = jnp.where(i == 0, NEG, 0.0).astype(F32)
    last_bias = jnp.where(i == n_tiles - 1, NEG, 0.0).astype(F32)
    lane_o = lax.broadcasted_iota(jnp.int32, (BLOCK, LANES), 1)
    g = g_ref[...]
    log2e = math.log2(math.e)

    for r in range(_Q_BLOCKS):
        pb = prev_bias + first_bias if r == 0 else prev_bias
        nb = next_bias + last_bias if r == _Q_BLOCKS - 1 else next_bias
        rows = slice(r * BLOCK, (r + 1) * BLOCK)
        win = slice(r * BLOCK, (r + 3) * BLOCK)
        tiles = []
        for c in range(N_KV_HEADS):
            q2 = jnp.concatenate([q_ref[rows, (2 * c) * LANES:(2 * c + 1) * LANES],
                                  q_ref[rows, (2 * c + 1) * LANES:(2 * c + 2) * LANES]], axis=0)
            kc = jnp.concatenate([k_lo[c][win], k_hi[c][win]], axis=0)
            vc = jnp.concatenate([v_lo[c][win], v_hi[c][win]], axis=0)
            s = _dot_nt(q2, kc)
            p_rows, inv_rows = [], []
            for t in range(2):
                ps, invs = [], []
                for hh in range(2):
                    head = 4 * c + 2 * t + hh
                    sink = sink_ref[head] * log2e
                    blk = lambda j: s[t * BLOCK:(t + 1) * BLOCK, (3 * hh + j) * BLOCK:(3 * hh + j + 1) * BLOCK]
                    sp, so, sn = blk(0) + pb, blk(1), blk(2) + nb
                    m = jnp.maximum(jnp.max(jnp.maximum(jnp.maximum(sp, so), sn), axis=-1, keepdims=True), sink)
                    pp, po, pn = jnp.exp2(sp - m), jnp.exp2(so - m), jnp.exp2(sn - m)
                    den = jnp.sum(pp + po + pn, axis=-1, keepdims=True) + jnp.exp2(sink - m)
                    ps += [pp.astype(BF16), po.astype(BF16), pn.astype(BF16)]
                    invs.append(1.0 / den)
                p_rows.append(jnp.concatenate(ps, axis=1))
                inv_rows.append(jnp.where(lane_o < HEAD_DIM, invs[0], invs[1]))
            o2 = _dot(jnp.concatenate(p_rows, axis=0), vc)
            tiles.append(o2[:BLOCK] * inv_rows[0])
            tiles.append(o2[BLOCK:] * inv_rows[1])
        o = jnp.concatenate(tiles, axis=1)
        ms = jnp.mean(o * o, axis=-1, keepdims=True)
        o_ref[rows, :] = (o * lax.rsqrt(ms + RMS_EPS) * g).astype(BF16)


def _attention(qa, ka, va, sink, att_g, batch, seq):
    T = qa.shape[0]
    n_tiles = seq // ATT_Q_TILE
    blocks_per_seq = seq // BLOCK
    own = lambda w: pl.BlockSpec((ATT_Q_TILE, w), lambda b, i: (b * n_tiles + i, 0))
    prev = pl.BlockSpec((BLOCK, KV_WIDTH),
                        lambda b, i: (b * blocks_per_seq + jnp.maximum(i * _Q_BLOCKS - 1, 0), 0))
    nxt = pl.BlockSpec((BLOCK, KV_WIDTH),
                       lambda b, i: (b * blocks_per_seq + jnp.minimum((i + 1) * _Q_BLOCKS, blocks_per_seq - 1), 0))
    return pl.pallas_call(
        functools.partial(_attn_kernel, n_tiles=n_tiles),
        grid=(batch, n_tiles),
        in_specs=[pl.BlockSpec(memory_space=pltpu.SMEM), own(ATT_WIDTH), prev, own(KV_WIDTH), nxt,
                  prev, own(KV_WIDTH), nxt, pl.BlockSpec((1, ATT_WIDTH), lambda b, i: (0, 0))],
        out_specs=own(ATT_WIDTH),
        out_shape=jax.ShapeDtypeStruct((T, ATT_WIDTH), BF16),
        compiler_params=_params("parallel", "parallel"),
        name="band_attention",
    )(sink, qa, ka, ka, ka, va, va, va, att_g)


MLSTM_SEQS = 4
_STATE_ROWS = MLSTM_HEAD_DIM + 8


def _mlstm_kernel(qt_f, vt_f, k_f, rn_f, gp_f, qt_b, vt_b, k_b, rn_b, gp_b, of_ref, ob_ref, c_state, m_state):
    @pl.when(pl.program_id(1) == 0)
    def _():
        c_state[...] = jnp.zeros_like(c_state)
        m_state[...] = jnp.zeros_like(m_state)

    key = lax.broadcasted_iota(jnp.int32, (CHUNK, CHUNK), 0)
    qry = lax.broadcasted_iota(jnp.int32, (CHUNK, CHUNK), 1)
    nh = N_MLSTM_HEADS

    units = []
    for sq in range(MLSTM_SEQS):
        for fwd, qt_ref, vt_ref, k_ref, rn_ref, gp_ref, o_ref in ((True, qt_f, vt_f, k_f, rn_f, gp_f, of_ref),
                                                                 (False, qt_b, vt_b, k_b, rn_b, gp_b, ob_ref)):
            gates = gp_ref[sq, 0]
            off = 0 if fwd else nh
            a_pos = CHUNK - 1 if fwd else 0
            for h in range(nh):
                hs = slice(h * MLSTM_HEAD_DIM, (h + 1) * MLSTM_HEAD_DIM)
                b = gates[off + h:off + h + 1, :]
                st = (2 * sq + (0 if fwd else 1)) * nh + h
                units.append(dict(
                    sq=sq, hs=hs, st=st, o_ref=o_ref, qt_ref=qt_ref, vt_ref=vt_ref, k_ref=k_ref,
                    visible=(key <= qry) if fwd else (key >= qry),
                    b=b, cm=gates[2 * nh + off + h:2 * nh + off + h + 1, :],
                    r_row=gates[4 * nh + off + h:4 * nh + off + h + 1, :],
                    r_keys=jnp.broadcast_to(rn_ref[sq, :, off + h:off + h + 1], (CHUNK, CHUNK)),
                    a=b[:, a_pos:a_pos + 1],
                    m_in=m_state[st:st + 1, 0:1]))

    for u in units:
        inter_log = u["b"] + u["m_in"]
        m_t = jnp.maximum(inter_log, u["b"] + u["cm"])
        u["decay"] = jnp.exp2(jnp.where(u["visible"], u["r_keys"] + (u["b"] - m_t), NEG))
        u["inter_w"] = jnp.exp2(inter_log - m_t)
        u["floor"] = jnp.exp2(-m_t)
    for u in units:
        qt = u["qt_ref"][u["sq"], u["hs"], :]
        u["k"] = u["k_ref"][u["sq"], :, u["hs"]]
        u["vt"] = u["vt_ref"][u["sq"], u["hs"], :]
        u["scores"] = _dot(u["k"], qt)
        u["c_in"] = c_state[u["st"]]
        u["inter"] = _dot(u["c_in"].astype(BF16), qt)
    for u in units:
        sw = u["scores"] * u["decay"]
        num = _dot(u["vt"], sw.astype(BF16)) + u["inter_w"] * u["inter"][:MLSTM_HEAD_DIM]
        den = (jnp.sum(sw, axis=0, keepdims=True)
               + u["inter_w"] * u["inter"][MLSTM_HEAD_DIM:MLSTM_HEAD_DIM + 1])
        u["o_ref"][u["sq"], u["hs"], :] = (num / jnp.maximum(jnp.abs(den), u["floor"])).astype(u["o_ref"].dtype)
    for u in units:
        a, m_in, st = u["a"], u["m_in"], u["st"]
        g_max = a + jnp.max(u["r_row"], axis=-1, keepdims=True)
        kw = u["k"].astype(F32) * jnp.exp2(u["r_keys"] + (a - g_max))
        m_new = jnp.maximum(a + m_in, g_max)
        keep = jnp.exp2(a + m_in - m_new)
        add = jnp.exp2(g_max - m_new)
        c_state[st, 0:MLSTM_HEAD_DIM, :] = keep * u["c_in"][:MLSTM_HEAD_DIM] + add * _dot(u["vt"], kw.astype(BF16))
        c_state[st, MLSTM_HEAD_DIM:MLSTM_HEAD_DIM + 1, :] = (
            keep * u["c_in"][MLSTM_HEAD_DIM:MLSTM_HEAD_DIM + 1] + add * jnp.sum(kw, axis=0, keepdims=True))
        m_state[st:st + 1, :] = jnp.broadcast_to(m_new, (1, LANES))


def _feature_major_spec(width, tm, tiles_per_seq):
    def index(i):
        s = i // tiles_per_seq
        return (s % MLSTM_SEQS, 0, (s // MLSTM_SEQS) * tiles_per_seq + i % tiles_per_seq)
    return pl.BlockSpec((None, width, tm), index)


def _mlstm(mqt, mvt, mk, rn, gp, batch, seq):
    T = mk.shape[0]
    nc = seq // CHUNK
    groups = batch // MLSTM_SEQS
    mk = mk.reshape(groups, MLSTM_SEQS, seq, MLSTM_WIDTH)
    rn = rn.reshape(groups, MLSTM_SEQS, seq, LANES)
    gp = gp.reshape(groups, MLSTM_SEQS, nc, _GP_ROWS, CHUNK)

    def specs(chunk_of):
        feat = pl.BlockSpec((MLSTM_SEQS, MLSTM_WIDTH, CHUNK), lambda b, i: (0, 0, b * nc + chunk_of(i)))
        return [feat, feat,
                pl.BlockSpec((None, MLSTM_SEQS, CHUNK, MLSTM_WIDTH), lambda b, i: (b, 0, chunk_of(i), 0)),
                pl.BlockSpec((None, MLSTM_SEQS, CHUNK, LANES), lambda b, i: (b, 0, chunk_of(i), 0)),
                pl.BlockSpec((None, MLSTM_SEQS, 1, _GP_ROWS, CHUNK), lambda b, i: (b, 0, chunk_of(i), 0, 0))]

    fwd_chunk = lambda i: i
    bwd_chunk = lambda i: nc - 1 - i
    out = jax.ShapeDtypeStruct((MLSTM_SEQS, MLSTM_WIDTH, T // MLSTM_SEQS), BF16)
    n_state = 2 * MLSTM_SEQS * N_MLSTM_HEADS
    return pl.pallas_call(
        _mlstm_kernel,
        grid=(groups, nc),
        in_specs=specs(fwd_chunk) + specs(bwd_chunk),
        out_specs=(specs(fwd_chunk)[0], specs(bwd_chunk)[0]),
        out_shape=(out, out),
        scratch_shapes=[pltpu.VMEM((n_state, _STATE_ROWS, MLSTM_HEAD_DIM), F32),
                        pltpu.VMEM((n_state, LANES), F32)],
        compiler_params=_params("parallel", "arbitrary"),
        name="mlstm",
    )(mqt, mvt, mk, rn, gp, mqt, mvt, mk, rn, gp)


def _mixer_out(x, att_ref, hf_ref, hb_ref, mo_ref, mg_ref, wa_ref, wm_ref, g_ref, b_ref):
    h = hf_ref[...].astype(F32) + hb_ref[...].astype(F32)
    parts = []
    for hd in range(N_MLSTM_HEADS):
        hh = h[hd * MLSTM_HEAD_DIM:(hd + 1) * MLSTM_HEAD_DIM]
        ms = jnp.mean(hh * hh, axis=0, keepdims=True)
        parts.append(hh * lax.rsqrt(ms + RMS_EPS))
    hn = jnp.concatenate(parts, axis=0) * mg_ref[...]
    gate = 1.0 / (1.0 + jnp.exp(-mo_ref[...].astype(F32)))
    y = _dot(att_ref[...], wa_ref[...]) + _dot_tn((hn * gate).astype(BF16), wm_ref[...])
    return _layer_norm(ALPHA * x + y, g_ref[...], b_ref[...])


def _kv_proj_kernel(m_ref, w_ref, k_ref, v_ref):
    mb = m_ref[...].astype(BF16)
    k_ref[...] = _dot(mb, w_ref[:, :D_MODEL]).astype(BF16)
    v_ref[...] = _dot(mb, w_ref[:, D_MODEL:]).astype(BF16)


def _kv_proj(mem, wkv):
    M = mem.shape[0]
    tm = TOKEN_TILE
    row = pl.BlockSpec((tm, D_MODEL), lambda i: (i, 0))
    out = jax.ShapeDtypeStruct((M, D_MODEL), BF16)
    return pl.pallas_call(
        _kv_proj_kernel,
        grid=(M // tm,),
        in_specs=[row, pl.BlockSpec(wkv.shape, lambda i: (0, 0))],
        out_specs=(row, row),
        out_shape=(out, out),
        compiler_params=_params("parallel"),
        name="mem_kv_proj",
    )(mem, wkv)


def _xattn_kernel(x_ref, att_ref, hf_ref, hb_ref, mo_ref, mg_ref, wa_ref, wm_ref, g1_ref, b1_ref,
                  k_ref, v_ref, wq_ref, wo_ref, g_ref, b_ref, rwt_ref, rb_ref, tri_ref, low_ref,
                  o_ref, info_ref, cnt_ref):
    subs = [slice(t * TOKEN_TILE, (t + 1) * TOKEN_TILE) for t in range(x_ref.shape[0] // TOKEN_TILE)]
    xs = [_mixer_out(x_ref[rows], att_ref.at[rows], hf_ref.at[:, rows], hb_ref.at[:, rows], mo_ref.at[:, rows],
                     mg_ref, wa_ref, wm_ref, g1_ref, b1_ref) for rows in subs]
    scale = math.log2(math.e) / math.sqrt(X_HEAD_DIM)
    qs = [_dot(x.astype(BF16), wq_ref[...]) for x in xs]
    outs = [[] for _ in subs]
    for h in range(N_X_HEADS):
        hs = slice(h * X_HEAD_DIM, (h + 1) * X_HEAD_DIM)
        for t, q in enumerate(qs):
            s = _dot_nt((q[:, hs] * scale).astype(BF16), k_ref[:, hs])
            p = jnp.exp2(s - jnp.max(s, axis=-1, keepdims=True))
            inv = 1.0 / jnp.sum(p, axis=-1, keepdims=True)
            outs[t].append((_dot(p.astype(BF16), v_ref[:, hs]) * inv).astype(BF16))
    ys = [_dot(jnp.concatenate(o, axis=1), wo_ref[...]) for o in outs]
    for t, rows in enumerate(subs):
        x2 = _layer_norm(ALPHA * xs[t] + ys[t], g_ref[...], b_ref[...])
        o_ref[rows, :] = x2
        _route(x2, rwt_ref, rb_ref, tri_ref, low_ref, info_ref.at[:, rows], cnt_ref.at[t:t + 1])


def _mixer_out_xattn(x, att, hft, hbt, mot, mg, wa, wm, g1, b1, k_mem, v_mem, wq, wo, g, b,
                     router_wt, router_b, tri_strict, low_strict, batch, seq):
    T = x.shape[0]
    tm = XATTN_TILE
    nt = seq // tm
    row = pl.BlockSpec((tm, D_MODEL), lambda bb, i: (bb * nt + i, 0))
    att_row = pl.BlockSpec((tm, ATT_WIDTH), lambda bb, i: (bb * nt + i, 0))
    flat_feat = _feature_major_spec(MLSTM_WIDTH, tm, nt)
    feat = pl.BlockSpec(flat_feat.block_shape, lambda bb, i: flat_feat.index_map(bb * nt + i))
    mem = pl.BlockSpec((N_MEM, D_MODEL), lambda bb, i: (bb, 0))
    full = lambda a: pl.BlockSpec(a.shape, lambda bb, i: (0,) * a.ndim)
    return pl.pallas_call(
        _xattn_kernel,
        grid=(batch, nt),
        in_specs=[row, att_row, feat, feat, feat, full(mg), full(wa), full(wm), full(g1), full(b1),
                  mem, mem, full(wq), full(wo), full(g), full(b), full(router_wt), full(router_b),
                  full(tri_strict), full(low_strict)],
        out_specs=(row, pl.BlockSpec((_IDX_ROWS, tm), lambda bb, i: (0, bb * nt + i)),
                   pl.BlockSpec((tm // TOKEN_TILE, N_EXPERTS, LANES), lambda bb, i: (bb * nt + i, 0, 0))),
        out_shape=(jax.ShapeDtypeStruct((T, D_MODEL), F32), jax.ShapeDtypeStruct((_IDX_ROWS, T), F32),
                   jax.ShapeDtypeStruct((T // TOKEN_TILE, N_EXPERTS, LANES), jnp.int32)),
        compiler_params=_params("parallel", "parallel"),
        name="mixer_out_xattn_route",
    )(x, att, hft, hbt, mot, mg, wa, wm, g1, b1, k_mem, v_mem, wq, wo, g, b, router_wt, router_b,
      tri_strict, low_strict)


_IDX_ROWS = 8
SLAB = 16
TILE_SLOTS = TOP_K * TOKEN_TILE + N_EXPERTS * SLAB


def _route(x, wt_ref, b_ref, tri_ref, low_ref, info_ref, cnt_ref):
    x_hi = x.astype(BF16)
    x_lo = (x - x_hi.astype(F32)).astype(BF16)
    by_hi = _dot_nt(wt_ref[...], x_hi)
    logits = by_hi[:N_EXPERTS] + by_hi[N_EXPERTS:] + _dot_nt(wt_ref[0:N_EXPERTS, :], x_lo)
    s = 1.0 / (1.0 + jnp.exp(-logits))
    sel = s + b_ref[...]
    srow = lambda e: s[e:e + 1, :]
    brow = lambda e: sel[e:e + 1, :]
    best = None
    gi = None
    for gidx in range(N_GROUPS):
        vals = [brow(gidx * EXPERTS_PER_GROUP + j) for j in range(EXPERTS_PER_GROUP)]
        top2 = None
        for a in range(EXPERTS_PER_GROUP):
            for b in range(a + 1, EXPERTS_PER_GROUP):
                pair = vals[a] + vals[b]
                top2 = pair if top2 is None else jnp.maximum(top2, pair)
        if best is None:
            best, gi = top2, jnp.zeros(top2.shape, jnp.int32)
        else:
            better = top2 > best
            gi = jnp.where(better, gidx, gi)
            best = jnp.where(better, top2, best)

    def in_group(rowfn, j):
        out = rowfn(j)
        for gidx in range(1, N_GROUPS):
            out = jnp.where(gi == gidx, rowfn(gidx * EXPERTS_PER_GROUP + j), out)
        return out

    bv = [in_group(brow, j) for j in range(EXPERTS_PER_GROUP)]
    sv = [in_group(srow, j) for j in range(EXPERTS_PER_GROUP)]

    def argmax_first(vals):
        bi = jnp.zeros(vals[0].shape, jnp.int32)
        bm = vals[0]
        for j in range(1, len(vals)):
            better = vals[j] > bm
            bi = jnp.where(better, j, bi)
            bm = jnp.where(better, vals[j], bm)
        return bi

    i1 = argmax_first(bv)
    i2 = argmax_first([jnp.where(i1 == j, -jnp.inf, bv[j]) for j in range(EXPERTS_PER_GROUP)])

    def pick(vals, idx):
        out = vals[0]
        for j in range(1, len(vals)):
            out = jnp.where(idx == j, vals[j], out)
        return out

    w1 = pick(sv, i1)
    w2 = pick(sv, i2)
    tot = w1 + w2
    tm = logits.shape[1]
    eid = lax.broadcasted_iota(jnp.int32, (N_EXPERTS, tm), 0)
    oh1 = (eid == gi * EXPERTS_PER_GROUP + i1).astype(F32)
    oh2 = (eid == gi * EXPERTS_PER_GROUP + i2).astype(F32)
    before1 = _dot(oh1.astype(BF16), tri_ref[...])
    before2 = _dot(oh2.astype(BF16), tri_ref[...])
    c1 = jnp.sum(oh1, axis=1, keepdims=True)
    cnt = c1 + jnp.sum(oh2, axis=1, keepdims=True)
    rows = jnp.floor((cnt + (SLAB - 1)) * (1.0 / SLAB)) * SLAB
    rows_b = jnp.broadcast_to(rows, (N_EXPERTS, LANES))
    start = jnp.dot(low_ref[...], rows_b, precision=lax.Precision.HIGHEST, preferred_element_type=F32)[:, 0:1]
    slot1 = jnp.sum(oh1 * (start + before1), axis=0, keepdims=True)
    slot2 = jnp.sum(oh2 * (start + c1 + before2), axis=0, keepdims=True)
    zf = jnp.zeros((_IDX_ROWS - 2 * TOP_K, tm), F32)
    info_ref[...] = jnp.concatenate([w1 / tot, w2 / tot, slot1, slot2, zf], axis=0)
    cnt_ref[0] = rows_b.astype(jnp.int32)


_BIG_SLAB = 4 * SLAB


def _slab_copies(i, gs_ref, ls_ref, rc_ref, make_copy, op):
    for e in range(N_EXPERTS):
        idx = i * N_EXPERTS + e
        ls = ls_ref[idx]
        gs = gs_ref[idx]
        n_big = rc_ref[idx] // _BIG_SLAB
        done = n_big * _BIG_SLAB

        def big(j, carry, ls=ls, gs=gs):
            op(make_copy(pl.multiple_of(ls + j * _BIG_SLAB, SLAB), pl.multiple_of(gs + j * _BIG_SLAB, SLAB),
                         _BIG_SLAB))
            return carry

        def small(j, carry, ls=ls + done, gs=gs + done):
            op(make_copy(pl.multiple_of(ls + j * SLAB, SLAB), pl.multiple_of(gs + j * SLAB, SLAB), SLAB))
            return carry

        lax.fori_loop(0, n_big, big, 0)
        lax.fori_loop(0, (rc_ref[idx] - done) // SLAB, small, 0)


_ZERO_ROWS = 128
_COMBINE_SUBTILES = 1


def _dispatch_kernel(gs_ref, ls_ref, rc_ref, ts_ref, tn_ref, x_ref, info_ref, xp_ref, xs_ref, zero_ref, sem, zsem):
    i = pl.program_id(0)
    last = pl.num_programs(0) - 1
    buf = i % 2
    slot0 = info_ref[2:3, :].astype(jnp.int32)
    slot1 = info_ref[3:4, :].astype(jnp.int32)
    xb = x_ref[...].astype(BF16)
    tm = xb.shape[0]
    sid = lax.broadcasted_iota(jnp.int32, (TILE_SLOTS, tm), 0)
    sel = jnp.where((sid == slot0) | (sid == slot1), 1.0, 0.0).astype(BF16)
    xs_ref[buf] = _dot(sel, xb).astype(BF16)

    def copies(tile, which, op):
        def make_copy(tile_row, global_row, rows):
            return pltpu.make_async_copy(xs_ref.at[which, pl.ds(tile_row, rows), :],
                                         xp_ref.at[pl.ds(global_row, rows), :], sem.at[which])
        _slab_copies(tile, gs_ref, ls_ref, rc_ref, make_copy, op)

    @pl.when(i > 0)
    def _():
        copies(i - 1, 1 - buf, lambda cp: cp.wait())

    copies(i, buf, lambda cp: cp.start())

    @pl.when(i == last)
    def _():
        zero_ref[...] = jnp.zeros_like(zero_ref)

        def tail(op):
            for e in range(N_EXPERTS + 1):
                t0 = ts_ref[e]
                rows = SLAB if e < N_EXPERTS else _ZERO_ROWS

                def body(j, carry, t0=t0, rows=rows):
                    op(pltpu.make_async_copy(zero_ref.at[pl.ds(0, rows), :],
                                             xp_ref.at[pl.ds(pl.multiple_of(t0 + j * rows, rows), rows), :], zsem))
                    return carry

                lax.fori_loop(0, tn_ref[e] // rows, body, 0)

        tail(lambda cp: cp.start())
        copies(i, buf, lambda cp: cp.wait())
        tail(lambda cp: cp.wait())


def _dispatch(x, info, gstart, lstart, rc, tail_start, tail_rows, n_rows):
    T = x.shape[0]
    tm = TOKEN_TILE
    grid_spec = pltpu.PrefetchScalarGridSpec(
        num_scalar_prefetch=5,
        grid=(T // tm,),
        in_specs=[pl.BlockSpec((tm, D_MODEL), lambda i, *_: (i, 0)),
                  pl.BlockSpec((_IDX_ROWS, tm), lambda i, *_: (0, i))],
        out_specs=pl.BlockSpec(memory_space=pl.ANY),
        scratch_shapes=[pltpu.VMEM((2, TILE_SLOTS, D_MODEL), BF16), pltpu.VMEM((_ZERO_ROWS, D_MODEL), BF16),
                        pltpu.SemaphoreType.DMA((2,)), pltpu.SemaphoreType.DMA],
    )
    return pl.pallas_call(
        _dispatch_kernel,
        grid_spec=grid_spec,
        out_shape=jax.ShapeDtypeStruct((n_rows, D_MODEL), BF16),
        compiler_params=_params("arbitrary"),
        name="moe_dispatch",
    )(gstart, lstart, rc, tail_start, tail_rows, x, info)


def _ffn_kernel(be_ref, nu_ref, x_ref, wg_ref, wu_ref, wd_ref, o_ref):
    i = pl.program_id(0)

    @pl.when(i < nu_ref[0])
    def _():
        xb = x_ref[...]
        gate = _dot(xb, wg_ref[...])
        up = _dot(xb, wu_ref[...])
        h = gate * (1.0 / (1.0 + jnp.exp(-gate))) * up
        o_ref[...] = _dot(h.astype(BF16), wd_ref[...]).astype(BF16)

    @pl.when(i >= nu_ref[0])
    def _():
        o_ref[...] = jnp.zeros_like(o_ref)


def _expert_ffn(xp, blk_e, n_used, w_gate, w_up, w_down, layer):
    P = xp.shape[0]
    wspec = lambda shp: pl.BlockSpec((None, None) + shp, lambda i, be, nu: (layer, be[i], 0, 0))
    grid_spec = pltpu.PrefetchScalarGridSpec(
        num_scalar_prefetch=2,
        grid=(P // ROW_BLOCK,),
        in_specs=[pl.BlockSpec((ROW_BLOCK, D_MODEL), lambda i, be, nu: (jnp.minimum(i, nu[0] - 1), 0)),
                  wspec((D_MODEL, D_FF_EXPERT)), wspec((D_MODEL, D_FF_EXPERT)), wspec((D_FF_EXPERT, D_MODEL))],
        out_specs=pl.BlockSpec((ROW_BLOCK, D_MODEL), lambda i, be, nu: (i, 0)),
    )
    return pl.pallas_call(
        _ffn_kernel,
        grid_spec=grid_spec,
        out_shape=jax.ShapeDtypeStruct((P, D_MODEL), BF16),
        compiler_params=_params("arbitrary"),
        name="expert_ffn",
    )(blk_e, n_used, xp, w_gate, w_up, w_down)


def _combine_kernel(gs_ref, ls_ref, rc_ref, x_ref, info_ref, eye_ref, g_ref, b_ref, yp_ref, o_ref, ys_ref, sem):
    i = pl.program_id(0)
    buf = i % 2

    def copies(tile, which, op):
        def make_copy(tile_row, global_row, rows):
            return pltpu.make_async_copy(yp_ref.at[pl.ds(global_row, rows), :],
                                         ys_ref.at[which, pl.ds(tile_row, rows), :], sem.at[which])
        _slab_copies(tile, gs_ref, ls_ref, rc_ref, make_copy, op)

    @pl.when(i == 0)
    def _():
        ys_ref[...] = jnp.zeros_like(ys_ref)
        copies(i, buf, lambda cp: cp.start())

    @pl.when(i + 1 < pl.num_programs(0))
    def _():
        copies(i + 1, 1 - buf, lambda cp: cp.start())

    cols = lax.dot_general(info_ref[...], eye_ref[...], (((0,), (0,)), ((), ())),
                           precision=lax.Precision.HIGHEST, preferred_element_type=F32)
    tm = cols.shape[0]
    sub = tm // _COMBINE_SUBTILES
    sid = lax.broadcasted_iota(jnp.int32, (sub, TILE_SLOTS), 1)
    weights = []
    for t in range(_COMBINE_SUBTILES):
        c = cols[t * sub:(t + 1) * sub]
        weights.append((jnp.where(sid == c[:, 2:3].astype(jnp.int32), c[:, 0:1], 0.0)
                        + jnp.where(sid == c[:, 3:4].astype(jnp.int32), c[:, 1:2], 0.0)).astype(BF16))
    copies(i, buf, lambda cp: cp.wait())
    ys = [_dot(w, ys_ref[buf]) for w in weights]
    for t in range(_COMBINE_SUBTILES):
        rows = slice(t * sub, (t + 1) * sub)
        o_ref[rows, :] = _layer_norm(ALPHA * x_ref[rows, :] + ys[t], g_ref[...], b_ref[...])


def _combine_ln(x, yp, info, gstart, lstart, rc, eye, g, b):
    T = x.shape[0]
    tm = TOKEN_TILE
    full = lambda a: pl.BlockSpec(a.shape, lambda i, *_: (0,) * a.ndim)
    grid_spec = pltpu.PrefetchScalarGridSpec(
        num_scalar_prefetch=3,
        grid=(T // tm,),
        in_specs=[pl.BlockSpec((tm, D_MODEL), lambda i, *_: (i, 0)),
                  pl.BlockSpec((_IDX_ROWS, tm), lambda i, *_: (0, i)), full(eye), full(g), full(b),
                  pl.BlockSpec(memory_space=pl.ANY)],
        out_specs=pl.BlockSpec((tm, D_MODEL), lambda i, *_: (i, 0)),
        scratch_shapes=[pltpu.VMEM((2, TILE_SLOTS, D_MODEL), BF16), pltpu.SemaphoreType.DMA((2,))],
    )
    return pl.pallas_call(
        _combine_kernel,
        grid_spec=grid_spec,
        out_shape=jax.ShapeDtypeStruct((T, D_MODEL), F32),
        compiler_params=_params("arbitrary"),
        name="moe_combine_ln3",
    )(gstart, lstart, rc, x, info, eye, g, b, yp)


def _moe(x, info, rows, consts, w_gate, w_up, w_down, layer, g, b):
    T = x.shape[0]
    n_tiles = T // TOKEN_TILE
    P = T * TOP_K + n_tiles * N_EXPERTS * SLAB + N_EXPERTS * ROW_BLOCK
    rc = rows[:, :, 0]
    lstart = jnp.cumsum(rc, axis=1) - rc
    region = (jnp.sum(rc, axis=0) + ROW_BLOCK - 1) // ROW_BLOCK * ROW_BLOCK
    region_end = jnp.cumsum(region)
    gstart = (region_end - region)[None, :] + jnp.cumsum(rc, axis=0) - rc
    n_blk = P // ROW_BLOCK
    blk_row0 = jnp.arange(n_blk, dtype=jnp.int32) * ROW_BLOCK
    blk_e = jnp.minimum(jnp.sum(region_end[None, :] <= blk_row0[:, None], axis=1), N_EXPERTS - 1).astype(jnp.int32)
    n_used = (region_end[-1:] // ROW_BLOCK).astype(jnp.int32)
    flat = lambda t: t.reshape(-1).astype(jnp.int32)
    used = jnp.sum(rc, axis=0)
    tail_start = flat(jnp.concatenate([region_end - region + used, region_end[-1:]]))
    tail_rows = flat(jnp.concatenate([region - used, P - region_end[-1:]]))
    gstart, lstart, rc = flat(gstart), flat(lstart), flat(rc)
    xp = _dispatch(x, info, gstart, lstart, rc, tail_start, tail_rows, P)
    yp = _expert_ffn(xp, blk_e, n_used, w_gate, w_up, w_down, layer)
    return _combine_ln(x, yp, info, gstart, lstart, rc, consts["eye"], g, b)


def _rope_tables(seq):
    inv_freq = ROPE_THETA ** (-jnp.arange(0, ROT_DIM, 2, dtype=F32) / ROT_DIM)
    ang = jnp.arange(seq, dtype=F32)[:, None] * inv_freq[None, :]
    cos, sin = jnp.cos(ang), jnp.sin(ang)
    half = ROT_DIM // 2
    one = jnp.ones((seq, HEAD_DIM - ROT_DIM), F32)
    zero = jnp.zeros((seq, HEAD_DIM - ROT_DIM), F32)
    zh = jnp.zeros((seq, half), F32)
    cos_h = jnp.concatenate([cos, cos, one], axis=1)
    sa_h = jnp.concatenate([-sin, zh, zero], axis=1)
    sb_h = jnp.concatenate([zh, sin, zero], axis=1)
    rep = lambda t: jnp.concatenate([t] * (LANES // HEAD_DIM), axis=1)
    return rep(cos_h), rep(sa_h), rep(sb_h)


def _constants():
    r = lax.broadcasted_iota(jnp.int32, (CHUNK, CHUNK), 0)
    c = lax.broadcasted_iota(jnp.int32, (CHUNK, CHUNK), 1)
    rr = lax.broadcasted_iota(jnp.int32, (TOKEN_TILE, TOKEN_TILE), 0)
    cc = lax.broadcasted_iota(jnp.int32, (TOKEN_TILE, TOKEN_TILE), 1)
    er = lax.broadcasted_iota(jnp.int32, (N_EXPERTS, N_EXPERTS), 0)
    ec = lax.broadcasted_iota(jnp.int32, (N_EXPERTS, N_EXPERTS), 1)
    return {
        "tri_u": (r <= c).astype(F32),
        "tri_l": (r >= c).astype(F32),
        "tri_strict": (rr < cc).astype(BF16),
        "low_strict": (ec < er).astype(F32),
        "eye": (lax.broadcasted_iota(jnp.int32, (_IDX_ROWS, LANES), 0)
                == lax.broadcasted_iota(jnp.int32, (_IDX_ROWS, LANES), 1)).astype(F32),
    }


def _trunk(x, mem, wts, consts):
    batch, seq, _ = x.shape
    T = batch * seq
    x = x.reshape(T, D_MODEL)
    mem2 = mem.reshape(batch * N_MEM, D_MODEL)
    rope_tabs = _rope_tables(seq)
    for l in range(DEPTH):
        qa, ka, va, mk, mqt, mvt, mot, gp, rn = _in_proj(
            x, wts["w_rows"][l], wts["w_t"][l], wts["gb"][l], wts["conv_wk"][l], wts["conv_bk"][l],
            wts["conv_q"][l], rope_tabs, consts["tri_u"], consts["tri_l"], seq)
        att = _attention(qa, ka, va, wts["sink"][l], wts["att_g"][l], batch, seq)
        hft, hbt = _mlstm(mqt, mvt, mk, rn, gp, batch, seq)
        k_mem, v_mem = _kv_proj(mem2, wts["wkv"][l])
        x, info, rows = _mixer_out_xattn(
            x, att, hft, hbt, mot, wts["mlstm_g"][l], wts["w_out_a"][l], wts["w_out_m"][l], wts["ln1_g"][l],
            wts["ln1_b"][l], k_mem, v_mem, wts["wq"][l], wts["wo"][l], wts["ln2_g"][l], wts["ln2_b"][l],
            wts["router_wt"], wts["router_b"], consts["tri_strict"], consts["low_strict"], batch, seq)
        x = _moe(x, info, rows, consts, wts["w_gate"], wts["w_up"], wts["w_down"], l,
                 wts["ln3_g"][l], wts["ln3_b"][l])
    return x.reshape(batch, seq, D_MODEL)


def _hi_lo_rows(w):
    hi = w.astype(BF16)
    return jnp.concatenate([hi, (w - hi.astype(F32)).astype(BF16)], axis=0)


def _prepare_weights(w_in, gate_bias, conv_w, conv_b, attn_sink, attn_norm_g, mlstm_norm_g, w_out, ln1_g, ln1_b,
                     wq_mem, wkv_mem, wo_mem, ln2_g, ln2_b, router_w, router_bias, w_gate, w_up, w_down, ln3_g, ln3_b):
    row = lambda t: t.astype(F32).reshape(DEPTH, 1, t.shape[-1])
    nh = N_MLSTM_HEADS
    gate_order = jnp.array(list(range(0, nh)) + list(range(2 * nh, 3 * nh))
                           + list(range(nh, 2 * nh)) + list(range(3 * nh, 4 * nh)), jnp.int32)
    cq = jnp.concatenate([jnp.swapaxes(conv_w[:, :, :MLSTM_WIDTH], 1, 2), conv_b[:, :MLSTM_WIDTH, None],
                          jnp.zeros((DEPTH, MLSTM_WIDTH, 4), conv_w.dtype)], axis=2).astype(F32)
    feature_major = jnp.concatenate([w_in[:, :, OFF_MQ:OFF_MQ + MLSTM_WIDTH], w_in[:, :, OFF_MV:OFF_G],
                                     w_in[:, :, OFF_G:][:, :, gate_order]], axis=2)
    return {
        "w_rows": jnp.concatenate([w_in[:, :, :OFF_MQ], w_in[:, :, OFF_MQ + MLSTM_WIDTH:OFF_MV]],
                                  axis=2).astype(BF16),
        "w_t": jnp.swapaxes(feature_major, 1, 2).astype(BF16),
        "gb": gate_bias.astype(F32)[:, gate_order].reshape(DEPTH, N_GATE_COLS, 1),
        "conv_wk": conv_w[:, :, MLSTM_WIDTH:].astype(F32),
        "conv_bk": row(conv_b[:, MLSTM_WIDTH:]),
        "conv_q": cq,
        "sink": attn_sink.astype(F32),
        "att_g": row(attn_norm_g),
        "mlstm_g": mlstm_norm_g.astype(F32).reshape(DEPTH, MLSTM_WIDTH, 1),
        "w_out_a": w_out[:, :ATT_WIDTH].astype(BF16),
        "w_out_m": w_out[:, ATT_WIDTH:].astype(BF16),
        "ln1_g": row(ln1_g), "ln1_b": row(ln1_b),
        "wq": wq_mem.astype(BF16), "wkv": wkv_mem.astype(BF16), "wo": wo_mem.astype(BF16),
        "ln2_g": row(ln2_g), "ln2_b": row(ln2_b),
        "router_wt": _hi_lo_rows(router_w.astype(F32).T),
        "router_b": router_bias.astype(F32).reshape(N_EXPERTS, 1),
        "w_gate": w_gate.astype(BF16), "w_up": w_up.astype(BF16), "w_down": w_down.astype(BF16),
        "ln3_g": row(ln3_g), "ln3_b": row(ln3_b),
    }


def kernel(x_prompt, x_sample, mem_prompt, mem_sample, w_in, gate_bias, conv_w, conv_b, attn_sink, attn_norm_g, mlstm_norm_g, w_out, ln1_g, ln1_b, wq_mem, wkv_mem, wo_mem, ln2_g, ln2_b, router_w, router_bias, w_gate, w_up, w_down, ln3_g, ln3_b):
    wts = _prepare_weights(w_in, gate_bias, conv_w, conv_b, attn_sink, attn_norm_g, mlstm_norm_g, w_out, ln1_g, ln1_b,
                           wq_mem, wkv_mem, wo_mem, ln2_g, ln2_b, router_w, router_bias, w_gate, w_up, w_down,
                           ln3_g, ln3_b)
    consts = _constants()
    return (_trunk(x_prompt, mem_prompt, wts, consts), _trunk(x_sample, mem_sample, wts, consts))
```

```python
import functools
import math

import jax
import jax.numpy as jnp
from jax import lax
from jax.experimental import pallas as pl
from jax.experimental.pallas import tpu as pltpu

F32 = jnp.float32
BF16 = jnp.bfloat16

D_MODEL = 1024
DEPTH = 4
HEAD_DIM = 64
N_ATT_HEADS = 8
N_KV_HEADS = 2
ATT_WIDTH = N_ATT_HEADS * HEAD_DIM
KV_WIDTH = N_KV_HEADS * HEAD_DIM
BLOCK = 128
ROT_DIM = HEAD_DIM // 4
ROPE_THETA = 500000.0
MLSTM_WIDTH = D_MODEL - ATT_WIDTH
N_MLSTM_HEADS = 4
MLSTM_HEAD_DIM = MLSTM_WIDTH // N_MLSTM_HEADS
CHUNK = 128
OFF_AQ = 0
OFF_AK = OFF_AQ + ATT_WIDTH
OFF_AV = OFF_AK + KV_WIDTH
OFF_MQ = OFF_AV + KV_WIDTH
OFF_MV = OFF_MQ + 2 * MLSTM_WIDTH
OFF_MO = OFF_MV + MLSTM_WIDTH
OFF_G = OFF_MO + MLSTM_WIDTH
N_GATE_COLS = 4 * N_MLSTM_HEADS
N_MEM = 256
N_X_HEADS = 4
X_HEAD_DIM = D_MODEL // N_X_HEADS
N_EXPERTS = 16
N_GROUPS = 4
EXPERTS_PER_GROUP = N_EXPERTS // N_GROUPS
TOP_K = 2
D_FF_EXPERT = 512
ALPHA = (2.0 * DEPTH) ** 0.25
LN_EPS = 1e-5
RMS_EPS = 1e-6
NEG = -1e30

LANES = 128
TOKEN_TILE = 512
IN_PROJ_TILE = 1024
XATTN_TILE = 1024
ROW_BLOCK = 1024
VMEM_LIMIT = 56 * 1024 * 1024


def _params(*sem):
    return pltpu.CompilerParams(dimension_semantics=sem, vmem_limit_bytes=VMEM_LIMIT)


def _layer_norm(z, g, b):
    mu = jnp.mean(z, axis=-1, keepdims=True)
    zc = z - mu
    var = jnp.mean(zc * zc, axis=-1, keepdims=True)
    return zc * lax.rsqrt(var + LN_EPS) * g + b


def _dot(a, b):
    return jnp.dot(a, b, preferred_element_type=F32)


def _dot_nt(a, b):
    return lax.dot_general(a, b, (((1,), (1,)), ((), ())), preferred_element_type=F32)


def _dot_tn(a, b):
    return lax.dot_general(a, b, (((0,), (0,)), ((), ())), preferred_element_type=F32)


_X_HALO = 8


_W_MK = OFF_MQ
_W_END = _W_MK + MLSTM_WIDTH
_GP_ROWS = 3 * 2 * N_MLSTM_HEADS


def _in_proj_kernel(x_ref, xp_ref, xn_ref, w_ref, wt_ref, gb_ref, cwk_ref, cbk_ref, cq_ref,
                    cos_ref, sa_ref, sb_ref, tri_u_ref, tri_l_ref,
                    qa_ref, ka_ref, va_ref, mk_ref, mqt_ref, mvt_ref, mot_ref, gp_ref, rn_ref, *, n_seq_tiles):
    xb = x_ref[...].astype(BF16)
    cos = cos_ref[...]
    sa = sa_ref[...]
    sb = sb_ref[...]

    def mm(lo, hi):
        return _dot(xb, w_ref[:, lo:hi])

    def rope(t):
        return t * cos + pltpu.roll(t, LANES - ROT_DIM // 2, 1) * sa + pltpu.roll(t, ROT_DIM // 2, 1) * sb

    q = mm(OFF_AQ, OFF_AK)
    scale = math.log2(math.e) / math.sqrt(HEAD_DIM)
    for j in range(ATT_WIDTH // LANES):
        qa_ref[:, j * LANES:(j + 1) * LANES] = (rope(q[:, j * LANES:(j + 1) * LANES]) * scale).astype(BF16)
    kv = mm(OFF_AK, OFF_MQ)
    ka_ref[...] = rope(kv[:, :KV_WIDTH]).astype(BF16)
    va_ref[...] = kv[:, KV_WIDTH:].astype(BF16)
    tm = xb.shape[0]
    pos = pl.program_id(0) % n_seq_tiles
    has_prev = pos > 0
    has_next = pos < n_seq_tiles - 1
    halo = jnp.concatenate([xp_ref[...], xn_ref[...]], axis=0).astype(BF16)

    def silu(y):
        return y * (1.0 / (1.0 + jnp.exp(-y)))

    u = mm(_W_MK, _W_END)
    uh = _dot(halo, w_ref[:, _W_MK:_W_END])
    rowi = lax.broadcasted_iota(jnp.int32, u.shape, 0)
    u_prev = jnp.where(rowi == 0, jnp.where(has_prev, uh[_X_HALO - 1:_X_HALO, :], 0.0), pltpu.roll(u, 1, 0))
    u_next = jnp.where(rowi == tm - 1, jnp.where(has_next, uh[_X_HALO:_X_HALO + 1, :], 0.0),
                       pltpu.roll(u, tm - 1, 0))
    yk = silu(cwk_ref[0:1, :] * u_prev + cwk_ref[1:2, :] * u + cwk_ref[2:3, :] * u_next + cbk_ref[...])
    mk_ref[...] = (yk * (MLSTM_HEAD_DIM ** -0.5)).astype(BF16)

    feat = _dot_nt(wt_ref[...], xb)
    ut = feat[:MLSTM_WIDTH]
    uht = _dot_nt(wt_ref[0:MLSTM_WIDTH, :], halo)
    lanei = lax.broadcasted_iota(jnp.int32, ut.shape, 1)
    ut_prev = jnp.where(lanei == 0, jnp.where(has_prev, uht[:, _X_HALO - 1:_X_HALO], 0.0), pltpu.roll(ut, 1, 1))
    ut_next = jnp.where(lanei == tm - 1, jnp.where(has_next, uht[:, _X_HALO:_X_HALO + 1], 0.0),
                        pltpu.roll(ut, tm - 1, 1))
    cq = cq_ref[...]
    mqt_ref[...] = silu(cq[:, 0:1] * ut_prev + cq[:, 1:2] * ut + cq[:, 2:3] * ut_next + cq[:, 3:4]).astype(BF16)
    mvt_ref[...] = feat[MLSTM_WIDTH:2 * MLSTM_WIDTH].astype(BF16)
    mot_ref[...] = feat[2 * MLSTM_WIDTH:3 * MLSTM_WIDTH].astype(BF16)

    gt = feat[3 * MLSTM_WIDTH:] + gb_ref[...]
    half = 2 * N_MLSTM_HEADS
    gi, gf = gt[:half], gt[half:]
    ls = jnp.minimum(gf, 0.0) - jnp.log1p(jnp.exp(-jnp.abs(gf)))
    n_chunks = tm // CHUNK
    stack = lambda t: jnp.concatenate([t[:, c * CHUNK:(c + 1) * CHUNK] for c in range(n_chunks)], axis=0)
    ls_rows, gi_rows = stack(ls), stack(gi)
    pre = jnp.dot(ls_rows, tri_u_ref[...], precision=lax.Precision.HIGHEST, preferred_element_type=F32)
    suf = jnp.dot(ls_rows, tri_l_ref[...], precision=lax.Precision.HIGHEST, preferred_element_type=F32)
    is_fwd = (lax.broadcasted_iota(jnp.int32, ls_rows.shape, 0) % half) < N_MLSTM_HEADS
    lane = lax.broadcasted_iota(jnp.int32, ls_rows.shape, 1)
    b = jnp.where(is_fwd, pre, suf)
    r = gi_rows - b
    cm_f, cm_b = r, r
    k = 1
    while k < CHUNK:
        cm_f = jnp.maximum(cm_f, jnp.where(lane >= k, pltpu.roll(cm_f, k, 1), -jnp.inf))
        cm_b = jnp.maximum(cm_b, jnp.where(lane < CHUNK - k, pltpu.roll(cm_b, CHUNK - k, 1), -jnp.inf))
        k *= 2
    cm = jnp.where(is_fwd, cm_f, cm_b)
    log2e = math.log2(math.e)
    b, cm, r = b * log2e, cm * log2e, r * log2e
    pad = jnp.zeros((CHUNK - half, CHUNK), F32)
    for c in range(n_chunks):
        rows = slice(c * half, (c + 1) * half)
        gp_ref[c] = jnp.concatenate([b[rows], cm[rows], r[rows]], axis=0)
        rn_ref[c * CHUNK:(c + 1) * CHUNK, :] = jnp.concatenate([r[rows], pad], axis=0).T


def _in_proj(x, w_rows, w_t, gb, cwk, cbk, cq, rope_tabs, tri_u, tri_l, seq):
    T = x.shape[0]
    tm = IN_PROJ_TILE
    n_seq_tiles = seq // tm
    halo_per_tile = tm // _X_HALO
    n_halo = T // _X_HALO
    cos_t, sa_t, sb_t = rope_tabs
    row_spec = lambda w: pl.BlockSpec((tm, w), lambda i: (i, 0))
    full = lambda a: pl.BlockSpec(a.shape, lambda i: (0,) * a.ndim)
    tab_spec = pl.BlockSpec((tm, LANES), lambda i: (i % n_seq_tiles, 0))
    prev_spec = pl.BlockSpec((_X_HALO, D_MODEL), lambda i: (jnp.maximum(i * halo_per_tile - 1, 0), 0))
    next_spec = pl.BlockSpec((_X_HALO, D_MODEL), lambda i: (jnp.minimum((i + 1) * halo_per_tile, n_halo - 1), 0))
    widths = (ATT_WIDTH, KV_WIDTH, KV_WIDTH, MLSTM_WIDTH)
    col_spec = _feature_major_spec(MLSTM_WIDTH, tm, n_seq_tiles)
    feat_major = jax.ShapeDtypeStruct((MLSTM_SEQS, MLSTM_WIDTH, T // MLSTM_SEQS), BF16)
    out_shapes = tuple(jax.ShapeDtypeStruct((T, w), BF16) for w in widths) + (
        feat_major, feat_major, feat_major,
        jax.ShapeDtypeStruct((T // CHUNK, _GP_ROWS, CHUNK), F32), jax.ShapeDtypeStruct((T, LANES), F32))
    out_specs = tuple(row_spec(w) for w in widths) + (
        col_spec, col_spec, col_spec,
        pl.BlockSpec((tm // CHUNK, _GP_ROWS, CHUNK), lambda i: (i, 0, 0)), row_spec(LANES))
    return pl.pallas_call(
        functools.partial(_in_proj_kernel, n_seq_tiles=n_seq_tiles),
        grid=(T // tm,),
        in_specs=[row_spec(D_MODEL), prev_spec, next_spec, full(w_rows), full(w_t), full(gb),
                  full(cwk), full(cbk), full(cq), tab_spec, tab_spec, tab_spec, full(tri_u), full(tri_l)],
        out_specs=out_specs,
        out_shape=out_shapes,
        compiler_params=_params("parallel"),
        name="in_proj",
    )(x, x, x, w_rows, w_t, gb, cwk, cbk, cq, cos_t, sa_t, sb_t, tri_u, tri_l)


ATT_Q_TILE = 1024
_Q_BLOCKS = ATT_Q_TILE // BLOCK


def _attn_kernel(sink_ref, q_ref, kp_ref, k_ref, kn_ref, vp_ref, v_ref, vn_ref, g_ref, o_ref, *, n_tiles):
    i = pl.program_id(1)
    lane = lax.broadcasted_iota(jnp.int32, (ATT_Q_TILE + 2 * BLOCK, LANES), 1)
    low = lane < HEAD_DIM

    def split(prev_ref, own_ref, next_ref):
        t = jnp.concatenate([prev_ref[...], own_ref[...], next_ref[...]], axis=0).astype(F32)
        r = pltpu.roll(t, HEAD_DIM, 1)
        zero = jnp.zeros_like(t)
        lo = (jnp.where(low, t, zero).astype(BF16), jnp.where(low, r, zero).astype(BF16))
        hi = (jnp.where(low, zero, r).astype(BF16), jnp.where(low, zero, t).astype(BF16))
        return lo, hi

    k_lo, k_hi = split(kp_ref, k_ref, kn_ref)
    v_lo, v_hi = split(vp_ref, v_ref, vn_ref)

    rowi = lax.broadcasted_iota(jnp.int32, (BLOCK, BLOCK), 0)
    coli = lax.broadcasted_iota(jnp.int32, (BLOCK, BLOCK), 1)
    prev_bias = jnp.where(coli < rowi, NEG, 0.0).astype(F32)
    next_bias = jnp.where(coli > rowi, NEG, 0.0).astype(F32)
    first_bias = jnp.where(i == 0, NEG, 0.0).astype(F32)
    last_bias = jnp.where(i == n_tiles - 1, NEG, 0.0).astype(F32)
    lane_o = lax.broadcasted_iota(jnp.int32, (BLOCK, LANES), 1)
    g = g_ref[...]
    log2e = math.log2(math.e)

    for r in range(_Q_BLOCKS):
        pb = prev_bias + first_bias if r == 0 else prev_bias
        nb = next_bias + last_bias if r == _Q_BLOCKS - 1 else next_bias
        rows = slice(r * BLOCK, (r + 1) * BLOCK)
        win = slice(r * BLOCK, (r + 3) * BLOCK)
        tiles = []
        for c in range(N_KV_HEADS):
            q2 = jnp.concatenate([q_ref[rows, (2 * c) * LANES:(2 * c + 1) * LANES],
                                  q_ref[rows, (2 * c + 1) * LANES:(2 * c + 2) * LANES]], axis=0)
            kc = jnp.concatenate([k_lo[c][win], k_hi[c][win]], axis=0)
            vc = jnp.concatenate([v_lo[c][win], v_hi[c][win]], axis=0)
            s = _dot_nt(q2, kc)
            p_rows, inv_rows = [], []
            for t in range(2):
                ps, invs = [], []
                for hh in range(2):
                    head = 4 * c + 2 * t + hh
                    sink = sink_ref[head] * log2e
                    blk = lambda j: s[t * BLOCK:(t + 1) * BLOCK, (3 * hh + j) * BLOCK:(3 * hh + j + 1) * BLOCK]
                    sp, so, sn = blk(0) + pb, blk(1), blk(2) + nb
                    m = jnp.maximum(jnp.max(jnp.maximum(jnp.maximum(sp, so), sn), axis=-1, keepdims=True), sink)
                    pp, po, pn = jnp.exp2(sp - m), jnp.exp2(so - m), jnp.exp2(sn - m)
                    den = jnp.sum(pp + po + pn, axis=-1, keepdims=True) + jnp.exp2(sink - m)
                    ps += [pp.astype(BF16), po.astype(BF16), pn.astype(BF16)]
                    invs.append(1.0 / den)
                p_rows.append(jnp.concatenate(ps, axis=1))
                inv_rows.append(jnp.where(lane_o < HEAD_DIM, invs[0], invs[1]))
            o2 = _dot(jnp.concatenate(p_rows, axis=0), vc)
            tiles.append(o2[:BLOCK] * inv_rows[0])
            tiles.append(o2[BLOCK:] * inv_rows[1])
        o = jnp.concatenate(tiles, axis=1)
        ms = jnp.mean(o * o, axis=-1, keepdims=True)
        o_ref[rows, :] = (o * lax.rsqrt(ms + RMS_EPS) * g).astype(BF16)


def _attention(qa, ka, va, sink, att_g, batch, seq):
    T = qa.shape[0]
    n_tiles = seq // ATT_Q_TILE
    blocks_per_seq = seq // BLOCK
    own = lambda w: pl.BlockSpec((ATT_Q_TILE, w), lambda b, i: (b * n_tiles + i, 0))
    prev = pl.BlockSpec((BLOCK, KV_WIDTH),
                        lambda b, i: (b * blocks_per_seq + jnp.maximum(i * _Q_BLOCKS - 1, 0), 0))
    nxt = pl.BlockSpec((BLOCK, KV_WIDTH),
                       lambda b, i: (b * blocks_per_seq + jnp.minimum((i + 1) * _Q_BLOCKS, blocks_per_seq - 1), 0))
    return pl.pallas_call(
        functools.partial(_attn_kernel, n_tiles=n_tiles),
        grid=(batch, n_tiles),
        in_specs=[pl.BlockSpec(memory_space=pltpu.SMEM), own(ATT_WIDTH), prev, own(KV_WIDTH), nxt,
                  prev, own(KV_WIDTH), nxt, pl.BlockSpec((1, ATT_WIDTH), lambda b, i: (0, 0))],
        out_specs=own(ATT_WIDTH),
        out_shape=jax.ShapeDtypeStruct((T, ATT_WIDTH), BF16),
        compiler_params=_params("parallel", "parallel"),
        name="band_attention",
    )(sink, qa, ka, ka, ka, va, va, va, att_g)


MLSTM_SEQS = 4
_STATE_ROWS = MLSTM_HEAD_DIM + 8


def _mlstm_kernel(qt_f, vt_f, k_f, rn_f, gp_f, qt_b, vt_b, k_b, rn_b, gp_b, of_ref, ob_ref, c_state, m_state):
    @pl.when(pl.program_id(1) == 0)
    def _():
        c_state[...] = jnp.zeros_like(c_state)
        m_state[...] = jnp.zeros_like(m_state)

    key = lax.broadcasted_iota(jnp.int32, (CHUNK, CHUNK), 0)
    qry = lax.broadcasted_iota(jnp.int32, (CHUNK, CHUNK), 1)
    nh = N_MLSTM_HEADS

    units = []
    for sq in range(MLSTM_SEQS):
        for fwd, qt_ref, vt_ref, k_ref, rn_ref, gp_ref, o_ref in ((True, qt_f, vt_f, k_f, rn_f, gp_f, of_ref),
                                                                 (False, qt_b, vt_b, k_b, rn_b, gp_b, ob_ref)):
            gates = gp_ref[sq, 0]
            off = 0 if fwd else nh
            a_pos = CHUNK - 1 if fwd else 0
            for h in range(nh):
                hs = slice(h * MLSTM_HEAD_DIM, (h + 1) * MLSTM_HEAD_DIM)
                b = gates[off + h:off + h + 1, :]
                st = (2 * sq + (0 if fwd else 1)) * nh + h
                units.append(dict(
                    sq=sq, hs=hs, st=st, o_ref=o_ref, qt_ref=qt_ref, vt_ref=vt_ref, k_ref=k_ref,
                    visible=(key <= qry) if fwd else (key >= qry),
                    b=b, cm=gates[2 * nh + off + h:2 * nh + off + h + 1, :],
                    r_row=gates[4 * nh + off + h:4 * nh + off + h + 1, :],
                    r_keys=jnp.broadcast_to(rn_ref[sq, :, off + h:off + h + 1], (CHUNK, CHUNK)),
                    a=b[:, a_pos:a_pos + 1],
                    m_in=m_state[st:st + 1, 0:1]))

    for u in units:
        inter_log = u["b"] + u["m_in"]
        m_t = jnp.maximum(inter_log, u["b"] + u["cm"])
        u["decay"] = jnp.exp2(jnp.where(u["visible"], u["r_keys"] + (u["b"] - m_t), NEG))
        u["inter_w"] = jnp.exp2(inter_log - m_t)
        u["floor"] = jnp.exp2(-m_t)
    for u in units:
        qt = u["qt_ref"][u["sq"], u["hs"], :]
        u["k"] = u["k_ref"][u["sq"], :, u["hs"]]
        u["vt"] = u["vt_ref"][u["sq"], u["hs"], :]
        u["scores"] = _dot(u["k"], qt)
        u["c_in"] = c_state[u["st"]]
        u["inter"] = _dot(u["c_in"].astype(BF16), qt)
    for u in units:
        sw = u["scores"] * u["decay"]
        num = _dot(u["vt"], sw.astype(BF16)) + u["inter_w"] * u["inter"][:MLSTM_HEAD_DIM]
        den = (jnp.sum(sw, axis=0, keepdims=True)
               + u["inter_w"] * u["inter"][MLSTM_HEAD_DIM:MLSTM_HEAD_DIM + 1])
        u["o_ref"][u["sq"], u["hs"], :] = (num / jnp.maximum(jnp.abs(den), u["floor"])).astype(u["o_ref"].dtype)
    for u in units:
        a, m_in, st = u["a"], u["m_in"], u["st"]
        g_max = a + jnp.max(u["r_row"], axis=-1, keepdims=True)
        kw = u["k"].astype(F32) * jnp.exp2(u["r_keys"] + (a - g_max))
        m_new = jnp.maximum(a + m_in, g_max)
        keep = jnp.exp2(a + m_in - m_new)
        add = jnp.exp2(g_max - m_new)
        c_state[st, 0:MLSTM_HEAD_DIM, :] = keep * u["c_in"][:MLSTM_HEAD_DIM] + add * _dot(u["vt"], kw.astype(BF16))
        c_state[st, MLSTM_HEAD_DIM:MLSTM_HEAD_DIM + 1, :] = (
            keep * u["c_in"][MLSTM_HEAD_DIM:MLSTM_HEAD_DIM + 1] + add * jnp.sum(kw, axis=0, keepdims=True))
        m_state[st:st + 1, :] = jnp.broadcast_to(m_new, (1, LANES))


def _feature_major_spec(width, tm, tiles_per_seq):
    def index(i):
        s = i // tiles_per_seq
        return (s % MLSTM_SEQS, 0, (s // MLSTM_SEQS) * tiles_per_seq + i % tiles_per_seq)
    return pl.BlockSpec((None, width, tm), index)


def _mlstm(mqt, mvt, mk, rn, gp, batch, seq):
    T = mk.shape[0]
    nc = seq // CHUNK
    groups = batch // MLSTM_SEQS
    mk = mk.reshape(groups, MLSTM_SEQS, seq, MLSTM_WIDTH)
    rn = rn.reshape(groups, MLSTM_SEQS, seq, LANES)
    gp = gp.reshape(groups, MLSTM_SEQS, nc, _GP_ROWS, CHUNK)

    def specs(chunk_of):
        feat = pl.BlockSpec((MLSTM_SEQS, MLSTM_WIDTH, CHUNK), lambda b, i: (0, 0, b * nc + chunk_of(i)))
        return [feat, feat,
                pl.BlockSpec((None, MLSTM_SEQS, CHUNK, MLSTM_WIDTH), lambda b, i: (b, 0, chunk_of(i), 0)),
                pl.BlockSpec((None, MLSTM_SEQS, CHUNK, LANES), lambda b, i: (b, 0, chunk_of(i), 0)),
                pl.BlockSpec((None, MLSTM_SEQS, 1, _GP_ROWS, CHUNK), lambda b, i: (b, 0, chunk_of(i), 0, 0))]

    fwd_chunk = lambda i: i
    bwd_chunk = lambda i: nc - 1 - i
    out = jax.ShapeDtypeStruct((MLSTM_SEQS, MLSTM_WIDTH, T // MLSTM_SEQS), BF16)
    n_state = 2 * MLSTM_SEQS * N_MLSTM_HEADS
    return pl.pallas_call(
        _mlstm_kernel,
        grid=(groups, nc),
        in_specs=specs(fwd_chunk) + specs(bwd_chunk),
        out_specs=(specs(fwd_chunk)[0], specs(bwd_chunk)[0]),
        out_shape=(out, out),
        scratch_shapes=[pltpu.VMEM((n_state, _STATE_ROWS, MLSTM_HEAD_DIM), F32),
                        pltpu.VMEM((n_state, LANES), F32)],
        compiler_params=_params("parallel", "arbitrary"),
        name="mlstm",
    )(mqt, mvt, mk, rn, gp, mqt, mvt, mk, rn, gp)


def _mixer_out(x, att_ref, hf_ref, hb_ref, mo_ref, mg_ref, wa_ref, wm_ref, g_ref, b_ref):
    h = hf_ref[...].astype(F32) + hb_ref[...].astype(F32)
    parts = []
    for hd in range(N_MLSTM_HEADS):
        hh = h[hd * MLSTM_HEAD_DIM:(hd + 1) * MLSTM_HEAD_DIM]
        ms = jnp.mean(hh * hh, axis=0, keepdims=True)
        parts.append(hh * lax.rsqrt(ms + RMS_EPS))
    hn = jnp.concatenate(parts, axis=0) * mg_ref[...]
    gate = 1.0 / (1.0 + jnp.exp(-mo_ref[...].astype(F32)))
    y = _dot(att_ref[...], wa_ref[...]) + _dot_tn((hn * gate).astype(BF16), wm_ref[...])
    return _layer_norm(ALPHA * x + y, g_ref[...], b_ref[...])


def _kv_proj_kernel(m_ref, w_ref, k_ref, v_ref):
    mb = m_ref[...].astype(BF16)
    k_ref[...] = _dot(mb, w_ref[:, :D_MODEL]).astype(BF16)
    v_ref[...] = _dot(mb, w_ref[:, D_MODEL:]).astype(BF16)


def _kv_proj(mem, wkv):
    M = mem.shape[0]
    tm = TOKEN_TILE
    row = pl.BlockSpec((tm, D_MODEL), lambda i: (i, 0))
    out = jax.ShapeDtypeStruct((M, D_MODEL), BF16)
    return pl.pallas_call(
        _kv_proj_kernel,
        grid=(M // tm,),
        in_specs=[row, pl.BlockSpec(wkv.shape, lambda i: (0, 0))],
        out_specs=(row, row),
        out_shape=(out, out),
        compiler_params=_params("parallel"),
        name="mem_kv_proj",
    )(mem, wkv)


def _xattn_kernel(x_ref, att_ref, hf_ref, hb_ref, mo_ref, mg_ref, wa_ref, wm_ref, g1_ref, b1_ref,
                  k_ref, v_ref, wq_ref, wo_ref, g_ref, b_ref, rwt_ref, rb_ref, tri_ref, low_ref,
                  o_ref, info_ref, cnt_ref):
    subs = [slice(t * TOKEN_TILE, (t + 1) * TOKEN_TILE) for t in range(x_ref.shape[0] // TOKEN_TILE)]
    xs = [_mixer_out(x_ref[rows], att_ref.at[rows], hf_ref.at[:, rows], hb_ref.at[:, rows], mo_ref.at[:, rows],
                     mg_ref, wa_ref, wm_ref, g1_ref, b1_ref) for rows in subs]
    scale = math.log2(math.e) / math.sqrt(X_HEAD_DIM)
    qs = [_dot(x.astype(BF16), wq_ref[...]) for x in xs]
    outs = [[] for _ in subs]
    for h in range(N_X_HEADS):
        hs = slice(h * X_HEAD_DIM, (h + 1) * X_HEAD_DIM)
        for t, q in enumerate(qs):
            s = _dot_nt((q[:, hs] * scale).astype(BF16), k_ref[:, hs])
            p = jnp.exp2(s - jnp.max(s, axis=-1, keepdims=True))
            inv = 1.0 / jnp.sum(p, axis=-1, keepdims=True)
            outs[t].append((_dot(p.astype(BF16), v_ref[:, hs]) * inv).astype(BF16))
    ys = [_dot(jnp.concatenate(o, axis=1), wo_ref[...]) for o in outs]
    for t, rows in enumerate(subs):
        x2 = _layer_norm(ALPHA * xs[t] + ys[t], g_ref[...], b_ref[...])
        o_ref[rows, :] = x2
        _route(x2, rwt_ref, rb_ref, tri_ref, low_ref, info_ref.at[:, rows], cnt_ref.at[t:t + 1])


def _mixer_out_xattn(x, att, hft, hbt, mot, mg, wa, wm, g1, b1, k_mem, v_mem, wq, wo, g, b,
                     router_wt, router_b, tri_strict, low_strict, batch, seq):
    T = x.shape[0]
    tm = XATTN_TILE
    nt = seq // tm
    row = pl.BlockSpec((tm, D_MODEL), lambda bb, i: (bb * nt + i, 0))
    att_row = pl.BlockSpec((tm, ATT_WIDTH), lambda bb, i: (bb * nt + i, 0))
    flat_feat = _feature_major_spec(MLSTM_WIDTH, tm, nt)
    feat = pl.BlockSpec(flat_feat.block_shape, lambda bb, i: flat_feat.index_map(bb * nt + i))
    mem = pl.BlockSpec((N_MEM, D_MODEL), lambda bb, i: (bb, 0))
    full = lambda a: pl.BlockSpec(a.shape, lambda bb, i: (0,) * a.ndim)
    return pl.pallas_call(
        _xattn_kernel,
        grid=(batch, nt),
        in_specs=[row, att_row, feat, feat, feat, full(mg), full(wa), full(wm), full(g1), full(b1),
                  mem, mem, full(wq), full(wo), full(g), full(b), full(router_wt), full(router_b),
                  full(tri_strict), full(low_strict)],
        out_specs=(row, pl.BlockSpec((_IDX_ROWS, tm), lambda bb, i: (0, bb * nt + i)),
                   pl.BlockSpec((tm // TOKEN_TILE, N_EXPERTS, LANES), lambda bb, i: (bb * nt + i, 0, 0))),
        out_shape=(jax.ShapeDtypeStruct((T, D_MODEL), F32), jax.ShapeDtypeStruct((_IDX_ROWS, T), F32),
                   jax.ShapeDtypeStruct((T // TOKEN_TILE, N_EXPERTS, LANES), jnp.int32)),
        compiler_params=_params("parallel", "parallel"),
        name="mixer_out_xattn_route",
    )(x, att, hft, hbt, mot, mg, wa, wm, g1, b1, k_mem, v_mem, wq, wo, g, b, router_wt, router_b,
      tri_strict, low_strict)


_IDX_ROWS = 8
SLAB = 16
TILE_SLOTS = TOP_K * TOKEN_TILE + N_EXPERTS * SLAB


def _route(x, wt_ref, b_ref, tri_ref, low_ref, info_ref, cnt_ref):
    x_hi = x.astype(BF16)
    x_lo = (x - x_hi.astype(F32)).astype(BF16)
    by_hi = _dot_nt(wt_ref[...], x_hi)
    logits = by_hi[:N_EXPERTS] + by_hi[N_EXPERTS:] + _dot_nt(wt_ref[0:N_EXPERTS, :], x_lo)
    s = 1.0 / (1.0 + jnp.exp(-logits))
    sel = s + b_ref[...]
    srow = lambda e: s[e:e + 1, :]
    brow = lambda e: sel[e:e + 1, :]
    best = None
    gi = None
    for gidx in range(N_GROUPS):
        vals = [brow(gidx * EXPERTS_PER_GROUP + j) for j in range(EXPERTS_PER_GROUP)]
        top2 = None
        for a in range(EXPERTS_PER_GROUP):
            for b in range(a + 1, EXPERTS_PER_GROUP):
                pair = vals[a] + vals[b]
                top2 = pair if top2 is None else jnp.maximum(top2, pair)
        if best is None:
            best, gi = top2, jnp.zeros(top2.shape, jnp.int32)
        else:
            better = top2 > best
            gi = jnp.where(better, gidx, gi)
            best = jnp.where(better, top2, best)

    def in_group(rowfn, j):
        out = rowfn(j)
        for gidx in range(1, N_GROUPS):
            out = jnp.where(gi == gidx, rowfn(gidx * EXPERTS_PER_GROUP + j), out)
        return out

    bv = [in_group(brow, j) for j in range(EXPERTS_PER_GROUP)]
    sv = [in_group(srow, j) for j in range(EXPERTS_PER_GROUP)]

    def argmax_first(vals):
        bi = jnp.zeros(vals[0].shape, jnp.int32)
        bm = vals[0]
        for j in range(1, len(vals)):
            better = vals[j] > bm
            bi = jnp.where(better, j, bi)
            bm = jnp.where(better, vals[j], bm)
        return bi

    i1 = argmax_first(bv)
    i2 = argmax_first([jnp.where(i1 == j, -jnp.inf, bv[j]) for j in range(EXPERTS_PER_GROUP)])

    def pick(vals, idx):
        out = vals[0]
        for j in range(1, len(vals)):
            out = jnp.where(idx == j, vals[j], out)
        return out

    w1 = pick(sv, i1)
    w2 = pick(sv, i2)
    tot = w1 + w2
    tm = logits.shape[1]
    eid = lax.broadcasted_iota(jnp.int32, (N_EXPERTS, tm), 0)
    oh1 = (eid == gi * EXPERTS_PER_GROUP + i1).astype(F32)
    oh2 = (eid == gi * EXPERTS_PER_GROUP + i2).astype(F32)
    before1 = _dot(oh1.astype(BF16), tri_ref[...])
    before2 = _dot(oh2.astype(BF16), tri_ref[...])
    c1 = jnp.sum(oh1, axis=1, keepdims=True)
    cnt = c1 + jnp.sum(oh2, axis=1, keepdims=True)
    rows = jnp.floor((cnt + (SLAB - 1)) * (1.0 / SLAB)) * SLAB
    rows_b = jnp.broadcast_to(rows, (N_EXPERTS, LANES))
    start = jnp.dot(low_ref[...], rows_b, precision=lax.Precision.HIGHEST, preferred_element_type=F32)[:, 0:1]
    slot1 = jnp.sum(oh1 * (start + before1), axis=0, keepdims=True)
    slot2 = jnp.sum(oh2 * (start + c1 + before2), axis=0, keepdims=True)
    zf = jnp.zeros((_IDX_ROWS - 2 * TOP_K, tm), F32)
    info_ref[...] = jnp.concatenate([w1 / tot, w2 / tot, slot1, slot2, zf], axis=0)
    cnt_ref[0] = rows_b.astype(jnp.int32)


_BIG_SLAB = 4 * SLAB


def _slab_copies(i, gs_ref, ls_ref, rc_ref, make_copy, op):
    for e in range(N_EXPERTS):
        idx = i * N_EXPERTS + e
        ls = ls_ref[idx]
        gs = gs_ref[idx]
        n_big = rc_ref[idx] // _BIG_SLAB
        done = n_big * _BIG_SLAB

        def big(j, carry, ls=ls, gs=gs):
            op(make_copy(pl.multiple_of(ls + j * _BIG_SLAB, SLAB), pl.multiple_of(gs + j * _BIG_SLAB, SLAB),
                         _BIG_SLAB))
            return carry

        def small(j, carry, ls=ls + done, gs=gs + done):
            op(make_copy(pl.multiple_of(ls + j * SLAB, SLAB), pl.multiple_of(gs + j * SLAB, SLAB), SLAB))
            return carry

        lax.fori_loop(0, n_big, big, 0)
        lax.fori_loop(0, (rc_ref[idx] - done) // SLAB, small, 0)


_ZERO_ROWS = 128
_COMBINE_SUBTILES = 1


def _dispatch_kernel(gs_ref, ls_ref, rc_ref, ts_ref, tn_ref, x_ref, info_ref, xp_ref, xs_ref, zero_ref, sem, zsem):
    i = pl.program_id(0)
    last = pl.num_programs(0) - 1
    buf = i % 2
    slot0 = info_ref[2:3, :].astype(jnp.int32)
    slot1 = info_ref[3:4, :].astype(jnp.int32)
    xb = x_ref[...].astype(BF16)
    tm = xb.shape[0]
    sid = lax.broadcasted_iota(jnp.int32, (TILE_SLOTS, tm), 0)
    sel = jnp.where((sid == slot0) | (sid == slot1), 1.0, 0.0).astype(BF16)
    xs_ref[buf] = _dot(sel, xb).astype(BF16)

    def copies(tile, which, op):
        def make_copy(tile_row, global_row, rows):
            return pltpu.make_async_copy(xs_ref.at[which, pl.ds(tile_row, rows), :],
                                         xp_ref.at[pl.ds(global_row, rows), :], sem.at[which])
        _slab_copies(tile, gs_ref, ls_ref, rc_ref, make_copy, op)

    @pl.when(i > 0)
    def _():
        copies(i - 1, 1 - buf, lambda cp: cp.wait())

    copies(i, buf, lambda cp: cp.start())

    @pl.when(i == last)
    def _():
        zero_ref[...] = jnp.zeros_like(zero_ref)

        def tail(op):
            for e in range(N_EXPERTS + 1):
                t0 = ts_ref[e]
                rows = SLAB if e < N_EXPERTS else _ZERO_ROWS

                def body(j, carry, t0=t0, rows=rows):
                    op(pltpu.make_async_copy(zero_ref.at[pl.ds(0, rows), :],
                                             xp_ref.at[pl.ds(pl.multiple_of(t0 + j * rows, rows), rows), :], zsem))
                    return carry

                lax.fori_loop(0, tn_ref[e] // rows, body, 0)

        tail(lambda cp: cp.start())
        copies(i, buf, lambda cp: cp.wait())
        tail(lambda cp: cp.wait())


def _dispatch(x, info, gstart, lstart, rc, tail_start, tail_rows, n_rows):
    T = x.shape[0]
    tm = TOKEN_TILE
    grid_spec = pltpu.PrefetchScalarGridSpec(
        num_scalar_prefetch=5,
        grid=(T // tm,),
        in_specs=[pl.BlockSpec((tm, D_MODEL), lambda i, *_: (i, 0)),
                  pl.BlockSpec((_IDX_ROWS, tm), lambda i, *_: (0, i))],
        out_specs=pl.BlockSpec(memory_space=pl.ANY),
        scratch_shapes=[pltpu.VMEM((2, TILE_SLOTS, D_MODEL), BF16), pltpu.VMEM((_ZERO_ROWS, D_MODEL), BF16),
                        pltpu.SemaphoreType.DMA((2,)), pltpu.SemaphoreType.DMA],
    )
    return pl.pallas_call(
        _dispatch_kernel,
        grid_spec=grid_spec,
        out_shape=jax.ShapeDtypeStruct((n_rows, D_MODEL), BF16),
        compiler_params=_params("arbitrary"),
        name="moe_dispatch",
    )(gstart, lstart, rc, tail_start, tail_rows, x, info)


def _ffn_kernel(be_ref, nu_ref, x_ref, wg_ref, wu_ref, wd_ref, o_ref):
    i = pl.program_id(0)

    @pl.when(i < nu_ref[0])
    def _():
        xb = x_ref[...]
        gate = _dot(xb, wg_ref[...])
        up = _dot(xb, wu_ref[...])
        h = gate * (1.0 / (1.0 + jnp.exp(-gate))) * up
        o_ref[...] = _dot(h.astype(BF16), wd_ref[...]).astype(BF16)

    @pl.when(i >= nu_ref[0])
    def _():
        o_ref[...] = jnp.zeros_like(o_ref)


def _expert_ffn(xp, blk_e, n_used, w_gate, w_up, w_down, layer):
    P = xp.shape[0]
    wspec = lambda shp: pl.BlockSpec((None, None) + shp, lambda i, be, nu: (layer, be[i], 0, 0))
    grid_spec = pltpu.PrefetchScalarGridSpec(
        num_scalar_prefetch=2,
        grid=(P // ROW_BLOCK,),
        in_specs=[pl.BlockSpec((ROW_BLOCK, D_MODEL), lambda i, be, nu: (jnp.minimum(i, nu[0] - 1), 0)),
                  wspec((D_MODEL, D_FF_EXPERT)), wspec((D_MODEL, D_FF_EXPERT)), wspec((D_FF_EXPERT, D_MODEL))],
        out_specs=pl.BlockSpec((ROW_BLOCK, D_MODEL), lambda i, be, nu: (i, 0)),
    )
    return pl.pallas_call(
        _ffn_kernel,
        grid_spec=grid_spec,
        out_shape=jax.ShapeDtypeStruct((P, D_MODEL), BF16),
        compiler_params=_params("arbitrary"),
        name="expert_ffn",
    )(blk_e, n_used, xp, w_gate, w_up, w_down)


def _combine_kernel(gs_ref, ls_ref, rc_ref, x_ref, info_ref, eye_ref, g_ref, b_ref, yp_ref, o_ref, ys_ref, sem):
    i = pl.program_id(0)
    buf = i % 2

    def copies(tile, which, op):
        def make_copy(tile_row, global_row, rows):
            return pltpu.make_async_copy(yp_ref.at[pl.ds(global_row, rows), :],
                                         ys_ref.at[which, pl.ds(tile_row, rows), :], sem.at[which])
        _slab_copies(tile, gs_ref, ls_ref, rc_ref, make_copy, op)

    @pl.when(i == 0)
    def _():
        ys_ref[...] = jnp.zeros_like(ys_ref)
        copies(i, buf, lambda cp: cp.start())

    @pl.when(i + 1 < pl.num_programs(0))
    def _():
        copies(i + 1, 1 - buf, lambda cp: cp.start())

    cols = lax.dot_general(info_ref[...], eye_ref[...], (((0,), (0,)), ((), ())),
                           precision=lax.Precision.HIGHEST, preferred_element_type=F32)
    tm = cols.shape[0]
    sub = tm // _COMBINE_SUBTILES
    sid = lax.broadcasted_iota(jnp.int32, (sub, TILE_SLOTS), 1)
    weights = []
    for t in range(_COMBINE_SUBTILES):
        c = cols[t * sub:(t + 1) * sub]
        weights.append((jnp.where(sid == c[:, 2:3].astype(jnp.int32), c[:, 0:1], 0.0)
                        + jnp.where(sid == c[:, 3:4].astype(jnp.int32), c[:, 1:2], 0.0)).astype(BF16))
    copies(i, buf, lambda cp: cp.wait())
    ys = [_dot(w, ys_ref[buf]) for w in weights]
    for t in range(_COMBINE_SUBTILES):
        rows = slice(t * sub, (t + 1) * sub)
        o_ref[rows, :] = _layer_norm(ALPHA * x_ref[rows, :] + ys[t], g_ref[...], b_ref[...])


def _combine_ln(x, yp, info, gstart, lstart, rc, eye, g, b):
    T = x.shape[0]
    tm = TOKEN_TILE
    full = lambda a: pl.BlockSpec(a.shape, lambda i, *_: (0,) * a.ndim)
    grid_spec = pltpu.PrefetchScalarGridSpec(
        num_scalar_prefetch=3,
        grid=(T // tm,),
        in_specs=[pl.BlockSpec((tm, D_MODEL), lambda i, *_: (i, 0)),
                  pl.BlockSpec((_IDX_ROWS, tm), lambda i, *_: (0, i)), full(eye), full(g), full(b),
                  pl.BlockSpec(memory_space=pl.ANY)],
        out_specs=pl.BlockSpec((tm, D_MODEL), lambda i, *_: (i, 0)),
        scratch_shapes=[pltpu.VMEM((2, TILE_SLOTS, D_MODEL), BF16), pltpu.SemaphoreType.DMA((2,))],
    )
    return pl.pallas_call(
        _combine_kernel,
        grid_spec=grid_spec,
        out_shape=jax.ShapeDtypeStruct((T, D_MODEL), F32),
        compiler_params=_params("arbitrary"),
        name="moe_combine_ln3",
    )(gstart, lstart, rc, x, info, eye, g, b, yp)


def _moe(x, info, rows, consts, w_gate, w_up, w_down, layer, g, b):
    T = x.shape[0]
    n_tiles = T // TOKEN_TILE
    P = T * TOP_K + n_tiles * N_EXPERTS * SLAB + N_EXPERTS * ROW_BLOCK
    rc = rows[:, :, 0]
    lstart = jnp.cumsum(rc, axis=1) - rc
    region = (jnp.sum(rc, axis=0) + ROW_BLOCK - 1) // ROW_BLOCK * ROW_BLOCK
    region_end = jnp.cumsum(region)
    gstart = (region_end - region)[None, :] + jnp.cumsum(rc, axis=0) - rc
    n_blk = P // ROW_BLOCK
    blk_row0 = jnp.arange(n_blk, dtype=jnp.int32) * ROW_BLOCK
    blk_e = jnp.minimum(jnp.sum(region_end[None, :] <= blk_row0[:, None], axis=1), N_EXPERTS - 1).astype(jnp.int32)
    n_used = (region_end[-1:] // ROW_BLOCK).astype(jnp.int32)
    flat = lambda t: t.reshape(-1).astype(jnp.int32)
    used = jnp.sum(rc, axis=0)
    tail_start = flat(jnp.concatenate([region_end - region + used, region_end[-1:]]))
    tail_rows = flat(jnp.concatenate([region - used, P - region_end[-1:]]))
    gstart, lstart, rc = flat(gstart), flat(lstart), flat(rc)
    xp = _dispatch(x, info, gstart, lstart, rc, tail_start, tail_rows, P)
    yp = _expert_ffn(xp, blk_e, n_used, w_gate, w_up, w_down, layer)
    return _combine_ln(x, yp, info, gstart, lstart, rc, consts["eye"], g, b)


def _rope_tables(seq):
    inv_freq = ROPE_THETA ** (-jnp.arange(0, ROT_DIM, 2, dtype=F32) / ROT_DIM)
    ang = jnp.arange(seq, dtype=F32)[:, None] * inv_freq[None, :]
    cos, sin = jnp.cos(ang), jnp.sin(ang)
    half = ROT_DIM // 2
    one = jnp.ones((seq, HEAD_DIM - ROT_DIM), F32)
    zero = jnp.zeros((seq, HEAD_DIM - ROT_DIM), F32)
    zh = jnp.zeros((seq, half), F32)
    cos_h = jnp.concatenate([cos, cos, one], axis=1)
    sa_h = jnp.concatenate([-sin, zh, zero], axis=1)
    sb_h = jnp.concatenate([zh, sin, zero], axis=1)
    rep = lambda t: jnp.concatenate([t] * (LANES // HEAD_DIM), axis=1)
    return rep(cos_h), rep(sa_h), rep(sb_h)


def _constants():
    r = lax.broadcasted_iota(jnp.int32, (CHUNK, CHUNK), 0)
    c = lax.broadcasted_iota(jnp.int32, (CHUNK, CHUNK), 1)
    rr = lax.broadcasted_iota(jnp.int32, (TOKEN_TILE, TOKEN_TILE), 0)
    cc = lax.broadcasted_iota(jnp.int32, (TOKEN_TILE, TOKEN_TILE), 1)
    er = lax.broadcasted_iota(jnp.int32, (N_EXPERTS, N_EXPERTS), 0)
    ec = lax.broadcasted_iota(jnp.int32, (N_EXPERTS, N_EXPERTS), 1)
    return {
        "tri_u": (r <= c).astype(F32),
        "tri_l": (r >= c).astype(F32),
        "tri_strict": (rr < cc).astype(BF16),
        "low_strict": (ec < er).astype(F32),
        "eye": (lax.broadcasted_iota(jnp.int32, (_IDX_ROWS, LANES), 0)
                == lax.broadcasted_iota(jnp.int32, (_IDX_ROWS, LANES), 1)).astype(F32),
    }


def _trunk(x, mem, wts, consts):
    batch, seq, _ = x.shape
    T = batch * seq
    x = x.reshape(T, D_MODEL)
    mem2 = mem.reshape(batch * N_MEM, D_MODEL)
    rope_tabs = _rope_tables(seq)
    for l in range(DEPTH):
        qa, ka, va, mk, mqt, mvt, mot, gp, rn = _in_proj(
            x, wts["w_rows"][l], wts["w_t"][l], wts["gb"][l], wts["conv_wk"][l], wts["conv_bk"][l],
            wts["conv_q"][l], rope_tabs, consts["tri_u"], consts["tri_l"], seq)
        att = _attention(qa, ka, va, wts["sink"][l], wts["att_g"][l], batch, seq)
        hft, hbt = _mlstm(mqt, mvt, mk, rn, gp, batch, seq)
        k_mem, v_mem = _kv_proj(mem2, wts["wkv"][l])
        x, info, rows = _mixer_out_xattn(
            x, att, hft, hbt, mot, wts["mlstm_g"][l], wts["w_out_a"][l], wts["w_out_m"][l], wts["ln1_g"][l],
            wts["ln1_b"][l], k_mem, v_mem, wts["wq"][l], wts["wo"][l], wts["ln2_g"][l], wts["ln2_b"][l],
            wts["router_wt"], wts["router_b"], consts["tri_strict"], consts["low_strict"], batch, seq)
        x = _moe(x, info, rows, consts, wts["w_gate"], wts["w_up"], wts["w_down"], l,
                 wts["ln3_g"][l], wts["ln3_b"][l])
    return x.reshape(batch, seq, D_MODEL)


def _hi_lo_rows(w):
    hi = w.astype(BF16)
    return jnp.concatenate([hi, (w - hi.astype(F32)).astype(BF16)], axis=0)


def _prepare_weights(w_in, gate_bias, conv_w, conv_b, attn_sink, attn_norm_g, mlstm_norm_g, w_out, ln1_g, ln1_b,
                     wq_mem, wkv_mem, wo_mem, ln2_g, ln2_b, router_w, router_bias, w_gate, w_up, w_down, ln3_g, ln3_b):
    row = lambda t: t.astype(F32).reshape(DEPTH, 1, t.shape[-1])
    nh = N_MLSTM_HEADS
    gate_order = jnp.array(list(range(0, nh)) + list(range(2 * nh, 3 * nh))
                           + list(range(nh, 2 * nh)) + list(range(3 * nh, 4 * nh)), jnp.int32)
    cq = jnp.concatenate([jnp.swapaxes(conv_w[:, :, :MLSTM_WIDTH], 1, 2), conv_b[:, :MLSTM_WIDTH, None],
                          jnp.zeros((DEPTH, MLSTM_WIDTH, 4), conv_w.dtype)], axis=2).astype(F32)
    feature_major = jnp.concatenate([w_in[:, :, OFF_MQ:OFF_MQ + MLSTM_WIDTH], w_in[:, :, OFF_MV:OFF_G],
                                     w_in[:, :, OFF_G:][:, :, gate_order]], axis=2)
    return {
        "w_rows": jnp.concatenate([w_in[:, :, :OFF_MQ], w_in[:, :, OFF_MQ + MLSTM_WIDTH:OFF_MV]],
                                  axis=2).astype(BF16),
        "w_t": jnp.swapaxes(feature_major, 1, 2).astype(BF16),
        "gb": gate_bias.astype(F32)[:, gate_order].reshape(DEPTH, N_GATE_COLS, 1),
        "conv_wk": conv_w[:, :, MLSTM_WIDTH:].astype(F32),
        "conv_bk": row(conv_b[:, MLSTM_WIDTH:]),
        "conv_q": cq,
        "sink": attn_sink.astype(F32),
        "att_g": row(attn_norm_g),
        "mlstm_g": mlstm_norm_g.astype(F32).reshape(DEPTH, MLSTM_WIDTH, 1),
        "w_out_a": w_out[:, :ATT_WIDTH].astype(BF16),
        "w_out_m": w_out[:, ATT_WIDTH:].astype(BF16),
        "ln1_g": row(ln1_g), "ln1_b": row(ln1_b),
        "wq": wq_mem.astype(BF16), "wkv": wkv_mem.astype(BF16), "wo": wo_mem.astype(BF16),
        "ln2_g": row(ln2_g), "ln2_b": row(ln2_b),
        "router_wt": _hi_lo_rows(router_w.astype(F32).T),
        "router_b": router_bias.astype(F32).reshape(N_EXPERTS, 1),
        "w_gate": w_gate.astype(BF16), "w_up": w_up.astype(BF16), "w_down": w_down.astype(BF16),
        "ln3_g": row(ln3_g), "ln3_b": row(ln3_b),
    }


def kernel(x_prompt, x_sample, mem_prompt, mem_sample, w_in, gate_bias, conv_w, conv_b, attn_sink, attn_norm_g, mlstm_norm_g, w_out, ln1_g, ln1_b, wq_mem, wkv_mem, wo_mem, ln2_g, ln2_b, router_w, router_bias, w_gate, w_up, w_down, ln3_g, ln3_b):
    wts = _prepare_weights(w_in, gate_bias, conv_w, conv_b, attn_sink, attn_norm_g, mlstm_norm_g, w_out, ln1_g, ln1_b,
                           wq_mem, wkv_mem, wo_mem, ln2_g, ln2_b, router_w, router_bias, w_gate, w_up, w_down,
                           ln3_g, ln3_b)
    consts = _constants()
    return (_trunk(x_prompt, mem_prompt, wts, consts), _trunk(x_sample, mem_sample, wts, consts))
```

```python
import functools
import math

import jax
import jax.numpy as jnp
from jax import lax
from jax.experimental import pallas as pl
from jax.experimental.pallas import tpu as pltpu

F32 = jnp.float32
BF16 = jnp.bfloat16

D_MODEL = 1024
DEPTH = 4
HEAD_DIM = 64
N_ATT_HEADS = 8
N_KV_HEADS = 2
ATT_WIDTH = N_ATT_HEADS * HEAD_DIM
KV_WIDTH = N_KV_HEADS * HEAD_DIM
BLOCK = 128
ROT_DIM = HEAD_DIM // 4
ROPE_THETA = 500000.0
MLSTM_WIDTH = D_MODEL - ATT_WIDTH
N_MLSTM_HEADS = 4
MLSTM_HEAD_DIM = MLSTM_WIDTH // N_MLSTM_HEADS
CHUNK = 128
OFF_AQ = 0
OFF_AK = OFF_AQ + ATT_WIDTH
OFF_AV = OFF_AK + KV_WIDTH
OFF_MQ = OFF_AV + KV_WIDTH
OFF_MV = OFF_MQ + 2 * MLSTM_WIDTH
OFF_MO = OFF_MV + MLSTM_WIDTH
OFF_G = OFF_MO + MLSTM_WIDTH
N_GATE_COLS = 4 * N_MLSTM_HEADS
N_MEM = 256
N_X_HEADS = 4
X_HEAD_DIM = D_MODEL // N_X_HEADS
N_EXPERTS = 16
N_GROUPS = 4
EXPERTS_PER_GROUP = N_EXPERTS // N_GROUPS
TOP_K = 2
D_FF_EXPERT = 512
ALPHA = (2.0 * DEPTH) ** 0.25
LN_EPS = 1e-5
RMS_EPS = 1e-6
NEG = -1e30

LANES = 128
TOKEN_TILE = 512
IN_PROJ_TILE = 1024
XATTN_TILE = 1024
ROW_BLOCK = 1024
VMEM_LIMIT = 56 * 1024 * 1024


def _params(*sem):
    return pltpu.CompilerParams(dimension_semantics=sem, vmem_limit_bytes=VMEM_LIMIT)


def _layer_norm(z, g, b):
    mu = jnp.mean(z, axis=-1, keepdims=True)
    zc = z - mu
    var = jnp.mean(zc * zc, axis=-1, keepdims=True)
    return zc * lax.rsqrt(var + LN_EPS) * g + b


def _dot(a, b):
    return jnp.dot(a, b, preferred_element_type=F32)


def _dot_nt(a, b):
    return lax.dot_general(a, b, (((1,), (1,)), ((), ())), preferred_element_type=F32)


def _dot_tn(a, b):
    return lax.dot_general(a, b, (((0,), (0,)), ((), ())), preferred_element_type=F32)


_X_HALO = 8


_W_MK = OFF_MQ
_W_END = _W_MK + MLSTM_WIDTH
_GP_ROWS = 3 * 2 * N_MLSTM_HEADS


def _in_proj_kernel(x_ref, xp_ref, xn_ref, w_ref, wt_ref, gb_ref, cwk_ref, cbk_ref, cq_ref,
                    cos_ref, sa_ref, sb_ref, tri_u_ref, tri_l_ref,
                    qa_ref, ka_ref, va_ref, mk_ref, mqt_ref, mvt_ref, mot_ref, gp_ref, rn_ref, *, n_seq_tiles):
    xb = x_ref[...].astype(BF16)
    cos = cos_ref[...]
    sa = sa_ref[...]
    sb = sb_ref[...]

    def mm(lo, hi):
        return _dot(xb, w_ref[:, lo:hi])

    def rope(t):
        return t * cos + pltpu.roll(t, LANES - ROT_DIM // 2, 1) * sa + pltpu.roll(t, ROT_DIM // 2, 1) * sb

    q = mm(OFF_AQ, OFF_AK)
    scale = math.log2(math.e) / math.sqrt(HEAD_DIM)
    for j in range(ATT_WIDTH // LANES):
        qa_ref[:, j * LANES:(j + 1) * LANES] = (rope(q[:, j * LANES:(j + 1) * LANES]) * scale).astype(BF16)
    kv = mm(OFF_AK, OFF_MQ)
    ka_ref[...] = rope(kv[:, :KV_WIDTH]).astype(BF16)
    va_ref[...] = kv[:, KV_WIDTH:].astype(BF16)
    tm = xb.shape[0]
    pos = pl.program_id(0) % n_seq_tiles
    has_prev = pos > 0
    has_next = pos < n_seq_tiles - 1
    halo = jnp.concatenate([xp_ref[...], xn_ref[...]], axis=0).astype(BF16)

    def silu(y):
        return y * (1.0 / (1.0 + jnp.exp(-y)))

    u = mm(_W_MK, _W_END)
    uh = _dot(halo, w_ref[:, _W_MK:_W_END])
    rowi = lax.broadcasted_iota(jnp.int32, u.shape, 0)
    u_prev = jnp.where(rowi == 0, jnp.where(has_prev, uh[_X_HALO - 1:_X_HALO, :], 0.0), pltpu.roll(u, 1, 0))
    u_next = jnp.where(rowi == tm - 1, jnp.where(has_next, uh[_X_HALO:_X_HALO + 1, :], 0.0),
                       pltpu.roll(u, tm - 1, 0))
    yk = silu(cwk_ref[0:1, :] * u_prev + cwk_ref[1:2, :] * u + cwk_ref[2:3, :] * u_next + cbk_ref[...])
    mk_ref[...] = (yk * (MLSTM_HEAD_DIM ** -0.5)).astype(BF16)

    feat = _dot_nt(wt_ref[...], xb)
    ut = feat[:MLSTM_WIDTH]
    uht = _dot_nt(wt_ref[0:MLSTM_WIDTH, :], halo)
    lanei = lax.broadcasted_iota(jnp.int32, ut.shape, 1)
    ut_prev = jnp.where(lanei == 0, jnp.where(has_prev, uht[:, _X_HALO - 1:_X_HALO], 0.0), pltpu.roll(ut, 1, 1))
    ut_next = jnp.where(lanei == tm - 1, jnp.where(has_next, uht[:, _X_HALO:_X_HALO + 1], 0.0),
                        pltpu.roll(ut, tm - 1, 1))
    cq = cq_ref[...]
    mqt_ref[...] = silu(cq[:, 0:1] * ut_prev + cq[:, 1:2] * ut + cq[:, 2:3] * ut_next + cq[:, 3:4]).astype(BF16)
    mvt_ref[...] = feat[MLSTM_WIDTH:2 * MLSTM_WIDTH].astype(BF16)
    mot_ref[...] = feat[2 * MLSTM_WIDTH:3 * MLSTM_WIDTH].astype(BF16)

    gt = feat[3 * MLSTM_WIDTH:] + gb_ref[...]
    half = 2 * N_MLSTM_HEADS
    gi, gf = gt[:half], gt[half:]
    ls = jnp.minimum(gf, 0.0) - jnp.log1p(jnp.exp(-jnp.abs(gf)))
    n_chunks = tm // CHUNK
    stack = lambda t: jnp.concatenate([t[:, c * CHUNK:(c + 1) * CHUNK] for c in range(n_chunks)], axis=0)
    ls_rows, gi_rows = stack(ls), stack(gi)
    pre = jnp.dot(ls_rows, tri_u_ref[...], precision=lax.Precision.HIGHEST, preferred_element_type=F32)
    suf = jnp.dot(ls_rows, tri_l_ref[...], precision=lax.Precision.HIGHEST, preferred_element_type=F32)
    is_fwd = (lax.broadcasted_iota(jnp.int32, ls_rows.shape, 0) % half) < N_MLSTM_HEADS
    lane = lax.broadcasted_iota(jnp.int32, ls_rows.shape, 1)
    b = jnp.where(is_fwd, pre, suf)
    r = gi_rows - b
    cm_f, cm_b = r, r
    k = 1
    while k < CHUNK:
        cm_f = jnp.maximum(cm_f, jnp.where(lane >= k, pltpu.roll(cm_f, k, 1), -jnp.inf))
        cm_b = jnp.maximum(cm_b, jnp.where(lane < CHUNK - k, pltpu.roll(cm_b, CHUNK - k, 1), -jnp.inf))
        k *= 2
    cm = jnp.where(is_fwd, cm_f, cm_b)
    log2e = math.log2(math.e)
    b, cm, r = b * log2e, cm * log2e, r * log2e
    pad = jnp.zeros((CHUNK - half, CHUNK), F32)
    for c in range(n_chunks):
        rows = slice(c * half, (c + 1) * half)
        gp_ref[c] = jnp.concatenate([b[rows], cm[rows], r[rows]], axis=0)
        rn_ref[c * CHUNK:(c + 1) * CHUNK, :] = jnp.concatenate([r[rows], pad], axis=0).T


def _in_proj(x, w_rows, w_t, gb, cwk, cbk, cq, rope_tabs, tri_u, tri_l, seq):
    T = x.shape[0]
    tm = IN_PROJ_TILE
    n_seq_tiles = seq // tm
    halo_per_tile = tm // _X_HALO
    n_halo = T // _X_HALO
    cos_t, sa_t, sb_t = rope_tabs
    row_spec = lambda w: pl.BlockSpec((tm, w), lambda i: (i, 0))
    full = lambda a: pl.BlockSpec(a.shape, lambda i: (0,) * a.ndim)
    tab_spec = pl.BlockSpec((tm, LANES), lambda i: (i % n_seq_tiles, 0))
    prev_spec = pl.BlockSpec((_X_HALO, D_MODEL), lambda i: (jnp.maximum(i * halo_per_tile - 1, 0), 0))
    next_spec = pl.BlockSpec((_X_HALO, D_MODEL), lambda i: (jnp.minimum((i + 1) * halo_per_tile, n_halo - 1), 0))
    widths = (ATT_WIDTH, KV_WIDTH, KV_WIDTH, MLSTM_WIDTH)
    col_spec = _feature_major_spec(MLSTM_WIDTH, tm, n_seq_tiles)
    feat_major = jax.ShapeDtypeStruct((MLSTM_SEQS, MLSTM_WIDTH, T // MLSTM_SEQS), BF16)
    out_shapes = tuple(jax.ShapeDtypeStruct((T, w), BF16) for w in widths) + (
        feat_major, feat_major, feat_major,
        jax.ShapeDtypeStruct((T // CHUNK, _GP_ROWS, CHUNK), F32), jax.ShapeDtypeStruct((T, LANES), F32))
    out_specs = tuple(row_spec(w) for w in widths) + (
        col_spec, col_spec, col_spec,
        pl.BlockSpec((tm // CHUNK, _GP_ROWS, CHUNK), lambda i: (i, 0, 0)), row_spec(LANES))
    return pl.pallas_call(
        functools.partial(_in_proj_kernel, n_seq_tiles=n_seq_tiles),
        grid=(T // tm,),
        in_specs=[row_spec(D_MODEL), prev_spec, next_spec, full(w_rows), full(w_t), full(gb),
                  full(cwk), full(cbk), full(cq), tab_spec, tab_spec, tab_spec, full(tri_u), full(tri_l)],
        out_specs=out_specs,
        out_shape=out_shapes,
        compiler_params=_params("parallel"),
        name="in_proj",
    )(x, x, x, w_rows, w_t, gb, cwk, cbk, cq, cos_t, sa_t, sb_t, tri_u, tri_l)


ATT_Q_TILE = 1024
_Q_BLOCKS = ATT_Q_TILE // BLOCK


def _attn_kernel(sink_ref, q_ref, kp_ref, k_ref, kn_ref, vp_ref, v_ref, vn_ref, g_ref, o_ref, *, n_tiles):
    i = pl.program_id(1)
    lane = lax.broadcasted_iota(jnp.int32, (ATT_Q_TILE + 2 * BLOCK, LANES), 1)
    low = lane < HEAD_DIM

    def split(prev_ref, own_ref, next_ref):
        t = jnp.concatenate([prev_ref[...], own_ref[...], next_ref[...]], axis=0).astype(F32)
        r = pltpu.roll(t, HEAD_DIM, 1)
        zero = jnp.zeros_like(t)
        lo = (jnp.where(low, t, zero).astype(BF16), jnp.where(low, r, zero).astype(BF16))
        hi = (jnp.where(low, zero, r).astype(BF16), jnp.where(low, zero, t).astype(BF16))
        return lo, hi

    k_lo, k_hi = split(kp_ref, k_ref, kn_ref)
    v_lo, v_hi = split(vp_ref, v_ref, vn_ref)

    rowi = lax.broadcasted_iota(jnp.int32, (BLOCK, BLOCK), 0)
    coli = lax.broadcasted_iota(jnp.int32, (BLOCK, BLOCK), 1)
    prev_bias = jnp.where(coli < rowi, NEG, 0.0).astype(F32)
    next_bias = jnp.where(coli > rowi, NEG, 0.0).astype(F32)
    first_bias = jnp.where(i == 0, NEG, 0.0).astype(F32)
    last_bias = jnp.where(i == n_tiles - 1, NEG, 0.0).astype(F32)
    lane_o = lax.broadcasted_iota(jnp.int32, (BLOCK, LANES), 1)
    g = g_ref[...]
    log2e = math.log2(math.e)

    for r in range(_Q_BLOCKS):
        pb = prev_bias + first_bias if r == 0 else prev_bias
        nb = next_bias + last_bias if r == _Q_BLOCKS - 1 else next_bias
        rows = slice(r * BLOCK, (r + 1) * BLOCK)
        win = slice(r * BLOCK, (r + 3) * BLOCK)
        tiles = []
        for c in range(N_KV_HEADS):
            q2 = jnp.concatenate([q_ref[rows, (2 * c) * LANES:(2 * c + 1) * LANES],
                                  q_ref[rows, (2 * c + 1) * LANES:(2 * c + 2) * LANES]], axis=0)
            kc = jnp.concatenate([k_lo[c][win], k_hi[c][win]], axis=0)
            vc = jnp.concatenate([v_lo[c][win], v_hi[c][win]], axis=0)
            s = _dot_nt(q2, kc)
            p_rows, inv_rows = [], []
            for t in range(2):
                ps, invs = [], []
                for hh in range(2):
                    head = 4 * c + 2 * t + hh
                    sink = sink_ref[head] * log2e
                    blk = lambda j: s[t * BLOCK:(t + 1) * BLOCK, (3 * hh + j) * BLOCK:(3 * hh + j + 1) * BLOCK]
                    sp, so, sn = blk(0) + pb, blk(1), blk(2) + nb
                    m = jnp.maximum(jnp.max(jnp.maximum(jnp.maximum(sp, so), sn), axis=-1, keepdims=True), sink)
                    pp, po, pn = jnp.exp2(sp - m), jnp.exp2(so - m), jnp.exp2(sn - m)
                    den = jnp.sum(pp + po + pn, axis=-1, keepdims=True) + jnp.exp2(sink - m)
                    ps += [pp.astype(BF16), po.astype(BF16), pn.astype(BF16)]
                    invs.append(1.0 / den)
                p_rows.append(jnp.concatenate(ps, axis=1))
                inv_rows.append(jnp.where(lane_o < HEAD_DIM, invs[0], invs[1]))
            o2 = _dot(jnp.concatenate(p_rows, axis=0), vc)
            tiles.append(o2[:BLOCK] * inv_rows[0])
            tiles.append(o2[BLOCK:] * inv_rows[1])
        o = jnp.concatenate(tiles, axis=1)
        ms = jnp.mean(o * o, axis=-1, keepdims=True)
        o_ref[rows, :] = (o * lax.rsqrt(ms + RMS_EPS) * g).astype(BF16)


def _attention(qa, ka, va, sink, att_g, batch, seq):
    T = qa.shape[0]
    n_tiles = seq // ATT_Q_TILE
    blocks_per_seq = seq // BLOCK
    own = lambda w: pl.BlockSpec((ATT_Q_TILE, w), lambda b, i: (b * n_tiles + i, 0))
    prev = pl.BlockSpec((BLOCK, KV_WIDTH),
                        lambda b, i: (b * blocks_per_seq + jnp.maximum(i * _Q_BLOCKS - 1, 0), 0))
    nxt = pl.BlockSpec((BLOCK, KV_WIDTH),
                       lambda b, i: (b * blocks_per_seq + jnp.minimum((i + 1) * _Q_BLOCKS, blocks_per_seq - 1), 0))
    return pl.pallas_call(
        functools.partial(_attn_kernel, n_tiles=n_tiles),
        grid=(batch, n_tiles),
        in_specs=[pl.BlockSpec(memory_space=pltpu.SMEM), own(ATT_WIDTH), prev, own(KV_WIDTH), nxt,
                  prev, own(KV_WIDTH), nxt, pl.BlockSpec((1, ATT_WIDTH), lambda b, i: (0, 0))],
        out_specs=own(ATT_WIDTH),
        out_shape=jax.ShapeDtypeStruct((T, ATT_WIDTH), BF16),
        compiler_params=_params("parallel", "parallel"),
        name="band_attention",
    )(sink, qa, ka, ka, ka, va, va, va, att_g)


MLSTM_SEQS = 4
_STATE_ROWS = MLSTM_HEAD_DIM + 8


def _mlstm_kernel(qt_f, vt_f, k_f, rn_f, gp_f, qt_b, vt_b, k_b, rn_b, gp_b, of_ref, ob_ref, c_state, m_state):
    @pl.when(pl.program_id(1) == 0)
    def _():
        c_state[...] = jnp.zeros_like(c_state)
        m_state[...] = jnp.zeros_like(m_state)

    key = lax.broadcasted_iota(jnp.int32, (CHUNK, CHUNK), 0)
    qry = lax.broadcasted_iota(jnp.int32, (CHUNK, CHUNK), 1)
    nh = N_MLSTM_HEADS

    units = []
    for sq in range(MLSTM_SEQS):
        for fwd, qt_ref, vt_ref, k_ref, rn_ref, gp_ref, o_ref in ((True, qt_f, vt_f, k_f, rn_f, gp_f, of_ref),
                                                                 (False, qt_b, vt_b, k_b, rn_b, gp_b, ob_ref)):
            gates = gp_ref[sq, 0]
            off = 0 if fwd else nh
            a_pos = CHUNK - 1 if fwd else 0
            for h in range(nh):
                hs = slice(h * MLSTM_HEAD_DIM, (h + 1) * MLSTM_HEAD_DIM)
                b = gates[off + h:off + h + 1, :]
                st = (2 * sq + (0 if fwd else 1)) * nh + h
                units.append(dict(
                    sq=sq, hs=hs, st=st, o_ref=o_ref, qt_ref=qt_ref, vt_ref=vt_ref, k_ref=k_ref,
                    visible=(key <= qry) if fwd else (key >= qry),
                    b=b, cm=gates[2 * nh + off + h:2 * nh + off + h + 1, :],
                    r_row=gates[4 * nh + off + h:4 * nh + off + h + 1, :],
                    r_keys=jnp.broadcast_to(rn_ref[sq, :, off + h:off + h + 1], (CHUNK, CHUNK)),
                    a=b[:, a_pos:a_pos + 1],
                    m_in=m_state[st:st + 1, 0:1]))

    for u in units:
        inter_log = u["b"] + u["m_in"]
        m_t = jnp.maximum(inter_log, u["b"] + u["cm"])
        u["decay"] = jnp.exp2(jnp.where(u["visible"], u["r_keys"] + (u["b"] - m_t), NEG))
        u["inter_w"] = jnp.exp2(inter_log - m_t)
        u["floor"] = jnp.exp2(-m_t)
    for u in units:
        qt = u["qt_ref"][u["sq"], u["hs"], :]
        u["k"] = u["k_ref"][u["sq"], :, u["hs"]]
        u["vt"] = u["vt_ref"][u["sq"], u["hs"], :]
        u["scores"] = _dot(u["k"], qt)
        u["c_in"] = c_state[u["st"]]
        u["inter"] = _dot(u["c_in"].astype(BF16), qt)
    for u in units:
        sw = u["scores"] * u["decay"]
        num = _dot(u["vt"], sw.astype(BF16)) + u["inter_w"] * u["inter"][:MLSTM_HEAD_DIM]
        den = (jnp.sum(sw, axis=0, keepdims=True)
               + u["inter_w"] * u["inter"][MLSTM_HEAD_DIM:MLSTM_HEAD_DIM + 1])
        u["o_ref"][u["sq"], u["hs"], :] = (num / jnp.maximum(jnp.abs(den), u["floor"])).astype(u["o_ref"].dtype)
    for u in units:
        a, m_in, st = u["a"], u["m_in"], u["st"]
        g_max = a + jnp.max(u["r_row"], axis=-1, keepdims=True)
        kw = u["k"].astype(F32) * jnp.exp2(u["r_keys"] + (a - g_max))
        m_new = jnp.maximum(a + m_in, g_max)
        keep = jnp.exp2(a + m_in - m_new)
        add = jnp.exp2(g_max - m_new)
        c_state[st, 0:MLSTM_HEAD_DIM, :] = keep * u["c_in"][:MLSTM_HEAD_DIM] + add * _dot(u["vt"], kw.astype(BF16))
        c_state[st, MLSTM_HEAD_DIM:MLSTM_HEAD_DIM + 1, :] = (
            keep * u["c_in"][MLSTM_HEAD_DIM:MLSTM_HEAD_DIM + 1] + add * jnp.sum(kw, axis=0, keepdims=True))
        m_state[st:st + 1, :] = jnp.broadcast_to(m_new, (1, LANES))


def _feature_major_spec(width, tm, tiles_per_seq):
    def index(i):
        s = i // tiles_per_seq
        return (s % MLSTM_SEQS, 0, (s // MLSTM_SEQS) * tiles_per_seq + i % tiles_per_seq)
    return pl.BlockSpec((None, width, tm), index)


def _mlstm(mqt, mvt, mk, rn, gp, batch, seq):
    T = mk.shape[0]
    nc = seq // CHUNK
    groups = batch // MLSTM_SEQS
    mk = mk.reshape(groups, MLSTM_SEQS, seq, MLSTM_WIDTH)
    rn = rn.reshape(groups, MLSTM_SEQS, seq, LANES)
    gp = gp.reshape(groups, MLSTM_SEQS, nc, _GP_ROWS, CHUNK)

    def specs(chunk_of):
        feat = pl.BlockSpec((MLSTM_SEQS, MLSTM_WIDTH, CHUNK), lambda b, i: (0, 0, b * nc + chunk_of(i)))
        return [feat, feat,
                pl.BlockSpec((None, MLSTM_SEQS, CHUNK, MLSTM_WIDTH), lambda b, i: (b, 0, chunk_of(i), 0)),
                pl.BlockSpec((None, MLSTM_SEQS, CHUNK, LANES), lambda b, i: (b, 0, chunk_of(i), 0)),
                pl.BlockSpec((None, MLSTM_SEQS, 1, _GP_ROWS, CHUNK), lambda b, i: (b, 0, chunk_of(i), 0, 0))]

    fwd_chunk = lambda i: i
    bwd_chunk = lambda i: nc - 1 - i
    out = jax.ShapeDtypeStruct((MLSTM_SEQS, MLSTM_WIDTH, T // MLSTM_SEQS), BF16)
    n_state = 2 * MLSTM_SEQS * N_MLSTM_HEADS
    return pl.pallas_call(
        _mlstm_kernel,
        grid=(groups, nc),
        in_specs=specs(fwd_chunk) + specs(bwd_chunk),
        out_specs=(specs(fwd_chunk)[0], specs(bwd_chunk)[0]),
        out_shape=(out, out),
        scratch_shapes=[pltpu.VMEM((n_state, _STATE_ROWS, MLSTM_HEAD_DIM), F32),
                        pltpu.VMEM((n_state, LANES), F32)],
        compiler_params=_params("parallel", "arbitrary"),
        name="mlstm",
    )(mqt, mvt, mk, rn, gp, mqt, mvt, mk, rn, gp)


def _mixer_out(x, att_ref, hf_ref, hb_ref, mo_ref, mg_ref, wa_ref, wm_ref, g_ref, b_ref):
    h = hf_ref[...].astype(F32) + hb_ref[...].astype(F32)
    parts = []
    for hd in range(N_MLSTM_HEADS):
        hh = h[hd * MLSTM_HEAD_DIM:(hd + 1) * MLSTM_HEAD_DIM]
        ms = jnp.mean(hh * hh, axis=0, keepdims=True)
        parts.append(hh * lax.rsqrt(ms + RMS_EPS))
    hn = jnp.concatenate(parts, axis=0) * mg_ref[...]
    gate = 1.0 / (1.0 + jnp.exp(-mo_ref[...].astype(F32)))
    y = _dot(att_ref[...], wa_ref[...]) + _dot_tn((hn * gate).astype(BF16), wm_ref[...])
    return _layer_norm(ALPHA * x + y, g_ref[...], b_ref[...])


def _kv_proj_kernel(m_ref, w_ref, k_ref, v_ref):
    mb = m_ref[...].astype(BF16)
    k_ref[...] = _dot(mb, w_ref[:, :D_MODEL]).astype(BF16)
    v_ref[...] = _dot(mb, w_ref[:, D_MODEL:]).astype(BF16)


def _kv_proj(mem, wkv):
    M = mem.shape[0]
    tm = TOKEN_TILE
    row = pl.BlockSpec((tm, D_MODEL), lambda i: (i, 0))
    out = jax.ShapeDtypeStruct((M, D_MODEL), BF16)
    return pl.pallas_call(
        _kv_proj_kernel,
        grid=(M // tm,),
        in_specs=[row, pl.BlockSpec(wkv.shape, lambda i: (0, 0))],
        out_specs=(row, row),
        out_shape=(out, out),
        compiler_params=_params("parallel"),
        name="mem_kv_proj",
    )(mem, wkv)


def _xattn_kernel(x_ref, att_ref, hf_ref, hb_ref, mo_ref, mg_ref, wa_ref, wm_ref, g1_ref, b1_ref,
                  k_ref, v_ref, wq_ref, wo_ref, g_ref, b_ref, rwt_ref, rb_ref, tri_ref, low_ref,
                  o_ref, info_ref, cnt_ref):
    subs = [slice(t * TOKEN_TILE, (t + 1) * TOKEN_TILE) for t in range(x_ref.shape[0] // TOKEN_TILE)]
    xs = [_mixer_out(x_ref[rows], att_ref.at[rows], hf_ref.at[:, rows], hb_ref.at[:, rows], mo_ref.at[:, rows],
                     mg_ref, wa_ref, wm_ref, g1_ref, b1_ref) for rows in subs]
    scale = math.log2(math.e) / math.sqrt(X_HEAD_DIM)
    qs = [_dot(x.astype(BF16), wq_ref[...]) for x in xs]
    outs = [[] for _ in subs]
    for h in range(N_X_HEADS):
        hs = slice(h * X_HEAD_DIM, (h + 1) * X_HEAD_DIM)
        for t, q in enumerate(qs):
            s = _dot_nt((q[:, hs] * scale).astype(BF16), k_ref[:, hs])
            p = jnp.exp2(s - jnp.max(s, axis=-1, keepdims=True))
            inv = 1.0 / jnp.sum(p, axis=-1, keepdims=True)
            outs[t].append((_dot(p.astype(BF16), v_ref[:, hs]) * inv).astype(BF16))
    ys = [_dot(jnp.concatenate(o, axis=1), wo_ref[...]) for o in outs]
    for t, rows in enumerate(subs):
        x2 = _layer_norm(ALPHA * xs[t] + ys[t], g_ref[...], b_ref[...])
        o_ref[rows, :] = x2
        _route(x2, rwt_ref, rb_ref, tri_ref, low_ref, info_ref.at[:, rows], cnt_ref.at[t:t + 1])


def _mixer_out_xattn(x, att, hft, hbt, mot, mg, wa, wm, g1, b1, k_mem, v_mem, wq, wo, g, b,
                     router_wt, router_b, tri_strict, low_strict, batch, seq):
    T = x.shape[0]
    tm = XATTN_TILE
    nt = seq // tm
    row = pl.BlockSpec((tm, D_MODEL), lambda bb, i: (bb * nt + i, 0))
    att_row = pl.BlockSpec((tm, ATT_WIDTH), lambda bb, i: (bb * nt + i, 0))
    flat_feat = _feature_major_spec(MLSTM_WIDTH, tm, nt)
    feat = pl.BlockSpec(flat_feat.block_shape, lambda bb, i: flat_feat.index_map(bb * nt + i))
    mem = pl.BlockSpec((N_MEM, D_MODEL), lambda bb, i: (bb, 0))
    full = lambda a: pl.BlockSpec(a.shape, lambda bb, i: (0,) * a.ndim)
    return pl.pallas_call(
        _xattn_kernel,
        grid=(batch, nt),
        in_specs=[row, att_row, feat, feat, feat, full(mg), full(wa), full(wm), full(g1), full(b1),
                  mem, mem, full(wq), full(wo), full(g), full(b), full(router_wt), full(router_b),
                  full(tri_strict), full(low_strict)],
        out_specs=(row, pl.BlockSpec((_IDX_ROWS, tm), lambda bb, i: (0, bb * nt + i)),
                   pl.BlockSpec((tm // TOKEN_TILE, N_EXPERTS, LANES), lambda bb, i: (bb * nt + i, 0, 0))),
        out_shape=(jax.ShapeDtypeStruct((T, D_MODEL), F32), jax.ShapeDtypeStruct((_IDX_ROWS, T), F32),
                   jax.ShapeDtypeStruct((T // TOKEN_TILE, N_EXPERTS, LANES), jnp.int32)),
        compiler_params=_params("parallel", "parallel"),
        name="mixer_out_xattn_route",
    )(x, att, hft, hbt, mot, mg, wa, wm, g1, b1, k_mem, v_mem, wq, wo, g, b, router_wt, router_b,
      tri_strict, low_strict)


_IDX_ROWS = 8
SLAB = 16
TILE_SLOTS = TOP_K * TOKEN_TILE + N_EXPERTS * SLAB


def _route(x, wt_ref, b_ref, tri_ref, low_ref, info_ref, cnt_ref):
    x_hi = x.astype(BF16)
    x_lo = (x - x_hi.astype(F32)).astype(BF16)
    by_hi = _dot_nt(wt_ref[...], x_hi)
    logits = by_hi[:N_EXPERTS] + by_hi[N_EXPERTS:] + _dot_nt(wt_ref[0:N_EXPERTS, :], x_lo)
    s = 1.0 / (1.0 + jnp.exp(-logits))
    sel = s + b_ref[...]
    srow = lambda e: s[e:e + 1, :]
    brow = lambda e: sel[e:e + 1, :]
    best = None
    gi = None
    for gidx in range(N_GROUPS):
        vals = [brow(gidx * EXPERTS_PER_GROUP + j) for j in range(EXPERTS_PER_GROUP)]
        top2 = None
        for a in range(EXPERTS_PER_GROUP):
            for b in range(a + 1, EXPERTS_PER_GROUP):
                pair = vals[a] + vals[b]
                top2 = pair if top2 is None else jnp.maximum(top2, pair)
        if best is None:
            best, gi = top2, jnp.zeros(top2.shape, jnp.int32)
        else:
            better = top2 > best
            gi = jnp.where(better, gidx, gi)
            best = jnp.where(better, top2, best)

    def in_group(rowfn, j):
        out = rowfn(j)
        for gidx in range(1, N_GROUPS):
            out = jnp.where(gi == gidx, rowfn(gidx * EXPERTS_PER_GROUP + j), out)
        return out

    bv = [in_group(brow, j) for j in range(EXPERTS_PER_GROUP)]
    sv = [in_group(srow, j) for j in range(EXPERTS_PER_GROUP)]

    def argmax_first(vals):
        bi = jnp.zeros(vals[0].shape, jnp.int32)
        bm = vals[0]
        for j in range(1, len(vals)):
            better = vals[j] > bm
            bi = jnp.where(better, j, bi)
            bm = jnp.where(better, vals[j], bm)
        return bi

    i1 = argmax_first(bv)
    i2 = argmax_first([jnp.where(i1 == j, -jnp.inf, bv[j]) for j in range(EXPERTS_PER_GROUP)])

    def pick(vals, idx):
        out = vals[0]
        for j in range(1, len(vals)):
            out = jnp.where(idx == j, vals[j], out)
        return out

    w1 = pick(sv, i1)
    w2 = pick(sv, i2)
    tot = w1 + w2
    tm = logits.shape[1]
    eid = lax.broadcasted_iota(jnp.int32, (N_EXPERTS, tm), 0)
    oh1 = (eid == gi * EXPERTS_PER_GROUP + i1).astype(F32)
    oh2 = (eid == gi * EXPERTS_PER_GROUP + i2).astype(F32)
    before1 = _dot(oh1.astype(BF16), tri_ref[...])
    before2 = _dot(oh2.astype(BF16), tri_ref[...])
    c1 = jnp.sum(oh1, axis=1, keepdims=True)
    cnt = c1 + jnp.sum(oh2, axis=1, keepdims=True)
    rows = jnp.floor((cnt + (SLAB - 1)) * (1.0 / SLAB)) * SLAB
    rows_b = jnp.broadcast_to(rows, (N_EXPERTS, LANES))
    start = jnp.dot(low_ref[...], rows_b, precision=lax.Precision.HIGHEST, preferred_element_type=F32)[:, 0:1]
    slot1 = jnp.sum(oh1 * (start + before1), axis=0, keepdims=True)
    slot2 = jnp.sum(oh2 * (start + c1 + before2), axis=0, keepdims=True)
    zf = jnp.zeros((_IDX_ROWS - 2 * TOP_K, tm), F32)
    info_ref[...] = jnp.concatenate([w1 / tot, w2 / tot, slot1, slot2, zf], axis=0)
    cnt_ref[0] = rows_b.astype(jnp.int32)


_BIG_SLAB = 4 * SLAB


def _slab_copies(i, gs_ref, ls_ref, rc_ref, make_copy, op):
    for e in range(N_EXPERTS):
        idx = i * N_EXPERTS + e
        ls = ls_ref[idx]
        gs = gs_ref[idx]
        n_big = rc_ref[idx] // _BIG_SLAB
        done = n_big * _BIG_SLAB

        def big(j, carry, ls=ls, gs=gs):
            op(make_copy(pl.multiple_of(ls + j * _BIG_SLAB, SLAB), pl.multiple_of(gs + j * _BIG_SLAB, SLAB),
                         _BIG_SLAB))
            return carry

        def small(j, carry, ls=ls + done, gs=gs + done):
            op(make_copy(pl.multiple_of(ls + j * SLAB, SLAB), pl.multiple_of(gs + j * SLAB, SLAB), SLAB))
            return carry

        lax.fori_loop(0, n_big, big, 0)
        lax.fori_loop(0, (rc_ref[idx] - done) // SLAB, small, 0)


_ZERO_ROWS = 128


def _dispatch_kernel(gs_ref, ls_ref, rc_ref, ts_ref, tn_ref, x_ref, info_ref, xp_ref, xs_ref, zero_ref, sem, zsem):
    i = pl.program_id(0)
    last = pl.num_programs(0) - 1
    buf = i % 2
    slot0 = info_ref[2:3, :].astype(jnp.int32)
    slot1 = info_ref[3:4, :].astype(jnp.int32)
    xb = x_ref[...].astype(BF16)
    tm = xb.shape[0]
    sid = lax.broadcasted_iota(jnp.int32, (TILE_SLOTS, tm), 0)
    sel = jnp.where((sid == slot0) | (sid == slot1), 1.0, 0.0).astype(BF16)
    xs_ref[buf] = _dot(sel, xb).astype(BF16)

    def copies(tile, which, op):
        def make_copy(tile_row, global_row, rows):
            return pltpu.make_async_copy(xs_ref.at[which, pl.ds(tile_row, rows), :],
                                         xp_ref.at[pl.ds(global_row, rows), :], sem.at[which])
        _slab_copies(tile, gs_ref, ls_ref, rc_ref, make_copy, op)

    @pl.when(i > 0)
    def _():
        copies(i - 1, 1 - buf, lambda cp: cp.wait())

    copies(i, buf, lambda cp: cp.start())

    @pl.when(i == last)
    def _():
        zero_ref[...] = jnp.zeros_like(zero_ref)

        def tail(op):
            for e in range(N_EXPERTS + 1):
                t0 = ts_ref[e]
                rows = SLAB if e < N_EXPERTS else _ZERO_ROWS

                def body(j, carry, t0=t0, rows=rows):
                    op(pltpu.make_async_copy(zero_ref.at[pl.ds(0, rows), :],
                                             xp_ref.at[pl.ds(pl.multiple_of(t0 + j * rows, rows), rows), :], zsem))
                    return carry

                lax.fori_loop(0, tn_ref[e] // rows, body, 0)

        tail(lambda cp: cp.start())
        copies(i, buf, lambda cp: cp.wait())
        tail(lambda cp: cp.wait())


def _dispatch(x, info, gstart, lstart, rc, tail_start, tail_rows, n_rows):
    T = x.shape[0]
    tm = TOKEN_TILE
    grid_spec = pltpu.PrefetchScalarGridSpec(
        num_scalar_prefetch=5,
        grid=(T // tm,),
        in_specs=[pl.BlockSpec((tm, D_MODEL), lambda i, *_: (i, 0)),
                  pl.BlockSpec((_IDX_ROWS, tm), lambda i, *_: (0, i))],
        out_specs=pl.BlockSpec(memory_space=pl.ANY),
        scratch_shapes=[pltpu.VMEM((2, TILE_SLOTS, D_MODEL), BF16), pltpu.VMEM((_ZERO_ROWS, D_MODEL), BF16),
                        pltpu.SemaphoreType.DMA((2,)), pltpu.SemaphoreType.DMA],
    )
    return pl.pallas_call(
        _dispatch_kernel,
        grid_spec=grid_spec,
        out_shape=jax.ShapeDtypeStruct((n_rows, D_MODEL), BF16),
        compiler_params=_params("arbitrary"),
        name="moe_dispatch",
    )(gstart, lstart, rc, tail_start, tail_rows, x, info)


def _ffn_kernel(be_ref, nu_ref, x_ref, wg_ref, wu_ref, wd_ref, o_ref):
    i = pl.program_id(0)

    @pl.when(i < nu_ref[0])
    def _():
        xb = x_ref[...]
        gate = _dot(xb, wg_ref[...])
        up = _dot(xb, wu_ref[...])
        h = gate * (1.0 / (1.0 + jnp.exp(-gate))) * up
        o_ref[...] = _dot(h.astype(BF16), wd_ref[...]).astype(BF16)

    @pl.when(i >= nu_ref[0])
    def _():
        o_ref[...] = jnp.zeros_like(o_ref)


def _expert_ffn(xp, blk_e, n_used, w_gate, w_up, w_down, layer):
    P = xp.shape[0]
    wspec = lambda shp: pl.BlockSpec((None, None) + shp, lambda i, be, nu: (layer, be[i], 0, 0))
    grid_spec = pltpu.PrefetchScalarGridSpec(
        num_scalar_prefetch=2,
        grid=(P // ROW_BLOCK,),
        in_specs=[pl.BlockSpec((ROW_BLOCK, D_MODEL), lambda i, be, nu: (jnp.minimum(i, nu[0] - 1), 0)),
                  wspec((D_MODEL, D_FF_EXPERT)), wspec((D_MODEL, D_FF_EXPERT)), wspec((D_FF_EXPERT, D_MODEL))],
        out_specs=pl.BlockSpec((ROW_BLOCK, D_MODEL), lambda i, be, nu: (i, 0)),
    )
    return pl.pallas_call(
        _ffn_kernel,
        grid_spec=grid_spec,
        out_shape=jax.ShapeDtypeStruct((P, D_MODEL), BF16),
        compiler_params=_params("arbitrary"),
        name="expert_ffn",
    )(blk_e, n_used, xp, w_gate, w_up, w_down)


def _combine_kernel(gs_ref, ls_ref, rc_ref, x_ref, info_ref, eye_ref, g_ref, b_ref, yp_ref, o_ref, ys_ref, sem):
    i = pl.program_id(0)
    buf = i % 2

    def copies(tile, which, op):
        def make_copy(tile_row, global_row, rows):
            return pltpu.make_async_copy(yp_ref.at[pl.ds(global_row, rows), :],
                                         ys_ref.at[which, pl.ds(tile_row, rows), :], sem.at[which])
        _slab_copies(tile, gs_ref, ls_ref, rc_ref, make_copy, op)

    @pl.when(i == 0)
    def _():
        ys_ref[...] = jnp.zeros_like(ys_ref)
        copies(i, buf, lambda cp: cp.start())

    @pl.when(i + 1 < pl.num_programs(0))
    def _():
        copies(i + 1, 1 - buf, lambda cp: cp.start())

    cols = lax.dot_general(info_ref[...], eye_ref[...], (((0,), (0,)), ((), ())),
                           precision=lax.Precision.HIGHEST, preferred_element_type=F32)
    sid = lax.broadcasted_iota(jnp.int32, (cols.shape[0], TILE_SLOTS), 1)
    weights = (jnp.where(sid == cols[:, 2:3].astype(jnp.int32), cols[:, 0:1], 0.0)
               + jnp.where(sid == cols[:, 3:4].astype(jnp.int32), cols[:, 1:2], 0.0)).astype(BF16)
    copies(i, buf, lambda cp: cp.wait())
    y = _dot(weights, ys_ref[buf])
    o_ref[...] = _layer_norm(ALPHA * x_ref[...] + y, g_ref[...], b_ref[...])


def _combine_ln(x, yp, info, gstart, lstart, rc, eye, g, b):
    T = x.shape[0]
    tm = TOKEN_TILE
    full = lambda a: pl.BlockSpec(a.shape, lambda i, *_: (0,) * a.ndim)
    grid_spec = pltpu.PrefetchScalarGridSpec(
        num_scalar_prefetch=3,
        grid=(T // tm,),
        in_specs=[pl.BlockSpec((tm, D_MODEL), lambda i, *_: (i, 0)),
                  pl.BlockSpec((_IDX_ROWS, tm), lambda i, *_: (0, i)), full(eye), full(g), full(b),
                  pl.BlockSpec(memory_space=pl.ANY)],
        out_specs=pl.BlockSpec((tm, D_MODEL), lambda i, *_: (i, 0)),
        scratch_shapes=[pltpu.VMEM((2, TILE_SLOTS, D_MODEL), BF16), pltpu.SemaphoreType.DMA((2,))],
    )
    return pl.pallas_call(
        _combine_kernel,
        grid_spec=grid_spec,
        out_shape=jax.ShapeDtypeStruct((T, D_MODEL), F32),
        compiler_params=_params("arbitrary"),
        name="moe_combine_ln3",
    )(gstart, lstart, rc, x, info, eye, g, b, yp)


def _moe(x, info, rows, consts, w_gate, w_up, w_down, layer, g, b):
    T = x.shape[0]
    n_tiles = T // TOKEN_TILE
    P = T * TOP_K + n_tiles * N_EXPERTS * SLAB + N_EXPERTS * ROW_BLOCK
    rc = rows[:, :, 0]
    lstart = jnp.cumsum(rc, axis=1) - rc
    region = (jnp.sum(rc, axis=0) + ROW_BLOCK - 1) // ROW_BLOCK * ROW_BLOCK
    region_end = jnp.cumsum(region)
    gstart = (region_end - region)[None, :] + jnp.cumsum(rc, axis=0) - rc
    n_blk = P // ROW_BLOCK
    blk_row0 = jnp.arange(n_blk, dtype=jnp.int32) * ROW_BLOCK
    blk_e = jnp.minimum(jnp.sum(region_end[None, :] <= blk_row0[:, None], axis=1), N_EXPERTS - 1).astype(jnp.int32)
    n_used = (region_end[-1:] // ROW_BLOCK).astype(jnp.int32)
    flat = lambda t: t.reshape(-1).astype(jnp.int32)
    used = jnp.sum(rc, axis=0)
    tail_start = flat(jnp.concatenate([region_end - region + used, region_end[-1:]]))
    tail_rows = flat(jnp.concatenate([region - used, P - region_end[-1:]]))
    gstart, lstart, rc = flat(gstart), flat(lstart), flat(rc)
    xp = _dispatch(x, info, gstart, lstart, rc, tail_start, tail_rows, P)
    yp = _expert_ffn(xp, blk_e, n_used, w_gate, w_up, w_down, layer)
    return _combine_ln(x, yp, info, gstart, lstart, rc, consts["eye"], g, b)


def _rope_tables(seq):
    inv_freq = ROPE_THETA ** (-jnp.arange(0, ROT_DIM, 2, dtype=F32) / ROT_DIM)
    ang = jnp.arange(seq, dtype=F32)[:, None] * inv_freq[None, :]
    cos, sin = jnp.cos(ang), jnp.sin(ang)
    half = ROT_DIM // 2
    one = jnp.ones((seq, HEAD_DIM - ROT_DIM), F32)
    zero = jnp.zeros((seq, HEAD_DIM - ROT_DIM), F32)
    zh = jnp.zeros((seq, half), F32)
    cos_h = jnp.concatenate([cos, cos, one], axis=1)
    sa_h = jnp.concatenate([-sin, zh, zero], axis=1)
    sb_h = jnp.concatenate([zh, sin, zero], axis=1)
    rep = lambda t: jnp.concatenate([t] * (LANES // HEAD_DIM), axis=1)
    return rep(cos_h), rep(sa_h), rep(sb_h)


def _constants():
    r = lax.broadcasted_iota(jnp.int32, (CHUNK, CHUNK), 0)
    c = lax.broadcasted_iota(jnp.int32, (CHUNK, CHUNK), 1)
    rr = lax.broadcasted_iota(jnp.int32, (TOKEN_TILE, TOKEN_TILE), 0)
    cc = lax.broadcasted_iota(jnp.int32, (TOKEN_TILE, TOKEN_TILE), 1)
    er = lax.broadcasted_iota(jnp.int32, (N_EXPERTS, N_EXPERTS), 0)
    ec = lax.broadcasted_iota(jnp.int32, (N_EXPERTS, N_EXPERTS), 1)
    return {
        "tri_u": (r <= c).astype(F32),
        "tri_l": (r >= c).astype(F32),
        "tri_strict": (rr < cc).astype(BF16),
        "low_strict": (ec < er).astype(F32),
        "eye": (lax.broadcasted_iota(jnp.int32, (_IDX_ROWS, LANES), 0)
                == lax.broadcasted_iota(jnp.int32, (_IDX_ROWS, LANES), 1)).astype(F32),
    }


def _trunk(x, mem, wts, consts):
    batch, seq, _ = x.shape
    T = batch * seq
    x = x.reshape(T, D_MODEL)
    mem2 = mem.reshape(batch * N_MEM, D_MODEL)
    rope_tabs = _rope_tables(seq)
    for l in range(DEPTH):
        qa, ka, va, mk, mqt, mvt, mot, gp, rn = _in_proj(
            x, wts["w_rows"][l], wts["w_t"][l], wts["gb"][l], wts["conv_wk"][l], wts["conv_bk"][l],
            wts["conv_q"][l], rope_tabs, consts["tri_u"], consts["tri_l"], seq)
        att = _attention(qa, ka, va, wts["sink"][l], wts["att_g"][l], batch, seq)
        hft, hbt = _mlstm(mqt, mvt, mk, rn, gp, batch, seq)
        k_mem, v_mem = _kv_proj(mem2, wts["wkv"][l])
        x, info, rows = _mixer_out_xattn(
            x, att, hft, hbt, mot, wts["mlstm_g"][l], wts["w_out_a"][l], wts["w_out_m"][l], wts["ln1_g"][l],
            wts["ln1_b"][l], k_mem, v_mem, wts["wq"][l], wts["wo"][l], wts["ln2_g"][l], wts["ln2_b"][l],
            wts["router_wt"], wts["router_b"], consts["tri_strict"], consts["low_strict"], batch, seq)
        x = _moe(x, info, rows, consts, wts["w_gate"], wts["w_up"], wts["w_down"], l,
                 wts["ln3_g"][l], wts["ln3_b"][l])
    return x.reshape(batch, seq, D_MODEL)


def _hi_lo_rows(w):
    hi = w.astype(BF16)
    return jnp.concatenate([hi, (w - hi.astype(F32)).astype(BF16)], axis=0)


def _prepare_weights(w_in, gate_bias, conv_w, conv_b, attn_sink, attn_norm_g, mlstm_norm_g, w_out, ln1_g, ln1_b,
                     wq_mem, wkv_mem, wo_mem, ln2_g, ln2_b, router_w, router_bias, w_gate, w_up, w_down, ln3_g, ln3_b):
    row = lambda t: t.astype(F32).reshape(DEPTH, 1, t.shape[-1])
    nh = N_MLSTM_HEADS
    gate_order = jnp.array(list(range(0, nh)) + list(range(2 * nh, 3 * nh))
                           + list(range(nh, 2 * nh)) + list(range(3 * nh, 4 * nh)), jnp.int32)
    cq = jnp.concatenate([jnp.swapaxes(conv_w[:, :, :MLSTM_WIDTH], 1, 2), conv_b[:, :MLSTM_WIDTH, None],
                          jnp.zeros((DEPTH, MLSTM_WIDTH, 4), conv_w.dtype)], axis=2).astype(F32)
    feature_major = jnp.concatenate([w_in[:, :, OFF_MQ:OFF_MQ + MLSTM_WIDTH], w_in[:, :, OFF_MV:OFF_G],
                                     w_in[:, :, OFF_G:][:, :, gate_order]], axis=2)
    return {
        "w_rows": jnp.concatenate([w_in[:, :, :OFF_MQ], w_in[:, :, OFF_MQ + MLSTM_WIDTH:OFF_MV]],
                                  axis=2).astype(BF16),
        "w_t": jnp.swapaxes(feature_major, 1, 2).astype(BF16),
        "gb": gate_bias.astype(F32)[:, gate_order].reshape(DEPTH, N_GATE_COLS, 1),
        "conv_wk": conv_w[:, :, MLSTM_WIDTH:].astype(F32),
        "conv_bk": row(conv_b[:, MLSTM_WIDTH:]),
        "conv_q": cq,
        "sink": attn_sink.astype(F32),
        "att_g": row(attn_norm_g),
        "mlstm_g": mlstm_norm_g.astype(F32).reshape(DEPTH, MLSTM_WIDTH, 1),
        "w_out_a": w_out[:, :ATT_WIDTH].astype(BF16),
        "w_out_m": w_out[:, ATT_WIDTH:].astype(BF16),
        "ln1_g": row(ln1_g), "ln1_b": row(ln1_b),
        "wq": wq_mem.astype(BF16), "wkv": wkv_mem.astype(BF16), "wo": wo_mem.astype(BF16),
        "ln2_g": row(ln2_g), "ln2_b": row(ln2_b),
        "router_wt": _hi_lo_rows(router_w.astype(F32).T),
        "router_b": router_bias.astype(F32).reshape(N_EXPERTS, 1),
        "w_gate": w_gate.astype(BF16), "w_up": w_up.astype(BF16), "w_down": w_down.astype(BF16),
        "ln3_g": row(ln3_g), "ln3_b": row(ln3_b),
    }


def kernel(x_prompt, x_sample, mem_prompt, mem_sample, w_in, gate_bias, conv_w, conv_b, attn_sink, attn_norm_g, mlstm_norm_g, w_out, ln1_g, ln1_b, wq_mem, wkv_mem, wo_mem, ln2_g, ln2_b, router_w, router_bias, w_gate, w_up, w_down, ln3_g, ln3_b):
    wts = _prepare_weights(w_in, gate_bias, conv_w, conv_b, attn_sink, attn_norm_g, mlstm_norm_g, w_out, ln1_g, ln1_b,
                           wq_mem, wkv_mem, wo_mem, ln2_g, ln2_b, router_w, router_bias, w_gate, w_up, w_down,
                           ln3_g, ln3_b)
    consts = _constants()
    return (_trunk(x_prompt, mem_prompt, wts, consts), _trunk(x_sample, mem_sample, wts, consts))
```

```python
import functools
import math

import jax
import jax.numpy as jnp
from jax import lax
from jax.experimental import pallas as pl
from jax.experimental.pallas import tpu as pltpu

F32 = jnp.float32
BF16 = jnp.bfloat16

D_MODEL = 1024
DEPTH = 4
HEAD_DIM = 64
N_ATT_HEADS = 8
N_KV_HEADS = 2
ATT_WIDTH = N_ATT_HEADS * HEAD_DIM
KV_WIDTH = N_KV_HEADS * HEAD_DIM
BLOCK = 128
ROT_DIM = HEAD_DIM // 4
ROPE_THETA = 500000.0
MLSTM_WIDTH = D_MODEL - ATT_WIDTH
N_MLSTM_HEADS = 4
MLSTM_HEAD_DIM = MLSTM_WIDTH // N_MLSTM_HEADS
CHUNK = 128
OFF_AQ = 0
OFF_AK = OFF_AQ + ATT_WIDTH
OFF_AV = OFF_AK + KV_WIDTH
OFF_MQ = OFF_AV + KV_WIDTH
OFF_MV = OFF_MQ + 2 * MLSTM_WIDTH
OFF_MO = OFF_MV + MLSTM_WIDTH
OFF_G = OFF_MO + MLSTM_WIDTH
N_GATE_COLS = 4 * N_MLSTM_HEADS
N_MEM = 256
N_X_HEADS = 4
X_HEAD_DIM = D_MODEL // N_X_HEADS
N_EXPERTS = 16
N_GROUPS = 4
EXPERTS_PER_GROUP = N_EXPERTS // N_GROUPS
TOP_K = 2
D_FF_EXPERT = 512
ALPHA = (2.0 * DEPTH) ** 0.25
LN_EPS = 1e-5
RMS_EPS = 1e-6
NEG = -1e30

LANES = 128
TOKEN_TILE = 512
IN_PROJ_TILE = 1024
XATTN_TILE = 1024
ROW_BLOCK = 1024
VMEM_LIMIT = 56 * 1024 * 1024


def _params(*sem):
    return pltpu.CompilerParams(dimension_semantics=sem, vmem_limit_bytes=VMEM_LIMIT)


def _layer_norm(z, g, b):
    mu = jnp.mean(z, axis=-1, keepdims=True)
    zc = z - mu
    var = jnp.mean(zc * zc, axis=-1, keepdims=True)
    return zc * lax.rsqrt(var + LN_EPS) * g + b


def _dot(a, b):
    return jnp.dot(a, b, preferred_element_type=F32)


def _dot_nt(a, b):
    return lax.dot_general(a, b, (((1,), (1,)), ((), ())), preferred_element_type=F32)


def _dot_tn(a, b):
    return lax.dot_general(a, b, (((0,), (0,)), ((), ())), preferred_element_type=F32)


_X_HALO = 8


_W_MK = OFF_MQ
_W_END = _W_MK + MLSTM_WIDTH
_GP_ROWS = 3 * 2 * N_MLSTM_HEADS


def _in_proj_kernel(x_ref, xp_ref, xn_ref, w_ref, wt_ref, gb_ref, cwk_ref, cbk_ref, cq_ref,
                    cos_ref, sa_ref, sb_ref, tri_u_ref, tri_l_ref,
                    qa_ref, ka_ref, va_ref, mk_ref, mqt_ref, mvt_ref, mot_ref, gp_ref, rn_ref, *, n_seq_tiles):
    xb = x_ref[...].astype(BF16)
    cos = cos_ref[...]
    sa = sa_ref[...]
    sb = sb_ref[...]

    def mm(lo, hi):
        return _dot(xb, w_ref[:, lo:hi])

    def rope(t):
        return t * cos + pltpu.roll(t, LANES - ROT_DIM // 2, 1) * sa + pltpu.roll(t, ROT_DIM // 2, 1) * sb

    q = mm(OFF_AQ, OFF_AK)
    scale = math.log2(math.e) / math.sqrt(HEAD_DIM)
    for j in range(ATT_WIDTH // LANES):
        qa_ref[:, j * LANES:(j + 1) * LANES] = (rope(q[:, j * LANES:(j + 1) * LANES]) * scale).astype(BF16)
    kv = mm(OFF_AK, OFF_MQ)
    ka_ref[...] = rope(kv[:, :KV_WIDTH]).astype(BF16)
    va_ref[...] = kv[:, KV_WIDTH:].astype(BF16)
    tm = xb.shape[0]
    pos = pl.program_id(0) % n_seq_tiles
    has_prev = pos > 0
    has_next = pos < n_seq_tiles - 1
    halo = jnp.concatenate([xp_ref[...], xn_ref[...]], axis=0).astype(BF16)

    def silu(y):
        return y * (1.0 / (1.0 + jnp.exp(-y)))

    u = mm(_W_MK, _W_END)
    uh = _dot(halo, w_ref[:, _W_MK:_W_END])
    rowi = lax.broadcasted_iota(jnp.int32, u.shape, 0)
    u_prev = jnp.where(rowi == 0, jnp.where(has_prev, uh[_X_HALO - 1:_X_HALO, :], 0.0), pltpu.roll(u, 1, 0))
    u_next = jnp.where(rowi == tm - 1, jnp.where(has_next, uh[_X_HALO:_X_HALO + 1, :], 0.0),
                       pltpu.roll(u, tm - 1, 0))
    yk = silu(cwk_ref[0:1, :] * u_prev + cwk_ref[1:2, :] * u + cwk_ref[2:3, :] * u_next + cbk_ref[...])
    mk_ref[...] = (yk * (MLSTM_HEAD_DIM ** -0.5)).astype(BF16)

    feat = _dot_nt(wt_ref[...], xb)
    ut = feat[:MLSTM_WIDTH]
    uht = _dot_nt(wt_ref[0:MLSTM_WIDTH, :], halo)
    lanei = lax.broadcasted_iota(jnp.int32, ut.shape, 1)
    ut_prev = jnp.where(lanei == 0, jnp.where(has_prev, uht[:, _X_HALO - 1:_X_HALO], 0.0), pltpu.roll(ut, 1, 1))
    ut_next = jnp.where(lanei == tm - 1, jnp.where(has_next, uht[:, _X_HALO:_X_HALO + 1], 0.0),
                        pltpu.roll(ut, tm - 1, 1))
    cq = cq_ref[...]
    mqt_ref[...] = silu(cq[:, 0:1] * ut_prev + cq[:, 1:2] * ut + cq[:, 2:3] * ut_next + cq[:, 3:4]).astype(BF16)
    mvt_ref[...] = feat[MLSTM_WIDTH:2 * MLSTM_WIDTH].astype(BF16)
    mot_ref[...] = feat[2 * MLSTM_WIDTH:3 * MLSTM_WIDTH].astype(BF16)

    gt = feat[3 * MLSTM_WIDTH:] + gb_ref[...]
    half = 2 * N_MLSTM_HEADS
    gi, gf = gt[:half], gt[half:]
    ls = jnp.minimum(gf, 0.0) - jnp.log1p(jnp.exp(-jnp.abs(gf)))
    n_chunks = tm // CHUNK
    stack = lambda t: jnp.concatenate([t[:, c * CHUNK:(c + 1) * CHUNK] for c in range(n_chunks)], axis=0)
    ls_rows, gi_rows = stack(ls), stack(gi)
    pre = jnp.dot(ls_rows, tri_u_ref[...], precision=lax.Precision.HIGHEST, preferred_element_type=F32)
    suf = jnp.dot(ls_rows, tri_l_ref[...], precision=lax.Precision.HIGHEST, preferred_element_type=F32)
    is_fwd = (lax.broadcasted_iota(jnp.int32, ls_rows.shape, 0) % half) < N_MLSTM_HEADS
    lane = lax.broadcasted_iota(jnp.int32, ls_rows.shape, 1)
    b = jnp.where(is_fwd, pre, suf)
    r = gi_rows - b
    cm_f, cm_b = r, r
    k = 1
    while k < CHUNK:
        cm_f = jnp.maximum(cm_f, jnp.where(lane >= k, pltpu.roll(cm_f, k, 1), -jnp.inf))
        cm_b = jnp.maximum(cm_b, jnp.where(lane < CHUNK - k, pltpu.roll(cm_b, CHUNK - k, 1), -jnp.inf))
        k *= 2
    cm = jnp.where(is_fwd, cm_f, cm_b)
    log2e = math.log2(math.e)
    b, cm, r = b * log2e, cm * log2e, r * log2e
    pad = jnp.zeros((CHUNK - half, CHUNK), F32)
    for c in range(n_chunks):
        rows = slice(c * half, (c + 1) * half)
        gp_ref[c] = jnp.concatenate([b[rows], cm[rows], r[rows]], axis=0)
        rn_ref[c * CHUNK:(c + 1) * CHUNK, :] = jnp.concatenate([r[rows], pad], axis=0).T


def _in_proj(x, w_rows, w_t, gb, cwk, cbk, cq, rope_tabs, tri_u, tri_l, seq):
    T = x.shape[0]
    tm = IN_PROJ_TILE
    n_seq_tiles = seq // tm
    halo_per_tile = tm // _X_HALO
    n_halo = T // _X_HALO
    cos_t, sa_t, sb_t = rope_tabs
    row_spec = lambda w: pl.BlockSpec((tm, w), lambda i: (i, 0))
    full = lambda a: pl.BlockSpec(a.shape, lambda i: (0,) * a.ndim)
    tab_spec = pl.BlockSpec((tm, LANES), lambda i: (i % n_seq_tiles, 0))
    prev_spec = pl.BlockSpec((_X_HALO, D_MODEL), lambda i: (jnp.maximum(i * halo_per_tile - 1, 0), 0))
    next_spec = pl.BlockSpec((_X_HALO, D_MODEL), lambda i: (jnp.minimum((i + 1) * halo_per_tile, n_halo - 1), 0))
    widths = (ATT_WIDTH, KV_WIDTH, KV_WIDTH, MLSTM_WIDTH)
    col_spec = _feature_major_spec(MLSTM_WIDTH, tm, n_seq_tiles)
    feat_major = jax.ShapeDtypeStruct((MLSTM_SEQS, MLSTM_WIDTH, T // MLSTM_SEQS), BF16)
    out_shapes = tuple(jax.ShapeDtypeStruct((T, w), BF16) for w in widths) + (
        feat_major, feat_major, feat_major,
        jax.ShapeDtypeStruct((T // CHUNK, _GP_ROWS, CHUNK), F32), jax.ShapeDtypeStruct((T, LANES), F32))
    out_specs = tuple(row_spec(w) for w in widths) + (
        col_spec, col_spec, col_spec,
        pl.BlockSpec((tm // CHUNK, _GP_ROWS, CHUNK), lambda i: (i, 0, 0)), row_spec(LANES))
    return pl.pallas_call(
        functools.partial(_in_proj_kernel, n_seq_tiles=n_seq_tiles),
        grid=(T // tm,),
        in_specs=[row_spec(D_MODEL), prev_spec, next_spec, full(w_rows), full(w_t), full(gb),
                  full(cwk), full(cbk), full(cq), tab_spec, tab_spec, tab_spec, full(tri_u), full(tri_l)],
        out_specs=out_specs,
        out_shape=out_shapes,
        compiler_params=_params("parallel"),
        name="in_proj",
    )(x, x, x, w_rows, w_t, gb, cwk, cbk, cq, cos_t, sa_t, sb_t, tri_u, tri_l)


ATT_Q_TILE = 1024
_Q_BLOCKS = ATT_Q_TILE // BLOCK


def _attn_kernel(sink_ref, q_ref, kp_ref, k_ref, kn_ref, vp_ref, v_ref, vn_ref, g_ref, o_ref, *, n_tiles):
    i = pl.program_id(1)
    lane = lax.broadcasted_iota(jnp.int32, (ATT_Q_TILE + 2 * BLOCK, LANES), 1)
    low = lane < HEAD_DIM

    def split(prev_ref, own_ref, next_ref):
        t = jnp.concatenate([prev_ref[...], own_ref[...], next_ref[...]], axis=0).astype(F32)
        r = pltpu.roll(t, HEAD_DIM, 1)
        zero = jnp.zeros_like(t)
        lo = (jnp.where(low, t, zero).astype(BF16), jnp.where(low, r, zero).astype(BF16))
        hi = (jnp.where(low, zero, r).astype(BF16), jnp.where(low, zero, t).astype(BF16))
        return lo, hi

    k_lo, k_hi = split(kp_ref, k_ref, kn_ref)
    v_lo, v_hi = split(vp_ref, v_ref, vn_ref)

    rowi = lax.broadcasted_iota(jnp.int32, (BLOCK, BLOCK), 0)
    coli = lax.broadcasted_iota(jnp.int32, (BLOCK, BLOCK), 1)
    prev_bias = jnp.where(coli < rowi, NEG, 0.0).astype(F32)
    next_bias = jnp.where(coli > rowi, NEG, 0.0).astype(F32)
    first_bias = jnp.where(i == 0, NEG, 0.0).astype(F32)
    last_bias = jnp.where(i == n_tiles - 1, NEG, 0.0).astype(F32)
    lane_o = lax.broadcasted_iota(jnp.int32, (BLOCK, LANES), 1)
    g = g_ref[...]
    log2e = math.log2(math.e)

    for r in range(_Q_BLOCKS):
        pb = prev_bias + first_bias if r == 0 else prev_bias
        nb = next_bias + last_bias if r == _Q_BLOCKS - 1 else next_bias
        rows = slice(r * BLOCK, (r + 1) * BLOCK)
        win = slice(r * BLOCK, (r + 3) * BLOCK)
        tiles = []
        for c in range(N_KV_HEADS):
            q2 = jnp.concatenate([q_ref[rows, (2 * c) * LANES:(2 * c + 1) * LANES],
                                  q_ref[rows, (2 * c + 1) * LANES:(2 * c + 2) * LANES]], axis=0)
            kc = jnp.concatenate([k_lo[c][win], k_hi[c][win]], axis=0)
            vc = jnp.concatenate([v_lo[c][win], v_hi[c][win]], axis=0)
            s = _dot_nt(q2, kc)
            p_rows, inv_rows = [], []
            for t in range(2):
                ps, invs = [], []
                for hh in range(2):
                    head = 4 * c + 2 * t + hh
                    sink = sink_ref[head] * log2e
                    blk = lambda j: s[t * BLOCK:(t + 1) * BLOCK, (3 * hh + j) * BLOCK:(3 * hh + j + 1) * BLOCK]
                    sp, so, sn = blk(0) + pb, blk(1), blk(2) + nb
                    m = jnp.maximum(jnp.max(jnp.maximum(jnp.maximum(sp, so), sn), axis=-1, keepdims=True), sink)
                    pp, po, pn = jnp.exp2(sp - m), jnp.exp2(so - m), jnp.exp2(sn - m)
                    den = jnp.sum(pp + po + pn, axis=-1, keepdims=True) + jnp.exp2(sink - m)
                    ps += [pp.astype(BF16), po.astype(BF16), pn.astype(BF16)]
                    invs.append(1.0 / den)
                p_rows.append(jnp.concatenate(ps, axis=1))
                inv_rows.append(jnp.where(lane_o < HEAD_DIM, invs[0], invs[1]))
            o2 = _dot(jnp.concatenate(p_rows, axis=0), vc)
            tiles.append(o2[:BLOCK] * inv_rows[0])
            tiles.append(o2[BLOCK:] * inv_rows[1])
        o = jnp.concatenate(tiles, axis=1)
        ms = jnp.mean(o * o, axis=-1, keepdims=True)
        o_ref[rows, :] = (o * lax.rsqrt(ms + RMS_EPS) * g).astype(BF16)


def _attention(qa, ka, va, sink, att_g, batch, seq):
    T = qa.shape[0]
    n_tiles = seq // ATT_Q_TILE
    blocks_per_seq = seq // BLOCK
    own = lambda w: pl.BlockSpec((ATT_Q_TILE, w), lambda b, i: (b * n_tiles + i, 0))
    prev = pl.BlockSpec((BLOCK, KV_WIDTH),
                        lambda b, i: (b * blocks_per_seq + jnp.maximum(i * _Q_BLOCKS - 1, 0), 0))
    nxt = pl.BlockSpec((BLOCK, KV_WIDTH),
                       lambda b, i: (b * blocks_per_seq + jnp.minimum((i + 1) * _Q_BLOCKS, blocks_per_seq - 1), 0))
    return pl.pallas_call(
        functools.partial(_attn_kernel, n_tiles=n_tiles),
        grid=(batch, n_tiles),
        in_specs=[pl.BlockSpec(memory_space=pltpu.SMEM), own(ATT_WIDTH), prev, own(KV_WIDTH), nxt,
                  prev, own(KV_WIDTH), nxt, pl.BlockSpec((1, ATT_WIDTH), lambda b, i: (0, 0))],
        out_specs=own(ATT_WIDTH),
        out_shape=jax.ShapeDtypeStruct((T, ATT_WIDTH), BF16),
        compiler_params=_params("parallel", "parallel"),
        name="band_attention",
    )(sink, qa, ka, ka, ka, va, va, va, att_g)


MLSTM_SEQS = 4
_STATE_ROWS = MLSTM_HEAD_DIM + 8


def _mlstm_kernel(qt_f, vt_f, k_f, rn_f, gp_f, qt_b, vt_b, k_b, rn_b, gp_b, of_ref, ob_ref, c_state, m_state):
    @pl.when(pl.program_id(1) == 0)
    def _():
        c_state[...] = jnp.zeros_like(c_state)
        m_state[...] = jnp.zeros_like(m_state)

    key = lax.broadcasted_iota(jnp.int32, (CHUNK, CHUNK), 0)
    qry = lax.broadcasted_iota(jnp.int32, (CHUNK, CHUNK), 1)
    nh = N_MLSTM_HEADS

    units = []
    for sq in range(MLSTM_SEQS):
        for fwd, qt_ref, vt_ref, k_ref, rn_ref, gp_ref, o_ref in ((True, qt_f, vt_f, k_f, rn_f, gp_f, of_ref),
                                                                 (False, qt_b, vt_b, k_b, rn_b, gp_b, ob_ref)):
            gates = gp_ref[sq, 0]
            off = 0 if fwd else nh
            a_pos = CHUNK - 1 if fwd else 0
            for h in range(nh):
                hs = slice(h * MLSTM_HEAD_DIM, (h + 1) * MLSTM_HEAD_DIM)
                b = gates[off + h:off + h + 1, :]
                st = (2 * sq + (0 if fwd else 1)) * nh + h
                units.append(dict(
                    sq=sq, hs=hs, st=st, o_ref=o_ref, qt_ref=qt_ref, vt_ref=vt_ref, k_ref=k_ref,
                    visible=(key <= qry) if fwd else (key >= qry),
                    b=b, cm=gates[2 * nh + off + h:2 * nh + off + h + 1, :],
                    r_row=gates[4 * nh + off + h:4 * nh + off + h + 1, :],
                    r_keys=jnp.broadcast_to(rn_ref[sq, :, off + h:off + h + 1], (CHUNK, CHUNK)),
                    a=b[:, a_pos:a_pos + 1],
                    m_in=m_state[st:st + 1, 0:1]))

    for u in units:
        inter_log = u["b"] + u["m_in"]
        m_t = jnp.maximum(inter_log, u["b"] + u["cm"])
        u["decay"] = jnp.exp2(jnp.where(u["visible"], u["r_keys"] + (u["b"] - m_t), NEG))
        u["inter_w"] = jnp.exp2(inter_log - m_t)
        u["floor"] = jnp.exp2(-m_t)
    for u in units:
        qt = u["qt_ref"][u["sq"], u["hs"], :]
        u["k"] = u["k_ref"][u["sq"], :, u["hs"]]
        u["vt"] = u["vt_ref"][u["sq"], u["hs"], :]
        u["scores"] = _dot(u["k"], qt)
        u["c_in"] = c_state[u["st"]]
        u["inter"] = _dot(u["c_in"].astype(BF16), qt)
    for u in units:
        sw = u["scores"] * u["decay"]
        num = _dot(u["vt"], sw.astype(BF16)) + u["inter_w"] * u["inter"][:MLSTM_HEAD_DIM]
        den = (jnp.sum(sw, axis=0, keepdims=True)
               + u["inter_w"] * u["inter"][MLSTM_HEAD_DIM:MLSTM_HEAD_DIM + 1])
        u["o_ref"][u["sq"], u["hs"], :] = (num / jnp.maximum(jnp.abs(den), u["floor"])).astype(u["o_ref"].dtype)
    for u in units:
        a, m_in, st = u["a"], u["m_in"], u["st"]
        g_max = a + jnp.max(u["r_row"], axis=-1, keepdims=True)
        kw = u["k"].astype(F32) * jnp.exp2(u["r_keys"] + (a - g_max))
        m_new = jnp.maximum(a + m_in, g_max)
        keep = jnp.exp2(a + m_in - m_new)
        add = jnp.exp2(g_max - m_new)
        c_state[st, 0:MLSTM_HEAD_DIM, :] = keep * u["c_in"][:MLSTM_HEAD_DIM] + add * _dot(u["vt"], kw.astype(BF16))
        c_state[st, MLSTM_HEAD_DIM:MLSTM_HEAD_DIM + 1, :] = (
            keep * u["c_in"][MLSTM_HEAD_DIM:MLSTM_HEAD_DIM + 1] + add * jnp.sum(kw, axis=0, keepdims=True))
        m_state[st:st + 1, :] = jnp.broadcast_to(m_new, (1, LANES))


def _feature_major_spec(width, tm, tiles_per_seq):
    def index(i):
        s = i // tiles_per_seq
        return (s % MLSTM_SEQS, 0, (s // MLSTM_SEQS) * tiles_per_seq + i % tiles_per_seq)
    return pl.BlockSpec((None, width, tm), index)


def _mlstm(mqt, mvt, mk, rn, gp, batch, seq):
    T = mk.shape[0]
    nc = seq // CHUNK
    groups = batch // MLSTM_SEQS
    mk = mk.reshape(groups, MLSTM_SEQS, seq, MLSTM_WIDTH)
    rn = rn.reshape(groups, MLSTM_SEQS, seq, LANES)
    gp = gp.reshape(groups, MLSTM_SEQS, nc, _GP_ROWS, CHUNK)

    def specs(chunk_of):
        feat = pl.BlockSpec((MLSTM_SEQS, MLSTM_WIDTH, CHUNK), lambda b, i: (0, 0, b * nc + chunk_of(i)))
        return [feat, feat,
                pl.BlockSpec((None, MLSTM_SEQS, CHUNK, MLSTM_WIDTH), lambda b, i: (b, 0, chunk_of(i), 0)),
                pl.BlockSpec((None, MLSTM_SEQS, CHUNK, LANES), lambda b, i: (b, 0, chunk_of(i), 0)),
                pl.BlockSpec((None, MLSTM_SEQS, 1, _GP_ROWS, CHUNK), lambda b, i: (b, 0, chunk_of(i), 0, 0))]

    fwd_chunk = lambda i: i
    bwd_chunk = lambda i: nc - 1 - i
    out = jax.ShapeDtypeStruct((MLSTM_SEQS, MLSTM_WIDTH, T // MLSTM_SEQS), BF16)
    n_state = 2 * MLSTM_SEQS * N_MLSTM_HEADS
    return pl.pallas_call(
        _mlstm_kernel,
        grid=(groups, nc),
        in_specs=specs(fwd_chunk) + specs(bwd_chunk),
        out_specs=(specs(fwd_chunk)[0], specs(bwd_chunk)[0]),
        out_shape=(out, out),
        scratch_shapes=[pltpu.VMEM((n_state, _STATE_ROWS, MLSTM_HEAD_DIM), F32),
                        pltpu.VMEM((n_state, LANES), F32)],
        compiler_params=_params("parallel", "arbitrary"),
        name="mlstm",
    )(mqt, mvt, mk, rn, gp, mqt, mvt, mk, rn, gp)


def _mixer_out(x, att_ref, hf_ref, hb_ref, mo_ref, mg_ref, wa_ref, wm_ref, g_ref, b_ref):
    h = hf_ref[...].astype(F32) + hb_ref[...].astype(F32)
    parts = []
    for hd in range(N_MLSTM_HEADS):
        hh = h[hd * MLSTM_HEAD_DIM:(hd + 1) * MLSTM_HEAD_DIM]
        ms = jnp.mean(hh * hh, axis=0, keepdims=True)
        parts.append(hh * lax.rsqrt(ms + RMS_EPS))
    hn = jnp.concatenate(parts, axis=0) * mg_ref[...]
    gate = 1.0 / (1.0 + jnp.exp(-mo_ref[...].astype(F32)))
    y = _dot(att_ref[...], wa_ref[...]) + _dot_tn((hn * gate).astype(BF16), wm_ref[...])
    return _layer_norm(ALPHA * x + y, g_ref[...], b_ref[...])


def _kv_proj_kernel(m_ref, w_ref, k_ref, v_ref):
    mb = m_ref[...].astype(BF16)
    k_ref[...] = _dot(mb, w_ref[:, :D_MODEL]).astype(BF16)
    v_ref[...] = _dot(mb, w_ref[:, D_MODEL:]).astype(BF16)


def _kv_proj(mem, wkv):
    M = mem.shape[0]
    tm = TOKEN_TILE
    row = pl.BlockSpec((tm, D_MODEL), lambda i: (i, 0))
    out = jax.ShapeDtypeStruct((M, D_MODEL), BF16)
    return pl.pallas_call(
        _kv_proj_kernel,
        grid=(M // tm,),
        in_specs=[row, pl.BlockSpec(wkv.shape, lambda i: (0, 0))],
        out_specs=(row, row),
        out_shape=(out, out),
        compiler_params=_params("parallel"),
        name="mem_kv_proj",
    )(mem, wkv)


def _xattn_kernel(x_ref, att_ref, hf_ref, hb_ref, mo_ref, mg_ref, wa_ref, wm_ref, g1_ref, b1_ref,
                  k_ref, v_ref, wq_ref, wo_ref, g_ref, b_ref, rwt_ref, rb_ref, tri_ref, low_ref,
                  o_ref, info_ref, cnt_ref):
    subs = [slice(t * TOKEN_TILE, (t + 1) * TOKEN_TILE) for t in range(x_ref.shape[0] // TOKEN_TILE)]
    xs = [_mixer_out(x_ref[rows], att_ref.at[rows], hf_ref.at[:, rows], hb_ref.at[:, rows], mo_ref.at[:, rows],
                     mg_ref, wa_ref, wm_ref, g1_ref, b1_ref) for rows in subs]
    scale = math.log2(math.e) / math.sqrt(X_HEAD_DIM)
    qs = [_dot(x.astype(BF16), wq_ref[...]) for x in xs]
    outs = [[] for _ in subs]
    for h in range(N_X_HEADS):
        hs = slice(h * X_HEAD_DIM, (h + 1) * X_HEAD_DIM)
        for t, q in enumerate(qs):
            s = _dot_nt((q[:, hs] * scale).astype(BF16), k_ref[:, hs])
            p = jnp.exp2(s - jnp.max(s, axis=-1, keepdims=True))
            inv = 1.0 / jnp.sum(p, axis=-1, keepdims=True)
            outs[t].append((_dot(p.astype(BF16), v_ref[:, hs]) * inv).astype(BF16))
    ys = [_dot(jnp.concatenate(o, axis=1), wo_ref[...]) for o in outs]
    for t, rows in enumerate(subs):
        x2 = _layer_norm(ALPHA * xs[t] + ys[t], g_ref[...], b_ref[...])
        o_ref[rows, :] = x2
        _route(x2, rwt_ref, rb_ref, tri_ref, low_ref, info_ref.at[:, rows], cnt_ref.at[t:t + 1])


def _mixer_out_xattn(x, att, hft, hbt, mot, mg, wa, wm, g1, b1, k_mem, v_mem, wq, wo, g, b,
                     router_wt, router_b, tri_strict, low_strict, batch, seq):
    T = x.shape[0]
    tm = XATTN_TILE
    nt = seq // tm
    row = pl.BlockSpec((tm, D_MODEL), lambda bb, i: (bb * nt + i, 0))
    att_row = pl.BlockSpec((tm, ATT_WIDTH), lambda bb, i: (bb * nt + i, 0))
    flat_feat = _feature_major_spec(MLSTM_WIDTH, tm, nt)
    feat = pl.BlockSpec(flat_feat.block_shape, lambda bb, i: flat_feat.index_map(bb * nt + i))
    mem = pl.BlockSpec((N_MEM, D_MODEL), lambda bb, i: (bb, 0))
    full = lambda a: pl.BlockSpec(a.shape, lambda bb, i: (0,) * a.ndim)
    return pl.pallas_call(
        _xattn_kernel,
        grid=(batch, nt),
        in_specs=[row, att_row, feat, feat, feat, full(mg), full(wa), full(wm), full(g1), full(b1),
                  mem, mem, full(wq), full(wo), full(g), full(b), full(router_wt), full(router_b),
                  full(tri_strict), full(low_strict)],
        out_specs=(row, pl.BlockSpec((_IDX_ROWS, tm), lambda bb, i: (0, bb * nt + i)),
                   pl.BlockSpec((tm // TOKEN_TILE, N_EXPERTS, LANES), lambda bb, i: (bb * nt + i, 0, 0))),
        out_shape=(jax.ShapeDtypeStruct((T, D_MODEL), F32), jax.ShapeDtypeStruct((_IDX_ROWS, T), F32),
                   jax.ShapeDtypeStruct((T // TOKEN_TILE, N_EXPERTS, LANES), jnp.int32)),
        compiler_params=_params("parallel", "parallel"),
        name="mixer_out_xattn_route",
    )(x, att, hft, hbt, mot, mg, wa, wm, g1, b1, k_mem, v_mem, wq, wo, g, b, router_wt, router_b,
      tri_strict, low_strict)


_IDX_ROWS = 8
SLAB = 16
TILE_SLOTS = TOP_K * TOKEN_TILE + N_EXPERTS * SLAB


def _route(x, wt_ref, b_ref, tri_ref, low_ref, info_ref, cnt_ref):
    x_hi = x.astype(BF16)
    x_lo = (x - x_hi.astype(F32)).astype(BF16)
    by_hi = _dot_nt(wt_ref[...], x_hi)
    logits = by_hi[:N_EXPERTS] + by_hi[N_EXPERTS:] + _dot_nt(wt_ref[0:N_EXPERTS, :], x_lo)
    s = 1.0 / (1.0 + jnp.exp(-logits))
    sel = s + b_ref[...]
    srow = lambda e: s[e:e + 1, :]
    brow = lambda e: sel[e:e + 1, :]
    best = None
    gi = None
    for gidx in range(N_GROUPS):
        vals = [brow(gidx * EXPERTS_PER_GROUP + j) for j in range(EXPERTS_PER_GROUP)]
        top2 = None
        for a in range(EXPERTS_PER_GROUP):
            for b in range(a + 1, EXPERTS_PER_GROUP):
                pair = vals[a] + vals[b]
                top2 = pair if top2 is None else jnp.maximum(top2, pair)
        if best is None:
            best, gi = top2, jnp.zeros(top2.shape, jnp.int32)
        else:
            better = top2 > best
            gi = jnp.where(better, gidx, gi)
            best = jnp.where(better, top2, best)

    def in_group(rowfn, j):
        out = rowfn(j)
        for gidx in range(1, N_GROUPS):
            out = jnp.where(gi == gidx, rowfn(gidx * EXPERTS_PER_GROUP + j), out)
        return out

    bv = [in_group(brow, j) for j in range(EXPERTS_PER_GROUP)]
    sv = [in_group(srow, j) for j in range(EXPERTS_PER_GROUP)]

    def argmax_first(vals):
        bi = jnp.zeros(vals[0].shape, jnp.int32)
        bm = vals[0]
        for j in range(1, len(vals)):
            better = vals[j] > bm
            bi = jnp.where(better, j, bi)
            bm = jnp.where(better, vals[j], bm)
        return bi

    i1 = argmax_first(bv)
    i2 = argmax_first([jnp.where(i1 == j, -jnp.inf, bv[j]) for j in range(EXPERTS_PER_GROUP)])

    def pick(vals, idx):
        out = vals[0]
        for j in range(1, len(vals)):
            out = jnp.where(idx == j, vals[j], out)
        return out

    w1 = pick(sv, i1)
    w2 = pick(sv, i2)
    tot = w1 + w2
    tm = logits.shape[1]
    eid = lax.broadcasted_iota(jnp.int32, (N_EXPERTS, tm), 0)
    oh1 = (eid == gi * EXPERTS_PER_GROUP + i1).astype(F32)
    oh2 = (eid == gi * EXPERTS_PER_GROUP + i2).astype(F32)
    before1 = _dot(oh1.astype(BF16), tri_ref[...])
    before2 = _dot(oh2.astype(BF16), tri_ref[...])
    c1 = jnp.sum(oh1, axis=1, keepdims=True)
    cnt = c1 + jnp.sum(oh2, axis=1, keepdims=True)
    rows = jnp.floor((cnt + (SLAB - 1)) * (1.0 / SLAB)) * SLAB
    rows_b = jnp.broadcast_to(rows, (N_EXPERTS, LANES))
    start = jnp.dot(low_ref[...], rows_b, precision=lax.Precision.HIGHEST, preferred_element_type=F32)[:, 0:1]
    slot1 = jnp.sum(oh1 * (start + before1), axis=0, keepdims=True)
    slot2 = jnp.sum(oh2 * (start + c1 + before2), axis=0, keepdims=True)
    zf = jnp.zeros((_IDX_ROWS - 2 * TOP_K, tm), F32)
    info_ref[...] = jnp.concatenate([w1 / tot, w2 / tot, slot1, slot2, zf], axis=0)
    cnt_ref[0] = rows_b.astype(jnp.int32)


_BIG_SLAB = 4 * SLAB


def _slab_copies(i, gs_ref, ls_ref, rc_ref, make_copy, op):
    for e in range(N_EXPERTS):
        idx = i * N_EXPERTS + e
        ls = ls_ref[idx]
        gs = gs_ref[idx]
        n_big = rc_ref[idx] // _BIG_SLAB
        done = n_big * _BIG_SLAB

        prio = e % 2

        def big(j, carry, ls=ls, gs=gs, prio=prio):
            op(make_copy(pl.multiple_of(ls + j * _BIG_SLAB, SLAB), pl.multiple_of(gs + j * _BIG_SLAB, SLAB),
                         _BIG_SLAB), prio)
            return carry

        def small(j, carry, ls=ls + done, gs=gs + done, prio=prio):
            op(make_copy(pl.multiple_of(ls + j * SLAB, SLAB), pl.multiple_of(gs + j * SLAB, SLAB), SLAB), prio)
            return carry

        lax.fori_loop(0, n_big, big, 0)
        lax.fori_loop(0, (rc_ref[idx] - done) // SLAB, small, 0)


_ZERO_ROWS = 128


def _dispatch_kernel(gs_ref, ls_ref, rc_ref, ts_ref, tn_ref, x_ref, info_ref, xp_ref, xs_ref, zero_ref, sem, zsem):
    i = pl.program_id(0)
    last = pl.num_programs(0) - 1
    buf = i % 2
    slot0 = info_ref[2:3, :].astype(jnp.int32)
    slot1 = info_ref[3:4, :].astype(jnp.int32)
    xb = x_ref[...].astype(BF16)
    tm = xb.shape[0]
    sid = lax.broadcasted_iota(jnp.int32, (TILE_SLOTS, tm), 0)
    sel = jnp.where((sid == slot0) | (sid == slot1), 1.0, 0.0).astype(BF16)
    xs_ref[buf] = _dot(sel, xb).astype(BF16)

    def copies(tile, which, op):
        def make_copy(tile_row, global_row, rows):
            return pltpu.make_async_copy(xs_ref.at[which, pl.ds(tile_row, rows), :],
                                         xp_ref.at[pl.ds(global_row, rows), :], sem.at[which])
        _slab_copies(tile, gs_ref, ls_ref, rc_ref, make_copy, op)

    @pl.when(i > 0)
    def _():
        copies(i - 1, 1 - buf, lambda cp, prio: cp.wait())

    copies(i, buf, lambda cp, prio: cp.start(priority=prio))

    @pl.when(i == last)
    def _():
        zero_ref[...] = jnp.zeros_like(zero_ref)

        def tail(op):
            for e in range(N_EXPERTS + 1):
                t0 = ts_ref[e]
                rows = SLAB if e < N_EXPERTS else _ZERO_ROWS

                def body(j, carry, t0=t0, rows=rows):
                    op(pltpu.make_async_copy(zero_ref.at[pl.ds(0, rows), :],
                                             xp_ref.at[pl.ds(pl.multiple_of(t0 + j * rows, rows), rows), :], zsem))
                    return carry

                lax.fori_loop(0, tn_ref[e] // rows, body, 0)

        tail(lambda cp: cp.start())
        copies(i, buf, lambda cp, prio: cp.wait())
        tail(lambda cp: cp.wait())


def _dispatch(x, info, gstart, lstart, rc, tail_start, tail_rows, n_rows):
    T = x.shape[0]
    tm = TOKEN_TILE
    grid_spec = pltpu.PrefetchScalarGridSpec(
        num_scalar_prefetch=5,
        grid=(T // tm,),
        in_specs=[pl.BlockSpec((tm, D_MODEL), lambda i, *_: (i, 0)),
                  pl.BlockSpec((_IDX_ROWS, tm), lambda i, *_: (0, i))],
        out_specs=pl.BlockSpec(memory_space=pl.ANY),
        scratch_shapes=[pltpu.VMEM((2, TILE_SLOTS, D_MODEL), BF16), pltpu.VMEM((_ZERO_ROWS, D_MODEL), BF16),
                        pltpu.SemaphoreType.DMA((2,)), pltpu.SemaphoreType.DMA],
    )
    return pl.pallas_call(
        _dispatch_kernel,
        grid_spec=grid_spec,
        out_shape=jax.ShapeDtypeStruct((n_rows, D_MODEL), BF16),
        compiler_params=_params("arbitrary"),
        name="moe_dispatch",
    )(gstart, lstart, rc, tail_start, tail_rows, x, info)


def _ffn_kernel(be_ref, nu_ref, x_ref, wg_ref, wu_ref, wd_ref, o_ref):
    i = pl.program_id(0)

    @pl.when(i < nu_ref[0])
    def _():
        xb = x_ref[...]
        gate = _dot(xb, wg_ref[...])
        up = _dot(xb, wu_ref[...])
        h = gate * (1.0 / (1.0 + jnp.exp(-gate))) * up
        o_ref[...] = _dot(h.astype(BF16), wd_ref[...]).astype(BF16)

    @pl.when(i >= nu_ref[0])
    def _():
        o_ref[...] = jnp.zeros_like(o_ref)


def _expert_ffn(xp, blk_e, n_used, w_gate, w_up, w_down, layer):
    P = xp.shape[0]
    wspec = lambda shp: pl.BlockSpec((None, None) + shp, lambda i, be, nu: (layer, be[i], 0, 0))
    grid_spec = pltpu.PrefetchScalarGridSpec(
        num_scalar_prefetch=2,
        grid=(P // ROW_BLOCK,),
        in_specs=[pl.BlockSpec((ROW_BLOCK, D_MODEL), lambda i, be, nu: (jnp.minimum(i, nu[0] - 1), 0)),
                  wspec((D_MODEL, D_FF_EXPERT)), wspec((D_MODEL, D_FF_EXPERT)), wspec((D_FF_EXPERT, D_MODEL))],
        out_specs=pl.BlockSpec((ROW_BLOCK, D_MODEL), lambda i, be, nu: (i, 0)),
    )
    return pl.pallas_call(
        _ffn_kernel,
        grid_spec=grid_spec,
        out_shape=jax.ShapeDtypeStruct((P, D_MODEL), BF16),
        compiler_params=_params("arbitrary"),
        name="expert_ffn",
    )(blk_e, n_used, xp, w_gate, w_up, w_down)


def _combine_kernel(gs_ref, ls_ref, rc_ref, x_ref, info_ref, eye_ref, g_ref, b_ref, yp_ref, o_ref, ys_ref, sem):
    i = pl.program_id(0)
    buf = i % 2

    def copies(tile, which, op):
        def make_copy(tile_row, global_row, rows):
            return pltpu.make_async_copy(yp_ref.at[pl.ds(global_row, rows), :],
                                         ys_ref.at[which, pl.ds(tile_row, rows), :], sem.at[which])
        _slab_copies(tile, gs_ref, ls_ref, rc_ref, make_copy, op)

    @pl.when(i == 0)
    def _():
        ys_ref[...] = jnp.zeros_like(ys_ref)
        copies(i, buf, lambda cp, prio: cp.start(priority=prio))

    @pl.when(i + 1 < pl.num_programs(0))
    def _():
        copies(i + 1, 1 - buf, lambda cp, prio: cp.start(priority=prio))

    cols = lax.dot_general(info_ref[...], eye_ref[...], (((0,), (0,)), ((), ())),
                           precision=lax.Precision.HIGHEST, preferred_element_type=F32)
    sid = lax.broadcasted_iota(jnp.int32, (cols.shape[0], TILE_SLOTS), 1)
    weights = (jnp.where(sid == cols[:, 2:3].astype(jnp.int32), cols[:, 0:1], 0.0)
               + jnp.where(sid == cols[:, 3:4].astype(jnp.int32), cols[:, 1:2], 0.0)).astype(BF16)
    copies(i, buf, lambda cp, prio: cp.wait())
    y = _dot(weights, ys_ref[buf])
    o_ref[...] = _layer_norm(ALPHA * x_ref[...] + y, g_ref[...], b_ref[...])


def _combine_ln(x, yp, info, gstart, lstart, rc, eye, g, b):
    T = x.shape[0]
    tm = TOKEN_TILE
    full = lambda a: pl.BlockSpec(a.shape, lambda i, *_: (0,) * a.ndim)
    grid_spec = pltpu.PrefetchScalarGridSpec(
        num_scalar_prefetch=3,
        grid=(T // tm,),
        in_specs=[pl.BlockSpec((tm, D_MODEL), lambda i, *_: (i, 0)),
                  pl.BlockSpec((_IDX_ROWS, tm), lambda i, *_: (0, i)), full(eye), full(g), full(b),
                  pl.BlockSpec(memory_space=pl.ANY)],
        out_specs=pl.BlockSpec((tm, D_MODEL), lambda i, *_: (i, 0)),
        scratch_shapes=[pltpu.VMEM((2, TILE_SLOTS, D_MODEL), BF16), pltpu.SemaphoreType.DMA((2,))],
    )
    return pl.pallas_call(
        _combine_kernel,
        grid_spec=grid_spec,
        out_shape=jax.ShapeDtypeStruct((T, D_MODEL), F32),
        compiler_params=_params("arbitrary"),
        name="moe_combine_ln3",
    )(gstart, lstart, rc, x, info, eye, g, b, yp)


def _moe(x, info, rows, consts, w_gate, w_up, w_down, layer, g, b):
    T = x.shape[0]
    n_tiles = T // TOKEN_TILE
    P = T * TOP_K + n_tiles * N_EXPERTS * SLAB + N_EXPERTS * ROW_BLOCK
    rc = rows[:, :, 0]
    lstart = jnp.cumsum(rc, axis=1) - rc
    region = (jnp.sum(rc, axis=0) + ROW_BLOCK - 1) // ROW_BLOCK * ROW_BLOCK
    region_end = jnp.cumsum(region)
    gstart = (region_end - region)[None, :] + jnp.cumsum(rc, axis=0) - rc
    n_blk = P // ROW_BLOCK
    blk_row0 = jnp.arange(n_blk, dtype=jnp.int32) * ROW_BLOCK
    blk_e = jnp.minimum(jnp.sum(region_end[None, :] <= blk_row0[:, None], axis=1), N_EXPERTS - 1).astype(jnp.int32)
    n_used = (region_end[-1:] // ROW_BLOCK).astype(jnp.int32)
    flat = lambda t: t.reshape(-1).astype(jnp.int32)
    used = jnp.sum(rc, axis=0)
    tail_start = flat(jnp.concatenate([region_end - region + used, region_end[-1:]]))
    tail_rows = flat(jnp.concatenate([region - used, P - region_end[-1:]]))
    gstart, lstart, rc = flat(gstart), flat(lstart), flat(rc)
    xp = _dispatch(x, info, gstart, lstart, rc, tail_start, tail_rows, P)
    yp = _expert_ffn(xp, blk_e, n_used, w_gate, w_up, w_down, layer)
    return _combine_ln(x, yp, info, gstart, lstart, rc, consts["eye"], g, b)


def _rope_tables(seq):
    inv_freq = ROPE_THETA ** (-jnp.arange(0, ROT_DIM, 2, dtype=F32) / ROT_DIM)
    ang = jnp.arange(seq, dtype=F32)[:, None] * inv_freq[None, :]
    cos, sin = jnp.cos(ang), jnp.sin(ang)
    half = ROT_DIM // 2
    one = jnp.ones((seq, HEAD_DIM - ROT_DIM), F32)
    zero = jnp.zeros((seq, HEAD_DIM - ROT_DIM), F32)
    zh = jnp.zeros((seq, half), F32)
    cos_h = jnp.concatenate([cos, cos, one], axis=1)
    sa_h = jnp.concatenate([-sin, zh, zero], axis=1)
    sb_h = jnp.concatenate([zh, sin, zero], axis=1)
    rep = lambda t: jnp.concatenate([t] * (LANES // HEAD_DIM), axis=1)
    return rep(cos_h), rep(sa_h), rep(sb_h)


def _constants():
    r = lax.broadcasted_iota(jnp.int32, (CHUNK, CHUNK), 0)
    c = lax.broadcasted_iota(jnp.int32, (CHUNK, CHUNK), 1)
    rr = lax.broadcasted_iota(jnp.int32, (TOKEN_TILE, TOKEN_TILE), 0)
    cc = lax.broadcasted_iota(jnp.int32, (TOKEN_TILE, TOKEN_TILE), 1)
    er = lax.broadcasted_iota(jnp.int32, (N_EXPERTS, N_EXPERTS), 0)
    ec = lax.broadcasted_iota(jnp.int32, (N_EXPERTS, N_EXPERTS), 1)
    return {
        "tri_u": (r <= c).astype(F32),
        "tri_l": (r >= c).astype(F32),
        "tri_strict": (rr < cc).astype(BF16),
        "low_strict": (ec < er).astype(F32),
        "eye": (lax.broadcasted_iota(jnp.int32, (_IDX_ROWS, LANES), 0)
                == lax.broadcasted_iota(jnp.int32, (_IDX_ROWS, LANES), 1)).astype(F32),
    }


def _trunk(x, mem, wts, consts):
    batch, seq, _ = x.shape
    T = batch * seq
    x = x.reshape(T, D_MODEL)
    mem2 = mem.reshape(batch * N_MEM, D_MODEL)
    rope_tabs = _rope_tables(seq)
    for l in range(DEPTH):
        qa, ka, va, mk, mqt, mvt, mot, gp, rn = _in_proj(
            x, wts["w_rows"][l], wts["w_t"][l], wts["gb"][l], wts["conv_wk"][l], wts["conv_bk"][l],
            wts["conv_q"][l], rope_tabs, consts["tri_u"], consts["tri_l"], seq)
        att = _attention(qa, ka, va, wts["sink"][l], wts["att_g"][l], batch, seq)
        hft, hbt = _mlstm(mqt, mvt, mk, rn, gp, batch, seq)
        k_mem, v_mem = _kv_proj(mem2, wts["wkv"][l])
        x, info, rows = _mixer_out_xattn(
            x, att, hft, hbt, mot, wts["mlstm_g"][l], wts["w_out_a"][l], wts["w_out_m"][l], wts["ln1_g"][l],
            wts["ln1_b"][l], k_mem, v_mem, wts["wq"][l], wts["wo"][l], wts["ln2_g"][l], wts["ln2_b"][l],
            wts["router_wt"], wts["router_b"], consts["tri_strict"], consts["low_strict"], batch, seq)
        x = _moe(x, info, rows, consts, wts["w_gate"], wts["w_up"], wts["w_down"], l,
                 wts["ln3_g"][l], wts["ln3_b"][l])
    return x.reshape(batch, seq, D_MODEL)


def _hi_lo_rows(w):
    hi = w.astype(BF16)
    return jnp.concatenate([hi, (w - hi.astype(F32)).astype(BF16)], axis=0)


def _prepare_weights(w_in, gate_bias, conv_w, conv_b, attn_sink, attn_norm_g, mlstm_norm_g, w_out, ln1_g, ln1_b,
                     wq_mem, wkv_mem, wo_mem, ln2_g, ln2_b, router_w, router_bias, w_gate, w_up, w_down, ln3_g, ln3_b):
    row = lambda t: t.astype(F32).reshape(DEPTH, 1, t.shape[-1])
    nh = N_MLSTM_HEADS
    gate_order = jnp.array(list(range(0, nh)) + list(range(2 * nh, 3 * nh))
                           + list(range(nh, 2 * nh)) + list(range(3 * nh, 4 * nh)), jnp.int32)
    cq = jnp.concatenate([jnp.swapaxes(conv_w[:, :, :MLSTM_WIDTH], 1, 2), conv_b[:, :MLSTM_WIDTH, None],
                          jnp.zeros((DEPTH, MLSTM_WIDTH, 4), conv_w.dtype)], axis=2).astype(F32)
    feature_major = jnp.concatenate([w_in[:, :, OFF_MQ:OFF_MQ + MLSTM_WIDTH], w_in[:, :, OFF_MV:OFF_G],
                                     w_in[:, :, OFF_G:][:, :, gate_order]], axis=2)
    return {
        "w_rows": jnp.concatenate([w_in[:, :, :OFF_MQ], w_in[:, :, OFF_MQ + MLSTM_WIDTH:OFF_MV]],
                                  axis=2).astype(BF16),
        "w_t": jnp.swapaxes(feature_major, 1, 2).astype(BF16),
        "gb": gate_bias.astype(F32)[:, gate_order].reshape(DEPTH, N_GATE_COLS, 1),
        "conv_wk": conv_w[:, :, MLSTM_WIDTH:].astype(F32),
        "conv_bk": row(conv_b[:, MLSTM_WIDTH:]),
        "conv_q": cq,
        "sink": attn_sink.astype(F32),
        "att_g": row(attn_norm_g),
        "mlstm_g": mlstm_norm_g.astype(F32).reshape(DEPTH, MLSTM_WIDTH, 1),
        "w_out_a": w_out[:, :ATT_WIDTH].astype(BF16),
        "w_out_m": w_out[:, ATT_WIDTH:].astype(BF16),
        "ln1_g": row(ln1_g), "ln1_b": row(ln1_b),
        "wq": wq_mem.astype(BF16), "wkv": wkv_mem.astype(BF16), "wo": wo_mem.astype(BF16),
        "ln2_g": row(ln2_g), "ln2_b": row(ln2_b),
        "router_wt": _hi_lo_rows(router_w.astype(F32).T),
        "router_b": router_bias.astype(F32).reshape(N_EXPERTS, 1),
        "w_gate": w_gate.astype(BF16), "w_up": w_up.astype(BF16), "w_down": w_down.astype(BF16),
        "ln3_g": row(ln3_g), "ln3_b": row(ln3_b),
    }


def kernel(x_prompt, x_sample, mem_prompt, mem_sample, w_in, gate_bias, conv_w, conv_b, attn_sink, attn_norm_g, mlstm_norm_g, w_out, ln1_g, ln1_b, wq_mem, wkv_mem, wo_mem, ln2_g, ln2_b, router_w, router_bias, w_gate, w_up, w_down, ln3_g, ln3_b):
    wts = _prepare_weights(w_in, gate_bias, conv_w, conv_b, attn_sink, attn_norm_g, mlstm_norm_g, w_out, ln1_g, ln1_b,
                           wq_mem, wkv_mem, wo_mem, ln2_g, ln2_b, router_w, router_bias, w_gate, w_up, w_down,
                           ln3_g, ln3_b)
    consts = _constants()
    return (_trunk(x_prompt, mem_prompt, wts, consts), _trunk(x_sample, mem_sample, wts, consts))
```
